```python
import numpy as np
import jax
import jax.numpy as jnp
from jax import lax

D_MODEL = 4096
BATCH = 2
SEQ = 8192
DEPTH = 1

N_MEM = 256
N_BRANCH = 3
BRANCH_W = D_MODEL // 2
GLA_HEADS = 4
GLA_DV = BRANCH_W // GLA_HEADS
GLA_DK = GLA_DV // 2
GLA_K_W = GLA_HEADS * GLA_DK
GLA_LOWRANK = 16
GLA_TAU = 16.0
GLA_CHUNK = 64
NSA_HEADS = 16
NSA_GROUPS = 4
NSA_REP = NSA_HEADS // NSA_GROUPS
NSA_HD = BRANCH_W // NSA_HEADS
NSA_KV_W = NSA_GROUPS * NSA_HD
CMP_LEN = 32
CMP_STRIDE = 16
SEL_LEN = 64
SEL_TOPK = 16
WINDOW = 512
Q_BLOCK = 128
FORCE_SCORE = 1e4
MEM_HEADS = 4
MEM_HD = BRANCH_W // MEM_HEADS
LN_EPS = 1e-5
RMS_EPS = 1e-6
NEG_INF = -1e30
DEEPNORM_ALPHA = (2 * DEPTH) ** 0.25
DEEPNORM_BETA = (8 * DEPTH) ** -0.25

IN_SIZES = (
    GLA_K_W, GLA_K_W, BRANCH_W, BRANCH_W, GLA_LOWRANK,
    BRANCH_W, NSA_KV_W, NSA_KV_W, NSA_KV_W, NSA_KV_W, NSA_KV_W, NSA_KV_W,
    BRANCH_W, 3 * NSA_HEADS,
    BRANCH_W, BRANCH_W,
    N_BRANCH * D_MODEL,
)
D_IN = sum(IN_SIZES)

kernel_name = 'hybrid_gla_nsa_memory_deepnorm'


def alibi_slopes(n):
    return jnp.exp2(-8.0 * (jnp.arange(n, dtype=jnp.float32) + 1.0) / n)


def layer_norm(z, g, b):
    zf = z.astype(jnp.float32)
    mu = jnp.mean(zf, -1, keepdims=True)
    zc = zf - mu
    var = jnp.mean(zc * zc, -1, keepdims=True)
    return (zc * lax.rsqrt(var + LN_EPS) * g + b).astype(z.dtype)


def gla_mixer(q, k, v, a_lr, w_a2, b_a, norm_g):
    B, S, _ = q.shape
    H, DK, DV, C = GLA_HEADS, GLA_DK, GLA_DV, GLA_CHUNK
    N = S // C
    f32 = jnp.float32
    log_a = jax.nn.log_sigmoid((a_lr @ w_a2 + b_a).astype(f32)) / GLA_TAU

    def to_chunks(t, d):
        return t.astype(f32).reshape(B, N, C, H, d).transpose(1, 0, 3, 2, 4)

    qc = to_chunks(q, DK) * DK ** -0.5
    kc = to_chunks(k, DK)
    vc = to_chunks(v, DV)
    ac = to_chunks(log_a, DK)
    causal = jnp.tril(jnp.ones((C, C), dtype=bool))

    def step(state, inp):
        qn, kn, vn, an = inp
        bcum = jnp.cumsum(an, axis=-2)
        b_last = bcum[..., -1:, :]
        q_d = qn * jnp.exp(bcum)
        k_d = kn * jnp.exp(-bcum)
        att = jnp.where(causal, jnp.einsum('bhcd,bhjd->bhcj', q_d, k_d), 0.0)
        o = att @ vn + jnp.einsum('bhcd,bhde->bhce', q_d, state)
        k_end = kn * jnp.exp(b_last - bcum)
        state = jnp.exp(b_last[..., 0, :])[..., None] * state + jnp.einsum('bhcd,bhce->bhde', k_end, vn)
        return state, o

    s0 = jnp.zeros((B, H, DK, DV), f32)
    _, o = lax.scan(step, s0, (qc, kc, vc, ac))
    o = o.transpose(1, 0, 3, 2, 4).reshape(B, S, H, DV)
    o = o * lax.rsqrt(jnp.mean(o * o, -1, keepdims=True) + RMS_EPS) * norm_g.astype(f32)
    return o.reshape(B, S, H * DV)


def nsa_mixer(q, kc, vc, ks, vs, kw, vw, gate_logits, pe_k, pe_v, wk1, wk2, wv1, wv2):
    B, S, _ = q.shape
    G, R, HD = NSA_GROUPS, NSA_REP, NSA_HD
    f32 = jnp.float32
    q = q.reshape(B, S, G, R, HD) * HD ** -0.5
    kc, vc, ks, vs, kw, vw = [t.reshape(B, S, G, HD) for t in (kc, vc, ks, vs, kw, vw)]
    slopes = alibi_slopes(NSA_HEADS).reshape(G, R)

    n_cmp = (S - CMP_LEN) // CMP_STRIDE + 1
    tok_idx = jnp.arange(n_cmp)[:, None] * CMP_STRIDE + jnp.arange(CMP_LEN)[None, :]
    cmp_start = tok_idx[:, 0]
    cmp_end = tok_idx[:, -1]

    def compress(t, pe, w1, w2):
        blk = t[:, tok_idx] + pe[None, None, :, None, :]
        blk = blk.transpose(0, 1, 3, 2, 4).reshape(B, n_cmp, G, CMP_LEN * HD)
        return jax.nn.silu(blk @ w1) @ w2

    k_cmp = compress(kc, pe_k, wk1, wk2)
    v_cmp = compress(vc, pe_v, wv1, wv2)

    n_sel = S // SEL_LEN
    top_k = min(SEL_TOPK, n_sel)
    ks_blk = ks.reshape(B, n_sel, SEL_LEN, G, HD).transpose(0, 3, 1, 2, 4)
    vs_blk = vs.reshape(B, n_sel, SEL_LEN, G, HD).transpose(0, 3, 1, 2, 4)
    sel_start = jnp.arange(n_sel) * SEL_LEN
    overlap = ((cmp_start[:, None] < sel_start[None, :] + SEL_LEN)
               & (cmp_end[:, None] >= sel_start[None, :])).astype(f32)
    blk_ids = jnp.arange(n_sel)

    pad = ((0, 0), (WINDOW, 0), (0, 0), (0, 0))
    kw_p = jnp.pad(kw, pad)
    vw_p = jnp.pad(vw, pad)

    b_ix = jnp.arange(B)[:, None, None, None]
    g_ix = jnp.arange(G)[None, :, None, None]

    def attend_block(qb):
        start = qb * Q_BLOCK
        t = start + jnp.arange(Q_BLOCK)
        qq = lax.dynamic_slice_in_dim(q, start, Q_BLOCK, axis=1)

        dist_c = t[:, None] - cmp_end[None, :]
        ok_c = dist_c >= 0
        s_c = (jnp.einsum('bqgrd,bngd->bgrqn', qq, k_cmp).astype(f32)
               - slopes[:, :, None, None] * dist_c.astype(f32))
        s_c = jnp.where(ok_c, s_c, NEG_INF)
        p_c = jax.nn.softmax(s_c, axis=-1) * jnp.any(ok_c, -1)[:, None]
        o_cmp = jnp.einsum('bgrqn,bngd->bqgrd', p_c.astype(v_cmp.dtype), v_cmp)

        imp = jnp.einsum('bgrqn,nj->bgqj', p_c, overlap)
        cur = t // SEL_LEN
        forced = ((blk_ids[None, :] == 0) | (blk_ids[None, :] == cur[:, None])
                  | (blk_ids[None, :] == cur[:, None] - 1))
        causal_blk = sel_start[None, :] <= t[:, None]
        score = jnp.where(forced, FORCE_SCORE, jnp.where(causal_blk, imp, -1.0))
        _, sel = lax.top_k(score, top_k)
        k_g = ks_blk[b_ix, g_ix, sel]
        v_g = vs_blk[b_ix, g_ix, sel]
        pos = sel[..., None] * SEL_LEN + jnp.arange(SEL_LEN)
        dist_s = (t[None, None, :, None, None] - pos)[:, :, None]
        s_s = (jnp.einsum('bqgrd,bgqkld->bgrqkl', qq, k_g).astype(f32)
               - slopes[None, :, :, None, None, None] * dist_s.astype(f32))
        s_s = jnp.where(dist_s >= 0, s_s, NEG_INF).reshape(B, G, R, Q_BLOCK, top_k * SEL_LEN)
        p_s = jax.nn.softmax(s_s, axis=-1).reshape(B, G, R, Q_BLOCK, top_k, SEL_LEN)
        o_sel = jnp.einsum('bgrqkl,bgqkld->bqgrd', p_s.astype(v_g.dtype), v_g)

        kk = lax.dynamic_slice_in_dim(kw_p, start, Q_BLOCK + WINDOW, axis=1)
        vv = lax.dynamic_slice_in_dim(vw_p, start, Q_BLOCK + WINDOW, axis=1)
        s_pos = start - WINDOW + jnp.arange(Q_BLOCK + WINDOW)
        dist_w = t[:, None] - s_pos[None, :]
        ok_w = (dist_w >= 0) & (dist_w < WINDOW) & (s_pos[None, :] >= 0)
        s_w = (jnp.einsum('bqgrd,bkgd->bgrqk', qq, kk).astype(f32)
               - slopes[:, :, None, None] * dist_w.astype(f32))
        s_w = jnp.where(ok_w, s_w, NEG_INF)
        p_w = jax.nn.softmax(s_w, axis=-1)
        o_win = jnp.einsum('bgrqk,bkgd->bqgrd', p_w.astype(vv.dtype), vv)
        return o_cmp, o_sel, o_win

    o_cmp, o_sel, o_win = lax.map(attend_block, jnp.arange(S // Q_BLOCK))

    def unblock(o):
        return o.transpose(1, 0, 2, 3, 4, 5).reshape(B, S, NSA_HEADS, HD)

    g = jax.nn.sigmoid(gate_logits).reshape(B, S, NSA_HEADS, 3)
    o = (g[..., 0:1] * unblock(o_cmp) + g[..., 1:2] * unblock(o_sel)
         + g[..., 2:3] * unblock(o_win))
    return o.reshape(B, S, NSA_HEADS * HD)


def memory_attention(q, mem, w_kv):
    B, S, _ = q.shape
    M = mem.shape[1]
    kv = (mem @ w_kv).reshape(B, M, 2, MEM_HEADS, MEM_HD)
    k, v = kv[:, :, 0], kv[:, :, 1]
    q = q.reshape(B, S, MEM_HEADS, MEM_HD) * MEM_HD ** -0.5
    s = jnp.einsum('bshd,bmhd->bhsm', q, k).astype(jnp.float32)
    p = jax.nn.softmax(s, axis=-1).astype(v.dtype)
    return jnp.einsum('bhsm,bmhd->bshd', p, v).reshape(B, S, MEM_HEADS * MEM_HD)


def hybrid_layer(x, mem, w_in, b_merge, gla_w_a2, gla_b_a, gla_norm_g, nsa_pe_k, nsa_pe_v,
                 nsa_wk1, nsa_wk2, nsa_wv1, nsa_wv2, w_mem_kv, w_br_gla, w_br_nsa, w_br_mem,
                 w_out, ln_g, ln_b):
    B, S, D = x.shape
    h = x @ w_in
    (gq, gk, gv, gz, ga, nq, nkc, nvc, nks, nvs, nkw, nvw, nz, nbg, mq, mz, mrg) = jnp.split(
        h, np.cumsum(IN_SIZES)[:-1].tolist(), axis=-1)
    o_gla = gla_mixer(gq, gk, gv, ga, gla_w_a2, gla_b_a, gla_norm_g).astype(x.dtype)
    y_gla = (o_gla * jax.nn.silu(gz)) @ w_br_gla
    o_nsa = nsa_mixer(nq, nkc, nvc, nks, nvs, nkw, nvw, nbg, nsa_pe_k, nsa_pe_v,
                      nsa_wk1, nsa_wk2, nsa_wv1, nsa_wv2)
    y_nsa = (o_nsa * jax.nn.silu(nz)) @ w_br_nsa
    o_mem = memory_attention(mq, mem, w_mem_kv)
    y_mem = (o_mem * jax.nn.silu(mz)) @ w_br_mem
    a = jax.nn.sigmoid(mrg + b_merge).reshape(B, S, N_BRANCH, D)
    merged = a[:, :, 0] * y_gla + a[:, :, 1] * y_nsa + a[:, :, 2] * y_mem
    out = merged @ w_out
    return layer_norm(DEEPNORM_ALPHA * x + out, ln_g, ln_b)


def setup_inputs(seed: int = 0) -> dict:
    key = jax.random.key(seed)
    ks = jax.random.split(key, 20)
    L, D = DEPTH, D_MODEL

    def nrm(k, shape, scale):
        return jax.random.normal(k, shape, jnp.float32) * scale

    return {
        'x': nrm(ks[0], (BATCH, SEQ, D), 1.0),
        'mem': nrm(ks[1], (BATCH, N_MEM, D), 1.0),
        'w_in': nrm(ks[2], (L, D, D_IN), D ** -0.5),
        'b_merge': nrm(ks[3], (L, N_BRANCH * D), 0.01),
        'gla_w_a2': nrm(ks[4], (L, GLA_LOWRANK, GLA_K_W), GLA_LOWRANK ** -0.5),
        'gla_b_a': nrm(ks[5], (L, GLA_K_W), 0.1),
        'gla_norm_g': 1.0 + nrm(ks[6], (L, GLA_DV), 0.02),
        'nsa_pe_k': nrm(ks[7], (L, CMP_LEN, NSA_HD), 0.1),
        'nsa_pe_v': nrm(ks[8], (L, CMP_LEN, NSA_HD), 0.1),
        'nsa_wk1': nrm(ks[9], (L, CMP_LEN * NSA_HD, NSA_HD), (CMP_LEN * NSA_HD) ** -0.5),
        'nsa_wk2': nrm(ks[10], (L, NSA_HD, NSA_HD), NSA_HD ** -0.5),
        'nsa_wv1': nrm(ks[11], (L, CMP_LEN * NSA_HD, NSA_HD), (CMP_LEN * NSA_HD) ** -0.5),
        'nsa_wv2': nrm(ks[12], (L, NSA_HD, NSA_HD), NSA_HD ** -0.5),
        'w_mem_kv': nrm(ks[13], (L, D, 2 * MEM_HEADS * MEM_HD), D ** -0.5),
        'w_br_gla': nrm(ks[14], (L, BRANCH_W, D), BRANCH_W ** -0.5 * DEEPNORM_BETA),
        'w_br_nsa': nrm(ks[15], (L, BRANCH_W, D), BRANCH_W ** -0.5 * DEEPNORM_BETA),
        'w_br_mem': nrm(ks[16], (L, BRANCH_W, D), BRANCH_W ** -0.5 * DEEPNORM_BETA),
        'w_out': nrm(ks[17], (L, D, D), D ** -0.5 * DEEPNORM_BETA),
        'ln_g': 1.0 + nrm(ks[18], (L, D), 0.02),
        'ln_b': nrm(ks[19], (L, D), 0.02),
    }


def reference(x, mem, w_in, b_merge, gla_w_a2, gla_b_a, gla_norm_g, nsa_pe_k, nsa_pe_v,
              nsa_wk1, nsa_wk2, nsa_wv1, nsa_wv2, w_mem_kv, w_br_gla, w_br_nsa, w_br_mem,
              w_out, ln_g, ln_b):
    for l in range(DEPTH):
        x = hybrid_layer(x, mem, w_in[l], b_merge[l], gla_w_a2[l], gla_b_a[l], gla_norm_g[l],
                         nsa_pe_k[l], nsa_pe_v[l], nsa_wk1[l], nsa_wk2[l], nsa_wv1[l], nsa_wv2[l],
                         w_mem_kv[l], w_br_gla[l], w_br_nsa[l], w_br_mem[l], w_out[l],
                         ln_g[l], ln_b[l])
    return x
```

```python
import functools

import jax
import jax.numpy as jnp
from jax import lax
from jax.experimental import pallas as pl
from jax.experimental.pallas import tpu as pltpu

N_BRANCH = 3
GLA_HEADS = 4
GLA_LOWRANK = 16
GLA_TAU = 16.0
GLA_CHUNK = 64
NSA_HEADS = 16
NSA_GROUPS = 4
NSA_REP = NSA_HEADS // NSA_GROUPS
CMP_LEN = 32
CMP_STRIDE = 16
SEL_LEN = 64
SEL_TOPK = 16
WINDOW = 512
Q_BLOCK = 128
FORCE_SCORE = 1e4
MEM_HEADS = 4
LN_EPS = 1e-5
RMS_EPS = 1e-6
NEG_INF = -1e30

LANES = 128
VMEM_LIMIT_BYTES = 56 * 1024 * 1024
MXU_DTYPE = jnp.bfloat16

SEL_KV_TILE = 512
GLA_STEP_CHUNKS = 4
LN_ROWS = 64
SMALL_W = LANES

f32 = jnp.float32


def _dot(a, b):
    return jnp.dot(a, b, preferred_element_type=f32)


def _dot_nt(a, b):
    return lax.dot_general(a, b, (((1,), (1,)), ((), ())), preferred_element_type=f32)


def _dot_tn(a, b):
    return lax.dot_general(a, b, (((0,), (0,)), ((), ())), preferred_element_type=f32)


def _sigmoid(x):
    return 1.0 / (1.0 + jnp.exp(-x))


def _silu(x):
    return x * _sigmoid(x)


def _log_sigmoid(x):
    return -(jnp.maximum(-x, 0.0) + jnp.log1p(jnp.exp(-jnp.abs(x))))


def _split2(x):
    hi = x.astype(MXU_DTYPE)
    lo = (x - hi.astype(f32)).astype(MXU_DTYPE)
    return hi, lo


def _split3(x):
    hi = x.astype(MXU_DTYPE)
    r1 = x - hi.astype(f32)
    mid = r1.astype(MXU_DTYPE)
    lo = (r1 - mid.astype(f32)).astype(MXU_DTYPE)
    return hi, mid, lo


def _params(*sem):
    return pltpu.CompilerParams(dimension_semantics=sem, vmem_limit_bytes=VMEM_LIMIT_BYTES)


def _proj_scale_kernel(x_ref, w_ref, s_ref, o_ref):
    acc = _dot(x_ref[...], w_ref[...])
    o_ref[...] = (acc * s_ref[...]).astype(o_ref.dtype)


def _proj_gate_kernel(x_ref, w_ref, b_ref, o_ref):
    acc = _dot(x_ref[...], w_ref[...])
    o_ref[...] = _sigmoid(acc + b_ref[...]).astype(o_ref.dtype)


def _project(x, w, row, out_dtype, gate=False, name="proj"):
    M, K = x.shape
    N = w.shape[1]
    bm = min(1024, M)
    bn = min(1024, N)
    assert M % bm == 0 and N % bn == 0
    return pl.pallas_call(
        _proj_gate_kernel if gate else _proj_scale_kernel,
        out_shape=jax.ShapeDtypeStruct((M, N), out_dtype),
        grid=(N // bn, M // bm),
        in_specs=[
            pl.BlockSpec((bm, K), lambda j, i: (i, 0)),
            pl.BlockSpec((K, bn), lambda j, i: (0, j)),
            pl.BlockSpec((1, bn), lambda j, i: (0, j)),
        ],
        out_specs=pl.BlockSpec((bm, bn), lambda j, i: (i, j)),
        compiler_params=_params("parallel", "parallel"),
        name=name,
    )(x, w, row.reshape(1, N).astype(f32))


def _gla_kernel(q_ref, k_ref, v_ref, z_ref, ga_ref, wa_ref, ba_ref, ng_ref, o_ref, st_ref, *, dk, dv):
    C = GLA_CHUNK

    @pl.when(pl.program_id(1) == 0)
    def _():
        st_ref[...] = jnp.zeros_like(st_ref)

    row = lax.broadcasted_iota(jnp.int32, (C, C), 0)
    col = lax.broadcasted_iota(jnp.int32, (C, C), 1)
    tril = row >= col
    ltri = jnp.where(tril, 1.0, 0.0).astype(MXU_DTYPE)
    wa_hi, wa_lo = _split2(wa_ref[...])
    for c in range(GLA_STEP_CHUNKS):
        rows = slice(c * C, (c + 1) * C)
        ga_hi, ga_lo = _split2(ga_ref[rows, :])
        zz = _dot(ga_hi, wa_hi) + _dot(ga_lo, wa_hi) + _dot(ga_hi, wa_lo) + ba_ref[...]
        la = _log_sigmoid(zz) * (1.0 / GLA_TAU)
        la_hi, la_mid, la_lo = _split3(la)
        bcum = _dot(ltri, la_hi) + _dot(ltri, la_mid) + _dot(ltri, la_lo)
        for h in range(GLA_HEADS):
            kc = slice(h * dk, (h + 1) * dk)
            vc = slice(h * dv, (h + 1) * dv)
            b = bcum[:, kc]
            bl = b[C - 1:C, :]
            qh = q_ref[rows, kc].astype(f32)
            kh = k_ref[rows, kc].astype(f32)
            vh = v_ref[rows, vc]
            q_d = (qh * jnp.exp(b)).astype(MXU_DTYPE)
            k_d = (kh * jnp.exp(-b)).astype(MXU_DTYPE)
            k_e = (kh * jnp.exp(bl - b)).astype(MXU_DTYPE)
            att = jnp.where(tril, _dot_nt(q_d, k_d), 0.0)
            st = st_ref[h]
            o = _dot(att.astype(MXU_DTYPE), vh) + _dot_nt(q_d, st.astype(MXU_DTYPE))
            st_ref[h] = st * jnp.exp(bl) + _dot_tn(vh, k_e)
            ms = jnp.mean(o * o, axis=-1, keepdims=True)
            on = o * lax.rsqrt(ms + RMS_EPS) * ng_ref[...]
            zg = z_ref[rows, vc].astype(f32)
            o_ref[rows, vc] = (on * _silu(zg)).astype(o_ref.dtype)


def _gla(h_gla, h_small, wa_pad, b_a, norm_g, B, S):
    T = B * S
    kw = h_gla.shape[1] // 6
    dk, dv = kw // GLA_HEADS, 2 * kw // GLA_HEADS
    cs = GLA_STEP_CHUNKS * GLA_CHUNK
    nb = S // cs
    assert S % cs == 0
    rowmap = lambda col: (lambda b, i: (b * nb + i, col))
    return pl.pallas_call(
        functools.partial(_gla_kernel, dk=dk, dv=dv),
        out_shape=jax.ShapeDtypeStruct((T, 2 * kw), MXU_DTYPE),
        grid=(B, nb),
        in_specs=[
            pl.BlockSpec((cs, kw), rowmap(0)),
            pl.BlockSpec((cs, kw), rowmap(1)),
            pl.BlockSpec((cs, 2 * kw), rowmap(1)),
            pl.BlockSpec((cs, 2 * kw), rowmap(2)),
            pl.BlockSpec((cs, SMALL_W), rowmap(0)),
            pl.BlockSpec((SMALL_W, kw), lambda b, i: (0, 0)),
            pl.BlockSpec((1, kw), lambda b, i: (0, 0)),
            pl.BlockSpec((1, dv), lambda b, i: (0, 0)),
        ],
        out_specs=pl.BlockSpec((cs, 2 * kw), rowmap(0)),
        scratch_shapes=[pltpu.VMEM((GLA_HEADS, dv, dk), f32)],
        compiler_params=_params("parallel", "arbitrary"),
        name="gla",
    )(h_gla, h_gla, h_gla, h_gla, h_small, wa_pad, b_a.reshape(1, kw), norm_g.reshape(1, dv))


def _compress_kernel(x_ref, pe_ref, w1_ref, w2_ref, o_ref):
    x = x_ref[0, 0].astype(f32)
    half = x.shape[1]
    xa = (x + pe_ref[0:1, :]).astype(MXU_DTYPE)
    xb = (x + pe_ref[1:2, :]).astype(MXU_DTYPE)
    ya = _dot(xa, w1_ref[0:half, :])
    yb = _dot(xb, w1_ref[half:2 * half, :])
    nc = x.shape[0]
    pre = ya + pltpu.roll(yb, nc - 1, 0)
    o_ref[0, 0] = _dot(_silu(pre).astype(MXU_DTYPE), w2_ref[...]).astype(o_ref.dtype)


def _compress(xblk, pe, w1, w2):
    B, G, NC, W = xblk.shape
    hd = w2.shape[0]
    return pl.pallas_call(
        _compress_kernel,
        out_shape=jax.ShapeDtypeStruct((B, G, NC, hd), MXU_DTYPE),
        grid=(B, G),
        in_specs=[
            pl.BlockSpec((1, 1, NC, W), lambda b, g: (b, g, 0, 0)),
            pl.BlockSpec((2, W), lambda b, g: (0, 0)),
            pl.BlockSpec((2 * W, hd), lambda b, g: (0, 0)),
            pl.BlockSpec((hd, hd), lambda b, g: (0, 0)),
        ],
        out_specs=pl.BlockSpec((1, 1, NC, hd), lambda b, g: (b, g, 0, 0)),
        compiler_params=_params("parallel", "parallel"),
        name="nsa_compress",
    )(xblk, pe.reshape(2, W).astype(f32), w1.astype(MXU_DTYPE), w2.astype(MXU_DTYPE))


def _softmax_cols(s, ok):
    s = jnp.where(ok, s, NEG_INF)
    m = jnp.max(s, axis=0, keepdims=True)
    e = jnp.where(ok, jnp.exp(s - m), 0.0)
    den = jnp.sum(e, axis=0, keepdims=True)
    return e * jnp.where(den > 0.0, 1.0 / den, 0.0)


def _nsa_kernel(q_ref, kc_ref, vct_ref, ks_ref, vst_ref, kw_ref, vwt_ref, gt_ref, sl_ref, z_ref,
                o_ref, sel_ref, *, S):
    Q, R = Q_BLOCK, NSA_REP
    HD = q_ref.shape[1] // R
    RQ = R * Q
    NC = S // CMP_STRIDE
    NS = S // SEL_LEN
    qb = pl.program_id(2)
    start = qb * Q

    q = q_ref[...]
    qT = jnp.concatenate([q[:, r * HD:(r + 1) * HD].T for r in range(R)], axis=1)
    slope = sl_ref[0]
    lane = lax.broadcasted_iota(jnp.int32, (1, RQ), 1)
    tq = start + jnp.bitwise_and(lane, Q - 1)

    n_idx = lax.broadcasted_iota(jnp.int32, (NC, RQ), 0)
    dist_c = tq - (n_idx * CMP_STRIDE + (CMP_LEN - 1))
    ok_c = dist_c >= 0
    s_c = _dot(kc_ref[0, 0], qT) - slope * dist_c.astype(f32)
    p_c = _softmax_cols(s_c, ok_c)
    o_cmp = _dot(vct_ref[0, 0], p_c.astype(MXU_DTYPE))

    p_sum = p_c[:, 0:Q]
    for r in range(1, R):
        p_sum = p_sum + p_c[:, r * Q:(r + 1) * Q]
    jj = lax.broadcasted_iota(jnp.int32, (NS, NC), 0) * SEL_LEN
    nn = lax.broadcasted_iota(jnp.int32, (NS, NC), 1) * CMP_STRIDE
    ov = jnp.where((nn < jj + SEL_LEN) & (nn + (CMP_LEN - 1) >= jj), 1.0, 0.0).astype(MXU_DTYPE)
    ps_hi, ps_mid, ps_lo = _split3(p_sum)
    imp = _dot(ov, ps_hi) + _dot(ov, ps_mid) + _dot(ov, ps_lo)

    blk = lax.broadcasted_iota(jnp.int32, (NS, Q), 0)
    tq1 = start + lax.broadcasted_iota(jnp.int32, (NS, Q), 1)
    cur = jnp.right_shift(tq1, SEL_LEN.bit_length() - 1)
    forced = (blk == 0) | (blk == cur) | (blk == cur - 1)
    score = jnp.where(forced, FORCE_SCORE, jnp.where(blk * SEL_LEN <= tq1, imp, -1.0))
    blk_f = blk.astype(f32)

    def pick(_, carry):
        sc, sel = carry
        m = jnp.max(sc, axis=0, keepdims=True)
        first = jnp.min(jnp.where(sc == m, blk_f, float(NS)), axis=0, keepdims=True)
        hit = blk_f == first
        return jnp.where(hit, -jnp.inf, sc), jnp.where(hit, 1.0, sel)

    _, sel = lax.fori_loop(0, min(SEL_TOPK, NS), pick, (score, jnp.zeros((NS, Q), f32)))
    sel_ref[...] = sel

    KT = SEL_KV_TILE
    BPT = KT // SEL_LEN
    kq = (lax.broadcasted_iota(jnp.int32, (KT, Q), 1) - lax.broadcasted_iota(jnp.int32, (KT, Q), 0))

    def sel_tile(i, carry):
        m, l, acc = carry
        k0 = pl.multiple_of(i * KT, KT)
        b0 = pl.multiple_of(i * BPT, BPT)
        sT = _dot(ks_ref[pl.ds(k0, KT), :], qT)
        dist = kq + (start - k0)
        selt = sel_ref[pl.ds(b0, BPT), :]
        selm = jnp.concatenate(
            [jnp.broadcast_to(selt[b:b + 1, :], (SEL_LEN, Q)) for b in range(BPT)], axis=0)
        ok = (dist >= 0) & (selm > 0.5)
        dist_f = dist.astype(f32)
        s_parts = []
        for r in range(R):
            s_r = sT[:, r * Q:(r + 1) * Q] - slope[:, r * Q:(r + 1) * Q] * dist_f
            s_parts.append(jnp.where(ok, s_r, NEG_INF))
        s = jnp.concatenate(s_parts, axis=1)
        m_new = jnp.maximum(m, jnp.max(s, axis=0, keepdims=True))
        p = jnp.exp(s - m_new)
        alpha = jnp.exp(m - m_new)
        l_new = alpha * l + jnp.sum(p, axis=0, keepdims=True)
        acc_new = alpha * acc + _dot(vst_ref[0, 0, :, pl.ds(k0, KT)], p.astype(MXU_DTYPE))
        return m_new, l_new, acc_new

    n_tiles = (start + Q - 1) // KT + 1
    m0 = jnp.full((1, RQ), NEG_INF, f32)
    _, l_s, acc_s = lax.fori_loop(0, n_tiles, sel_tile,
                                  (m0, jnp.zeros((1, RQ), f32), jnp.zeros((HD, RQ), f32)))
    o_sel = acc_s * (1.0 / l_s)

    WK = WINDOW + Q
    ws = pl.multiple_of(jnp.maximum(start - WINDOW, 0), Q)
    pos_w = ws + lax.broadcasted_iota(jnp.int32, (WK, RQ), 0)
    dist_w = tq - pos_w
    ok_w = (dist_w >= 0) & (dist_w < WINDOW)
    s_w = _dot(kw_ref[pl.ds(ws, WK), :], qT) - slope * dist_w.astype(f32)
    p_w = _softmax_cols(s_w, ok_w)
    o_win = _dot(vwt_ref[0, 0, :, pl.ds(ws, WK)], p_w.astype(MXU_DTYPE))

    gates = _sigmoid(gt_ref[0])
    for r in range(R):
        cs = slice(r * Q, (r + 1) * Q)
        o_r = (gates[3 * r:3 * r + 1, :] * o_cmp[:, cs] + gates[3 * r + 1:3 * r + 2, :] * o_sel[:, cs]
               + gates[3 * r + 2:3 * r + 3, :] * o_win[:, cs])
        hs = slice(r * HD, (r + 1) * HD)
        o_ref[:, hs] = (o_r.T * _silu(z_ref[:, hs].astype(f32))).astype(o_ref.dtype)


def _nsa_attention(h_nsa, k_cmp, v_cmp_t, vs_t, vw_t, gates_t, slopes, B, S):
    T = B * S
    G, R = NSA_GROUPS, NSA_REP
    bw = h_nsa.shape[1] * 2 // 7
    HD = bw // NSA_HEADS
    NQ = S // Q_BLOCK
    NC = S // CMP_STRIDE
    NS = S // SEL_LEN
    RQ = R * Q_BLOCK
    kv0 = bw // HD
    assert S % SEL_KV_TILE == 0 and S >= WINDOW + Q_BLOCK
    return pl.pallas_call(
        functools.partial(_nsa_kernel, S=S),
        out_shape=jax.ShapeDtypeStruct((T, bw), MXU_DTYPE),
        grid=(B, G, NQ),
        in_specs=[
            pl.BlockSpec((Q_BLOCK, R * HD), lambda b, g, i: (b * NQ + i, g)),
            pl.BlockSpec((1, 1, NC, HD), lambda b, g, i: (b, g, 0, 0)),
            pl.BlockSpec((1, 1, HD, NC), lambda b, g, i: (b, g, 0, 0)),
            pl.BlockSpec((S, HD), lambda b, g, i: (b, kv0 + 2 * G + g)),
            pl.BlockSpec((1, 1, HD, S), lambda b, g, i: (b, g, 0, 0)),
            pl.BlockSpec((S, HD), lambda b, g, i: (b, kv0 + 4 * G + g)),
            pl.BlockSpec((1, 1, HD, S), lambda b, g, i: (b, g, 0, 0)),
            pl.BlockSpec((1, 16, Q_BLOCK), lambda b, g, i: (g, 0, b * NQ + i)),
            pl.BlockSpec((1, 1, RQ), lambda b, g, i: (g, 0, 0)),
            pl.BlockSpec((Q_BLOCK, R * HD), lambda b, g, i: (b * NQ + i, (bw + 6 * G * HD) // (R * HD) + g)),
        ],
        out_specs=pl.BlockSpec((Q_BLOCK, R * HD), lambda b, g, i: (b * NQ + i, g)),
        scratch_shapes=[pltpu.VMEM((NS, Q_BLOCK), f32)],
        compiler_params=_params("parallel", "parallel", "arbitrary"),
        name="nsa_attention",
    )(h_nsa, k_cmp, v_cmp_t, h_nsa, vs_t, h_nsa, vw_t, gates_t, slopes, h_nsa)


def _mem_kernel(q_ref, z_ref, kv_ref, o_ref):
    hw = q_ref.shape[1] // MEM_HEADS
    bw = q_ref.shape[1]
    for h in range(MEM_HEADS):
        cs = slice(h * hw, (h + 1) * hw)
        s = _dot_nt(q_ref[:, cs], kv_ref[:, cs])
        m = jnp.max(s, axis=-1, keepdims=True)
        e = jnp.exp(s - m)
        p = e * (1.0 / jnp.sum(e, axis=-1, keepdims=True))
        o = _dot(p.astype(MXU_DTYPE), kv_ref[:, bw + h * hw:bw + (h + 1) * hw])
        o_ref[:, cs] = (o * _silu(z_ref[:, cs].astype(f32))).astype(o_ref.dtype)


def _mem_attention(h_mem, kv, B, S):
    T = B * S
    bw = h_mem.shape[1] // 2
    M = kv.shape[0] // B
    tq = min(512, S)
    nb = S // tq
    return pl.pallas_call(
        _mem_kernel,
        out_shape=jax.ShapeDtypeStruct((T, bw), MXU_DTYPE),
        grid=(B, nb),
        in_specs=[
            pl.BlockSpec((tq, bw), lambda b, i: (b * nb + i, 0)),
            pl.BlockSpec((tq, bw), lambda b, i: (b * nb + i, 1)),
            pl.BlockSpec((M, 2 * bw), lambda b, i: (b, 0)),
        ],
        out_specs=pl.BlockSpec((tq, bw), lambda b, i: (b * nb + i, 0)),
        compiler_params=_params("parallel", "parallel"),
        name="mem_attention",
    )(h_mem, h_mem, kv)


def _merge_kernel(og_ref, on_ref, om_ref, wg_ref, wn_ref, wm_ref, ag_ref, an_ref, am_ref, o_ref):
    y = ag_ref[...].astype(f32) * _dot(og_ref[...], wg_ref[...])
    y = y + an_ref[...].astype(f32) * _dot(on_ref[...], wn_ref[...])
    y = y + am_ref[...].astype(f32) * _dot(om_ref[...], wm_ref[...])
    o_ref[...] = y.astype(o_ref.dtype)


def _merge(o_gla, o_nsa, o_mem, w_g, w_n, w_m, a):
    T, bw = o_gla.shape
    D = w_g.shape[1]
    tm, tn = min(512, T), min(1024, D)
    nj = D // tn
    osp = pl.BlockSpec((tm, bw), lambda j, i: (i, 0))
    wsp = pl.BlockSpec((bw, tn), lambda j, i: (0, j))
    asp = lambda c: pl.BlockSpec((tm, tn), lambda j, i: (i, c * nj + j))
    return pl.pallas_call(
        _merge_kernel,
        out_shape=jax.ShapeDtypeStruct((T, D), MXU_DTYPE),
        grid=(nj, T // tm),
        in_specs=[osp, osp, osp, wsp, wsp, wsp, asp(0), asp(1), asp(2)],
        out_specs=pl.BlockSpec((tm, tn), lambda j, i: (i, j)),
        compiler_params=_params("parallel", "parallel"),
        name="branch_merge",
    )(o_gla, o_nsa, o_mem, w_g, w_n, w_m, a, a, a)


def _out_ln_kernel(m_ref, w_ref, x_ref, g_ref, b_ref, o_ref, *, alpha, tn):
    j = pl.program_id(1)
    c0 = pl.multiple_of(j * tn, tn)
    o_ref[:, pl.ds(c0, tn)] = alpha * x_ref[...] + _dot(m_ref[...], w_ref[...])

    @pl.when(j == pl.num_programs(1) - 1)
    def _():
        def ln_rows(c, _):
            rows = pl.ds(pl.multiple_of(c * LN_ROWS, LN_ROWS), LN_ROWS)
            z = o_ref[rows, :]
            mu = jnp.mean(z, axis=-1, keepdims=True)
            zc = z - mu
            var = jnp.mean(zc * zc, axis=-1, keepdims=True)
            o_ref[rows, :] = zc * lax.rsqrt(var + LN_EPS) * g_ref[...] + b_ref[...]
            return 0

        lax.fori_loop(0, o_ref.shape[0] // LN_ROWS, ln_rows, 0)


def _out_ln(merged, w_out, x2, ln_g, ln_b, alpha):
    T, D = x2.shape
    tm, tn = min(512, T), min(1024, D)
    return pl.pallas_call(
        functools.partial(_out_ln_kernel, alpha=alpha, tn=tn),
        out_shape=jax.ShapeDtypeStruct((T, D), x2.dtype),
        grid=(T // tm, D // tn),
        in_specs=[
            pl.BlockSpec((tm, D), lambda i, j: (i, 0)),
            pl.BlockSpec((D, tn), lambda i, j: (0, j)),
            pl.BlockSpec((tm, tn), lambda i, j: (i, j)),
            pl.BlockSpec((1, D), lambda i, j: (0, 0)),
            pl.BlockSpec((1, D), lambda i, j: (0, 0)),
        ],
        out_specs=pl.BlockSpec((tm, D), lambda i, j: (i, 0)),
        compiler_params=_params("parallel", "arbitrary"),
        name="out_proj_layernorm",
    )(merged, w_out, x2, ln_g.reshape(1, D), ln_b.reshape(1, D))


def _layer(x, mem, w_in, b_merge, gla_w_a2, gla_b_a, gla_norm_g, nsa_pe_k, nsa_pe_v, nsa_wk1, nsa_wk2,
           nsa_wv1, nsa_wv2, w_mem_kv, w_br_gla, w_br_nsa, w_br_mem, w_out, ln_g, ln_b, depth):
    B, S, D = x.shape
    T = B * S
    bw = D // 2
    gk = bw // 2
    G, R = NSA_GROUPS, NSA_REP
    HD = bw // NSA_HEADS
    kvw = G * HD
    cdt = MXU_DTYPE

    o_ga = 2 * gk + 2 * bw
    o_nq = o_ga + GLA_LOWRANK
    o_nbg = o_nq + bw + 6 * kvw + bw
    o_mq = o_nbg + 3 * NSA_HEADS
    o_mrg = o_mq + 2 * bw
    assert w_in.shape[1] == o_mrg + N_BRANCH * D

    x2 = x.reshape(T, D)
    xb = x2.astype(cdt)
    ones = lambda n: jnp.ones((n,), f32)

    dk = gk // GLA_HEADS
    h_gla = _project(xb, w_in[:, :o_ga].astype(cdt),
                     jnp.concatenate([jnp.full((gk,), dk ** -0.5, f32), ones(o_ga - gk)]), cdt, name="proj_gla")
    npad = SMALL_W - GLA_LOWRANK - 3 * NSA_HEADS
    w_small = jnp.concatenate([w_in[:, o_ga:o_nq], w_in[:, o_nbg:o_mq], jnp.zeros((D, npad), f32)], axis=1)
    h_small = _project(xb, w_small.astype(cdt), ones(SMALL_W), f32, name="proj_small")
    h_nsa = _project(xb, w_in[:, o_nq:o_nbg].astype(cdt),
                     jnp.concatenate([jnp.full((bw,), HD ** -0.5, f32), ones(6 * kvw + bw)]), cdt, name="proj_nsa")
    mhd = bw // MEM_HEADS
    h_mem = _project(xb, w_in[:, o_mq:o_mrg].astype(cdt),
                     jnp.concatenate([jnp.full((bw,), mhd ** -0.5, f32), ones(bw)]), cdt, name="proj_mem")
    a = _project(xb, w_in[:, o_mrg:].astype(cdt), b_merge, cdt, gate=True, name="proj_merge_gates")

    wa_pad = jnp.concatenate([gla_w_a2, jnp.zeros((SMALL_W - GLA_LOWRANK, gk), f32)], axis=0)
    o_gla = _gla(h_gla, h_small, wa_pad, gla_b_a, gla_norm_g, B, S)

    def grouped(c0):
        return h_nsa[:, c0:c0 + kvw].reshape(B, S, G, HD).transpose(0, 2, 1, 3)

    NC = S // CMP_STRIDE
    c_kv = bw
    k_cmp = _compress(grouped(c_kv).reshape(B, G, NC, CMP_STRIDE * HD), nsa_pe_k, nsa_wk1, nsa_wk2)
    v_cmp = _compress(grouped(c_kv + kvw).reshape(B, G, NC, CMP_STRIDE * HD), nsa_pe_v, nsa_wv1, nsa_wv2)
    v_cmp_t = v_cmp.transpose(0, 1, 3, 2)
    vs_t = grouped(c_kv + 3 * kvw).transpose(0, 1, 3, 2)
    vw_t = grouped(c_kv + 5 * kvw).transpose(0, 1, 3, 2)
    gl = h_small[:, GLA_LOWRANK:GLA_LOWRANK + 3 * NSA_HEADS].reshape(T, G, 3 * R).transpose(1, 2, 0)
    gates_t = jnp.concatenate([gl, jnp.zeros((G, 16 - 3 * R, T), f32)], axis=1)
    slopes = jnp.exp2(-8.0 * (jnp.arange(NSA_HEADS, dtype=f32) + 1.0) / NSA_HEADS)
    slopes = jnp.repeat(slopes.reshape(G, 1, R), Q_BLOCK, axis=2)
    o_nsa = _nsa_attention(h_nsa, k_cmp, v_cmp_t, vs_t, vw_t, gates_t, slopes, B, S)

    M = mem.shape[1]
    kv = _project(mem.reshape(B * M, D).astype(cdt), w_mem_kv.astype(cdt), ones(2 * bw), cdt, name="proj_mem_kv")
    o_mem = _mem_attention(h_mem, kv, B, S)

    merged = _merge(o_gla, o_nsa, o_mem, w_br_gla.astype(cdt), w_br_nsa.astype(cdt), w_br_mem.astype(cdt), a)
    alpha = (2 * depth) ** 0.25
    return _out_ln(merged, w_out.astype(cdt), x2, ln_g, ln_b, alpha).reshape(B, S, D)


def kernel(x, mem, w_in, b_merge, gla_w_a2, gla_b_a, gla_norm_g, nsa_pe_k, nsa_pe_v, nsa_wk1, nsa_wk2, nsa_wv1, nsa_wv2, w_mem_kv, w_br_gla, w_br_nsa, w_br_mem, w_out, ln_g, ln_b):
    depth = w_in.shape[0]
    for l in range(depth):
        x = _layer(x, mem, w_in[l], b_merge[l], gla_w_a2[l], gla_b_a[l], gla_norm_g[l], nsa_pe_k[l], nsa_pe_v[l],
                   nsa_wk1[l], nsa_wk2[l], nsa_wv1[l], nsa_wv2[l], w_mem_kv[l], w_br_gla[l], w_br_nsa[l],
                   w_br_mem[l], w_out[l], ln_g[l], ln_b[l], depth)
    return x
```

```python
import functools

import jax
import jax.numpy as jnp
from jax import lax
from jax.experimental import pallas as pl
from jax.experimental.pallas import tpu as pltpu

N_BRANCH = 3
GLA_HEADS = 4
GLA_LOWRANK = 16
GLA_TAU = 16.0
GLA_CHUNK = 64
NSA_HEADS = 16
NSA_GROUPS = 4
NSA_REP = NSA_HEADS // NSA_GROUPS
CMP_LEN = 32
CMP_STRIDE = 16
SEL_LEN = 64
SEL_TOPK = 16
WINDOW = 512
Q_BLOCK = 128
FORCE_SCORE = 1e4
MEM_HEADS = 4
LN_EPS = 1e-5
RMS_EPS = 1e-6
NEG_INF = -1e30
LOG2E = 1.4426950408889634

LANES = 128
VMEM_LIMIT_BYTES = 56 * 1024 * 1024
MXU_DTYPE = jnp.bfloat16

SEL_KV_TILE = 512
GLA_STEP_CHUNKS = 4
LN_ROWS = 64
SMALL_W = LANES
AUX_W = LANES
MASK_COL0 = 16
V_AUG = 8
MASK_BIG = -NEG_INF

f32 = jnp.float32


def _dot(a, b):
    return jnp.dot(a, b, preferred_element_type=f32)


def _dot_nt(a, b):
    return lax.dot_general(a, b, (((1,), (1,)), ((), ())), preferred_element_type=f32)


def _dot_tn(a, b):
    return lax.dot_general(a, b, (((0,), (0,)), ((), ())), preferred_element_type=f32)


def _sigmoid(x):
    return 1.0 / (1.0 + jnp.exp(-x))


def _silu(x):
    return x * _sigmoid(x)


def _log_sigmoid(x):
    return -(jnp.maximum(-x, 0.0) + jnp.log1p(jnp.exp(-jnp.abs(x))))


def _split2(x):
    hi = x.astype(MXU_DTYPE)
    lo = (x - hi.astype(f32)).astype(MXU_DTYPE)
    return hi, lo


def _split3(x):
    hi = x.astype(MXU_DTYPE)
    r1 = x - hi.astype(f32)
    mid = r1.astype(MXU_DTYPE)
    lo = (r1 - mid.astype(f32)).astype(MXU_DTYPE)
    return hi, mid, lo


def _params(*sem):
    return pltpu.CompilerParams(dimension_semantics=sem, vmem_limit_bytes=VMEM_LIMIT_BYTES)


def _proj_scale_kernel(x_ref, w_ref, s_ref, o_ref):
    acc = _dot(x_ref[...], w_ref[...])
    o_ref[...] = (acc * s_ref[...]).astype(o_ref.dtype)


def _proj_gate_kernel(x_ref, w_ref, b_ref, o_ref):
    acc = _dot(x_ref[...], w_ref[...])
    o_ref[...] = _sigmoid(acc + b_ref[...]).astype(o_ref.dtype)


def _project(x, w, row, out_dtype, gate=False, name="proj"):
    M, K = x.shape
    N = w.shape[1]
    bm = min(1024, M)
    bn = min(1024, N)
    assert M % bm == 0 and N % bn == 0
    return pl.pallas_call(
        _proj_gate_kernel if gate else _proj_scale_kernel,
        out_shape=jax.ShapeDtypeStruct((M, N), out_dtype),
        grid=(N // bn, M // bm),
        in_specs=[
            pl.BlockSpec((bm, K), lambda j, i: (i, 0)),
            pl.BlockSpec((K, bn), lambda j, i: (0, j)),
            pl.BlockSpec((1, bn), lambda j, i: (0, j)),
        ],
        out_specs=pl.BlockSpec((bm, bn), lambda j, i: (i, j)),
        compiler_params=_params("parallel", "parallel"),
        name=name,
    )(x, w, row.reshape(1, N).astype(f32))


def _gla_kernel(q_ref, k_ref, v_ref, z_ref, ga_ref, wa_ref, ba_ref, ng_ref, o_ref, st_ref, *, dk, dv):
    C = GLA_CHUNK

    @pl.when(pl.program_id(1) == 0)
    def _():
        st_ref[...] = jnp.zeros_like(st_ref)

    row = lax.broadcasted_iota(jnp.int32, (C, C), 0)
    col = lax.broadcasted_iota(jnp.int32, (C, C), 1)
    tril = row >= col
    ltri = jnp.where(tril, 1.0, 0.0).astype(MXU_DTYPE)
    wa_hi, wa_lo = _split2(wa_ref[...])
    for c in range(GLA_STEP_CHUNKS):
        rows = slice(c * C, (c + 1) * C)
        ga_hi, ga_lo = _split2(ga_ref[rows, :])
        zz = _dot(ga_hi, wa_hi) + _dot(ga_lo, wa_hi) + _dot(ga_hi, wa_lo) + ba_ref[...]
        la = _log_sigmoid(zz) * (1.0 / GLA_TAU)
        la_hi, la_mid, la_lo = _split3(la)
        bcum = _dot(ltri, la_hi) + _dot(ltri, la_mid) + _dot(ltri, la_lo)
        for h in range(GLA_HEADS):
            kc = slice(h * dk, (h + 1) * dk)
            vc = slice(h * dv, (h + 1) * dv)
            b = bcum[:, kc]
            bl = b[C - 1:C, :]
            qh = q_ref[rows, kc].astype(f32)
            kh = k_ref[rows, kc].astype(f32)
            vh = v_ref[rows, vc]
            q_d = (qh * jnp.exp(b)).astype(MXU_DTYPE)
            k_d = (kh * jnp.exp(-b)).astype(MXU_DTYPE)
            k_e = (kh * jnp.exp(bl - b)).astype(MXU_DTYPE)
            att = jnp.where(tril, _dot_nt(q_d, k_d), 0.0)
            st = st_ref[h]
            o = _dot(att.astype(MXU_DTYPE), vh) + _dot_nt(q_d, st.astype(MXU_DTYPE))
            st_ref[h] = st * jnp.exp(bl) + _dot_tn(vh, k_e)
            ms = jnp.mean(o * o, axis=-1, keepdims=True)
            on = o * lax.rsqrt(ms + RMS_EPS) * ng_ref[...]
            zg = z_ref[rows, vc].astype(f32)
            o_ref[rows, vc] = (on * _silu(zg)).astype(o_ref.dtype)


def _gla(h_gla, h_small, wa_pad, b_a, norm_g, B, S):
    T = B * S
    kw = h_gla.shape[1] // 6
    dk, dv = kw // GLA_HEADS, 2 * kw // GLA_HEADS
    cs = GLA_STEP_CHUNKS * GLA_CHUNK
    nb = S // cs
    assert S % cs == 0
    rowmap = lambda col: (lambda b, i: (b * nb + i, col))
    return pl.pallas_call(
        functools.partial(_gla_kernel, dk=dk, dv=dv),
        out_shape=jax.ShapeDtypeStruct((T, 2 * kw), MXU_DTYPE),
        grid=(B, nb),
        in_specs=[
            pl.BlockSpec((cs, kw), rowmap(0)),
            pl.BlockSpec((cs, kw), rowmap(1)),
            pl.BlockSpec((cs, 2 * kw), rowmap(1)),
            pl.BlockSpec((cs, 2 * kw), rowmap(2)),
            pl.BlockSpec((cs, SMALL_W), rowmap(0)),
            pl.BlockSpec((SMALL_W, kw), lambda b, i: (0, 0)),
            pl.BlockSpec((1, kw), lambda b, i: (0, 0)),
            pl.BlockSpec((1, dv), lambda b, i: (0, 0)),
        ],
        out_specs=pl.BlockSpec((cs, 2 * kw), rowmap(0)),
        scratch_shapes=[pltpu.VMEM((GLA_HEADS, dv, dk), f32)],
        compiler_params=_params("parallel", "arbitrary"),
        name="gla",
    )(h_gla, h_gla, h_gla, h_gla, h_small, wa_pad, b_a.reshape(1, kw), norm_g.reshape(1, dv))


def _compress_kernel(x_ref, pe_ref, w1_ref, w2_ref, o_ref):
    x = x_ref[0, 0].astype(f32)
    half = x.shape[1]
    xa = (x + pe_ref[0:1, :]).astype(MXU_DTYPE)
    xb = (x + pe_ref[1:2, :]).astype(MXU_DTYPE)
    ya = _dot(xa, w1_ref[0:half, :])
    yb = _dot(xb, w1_ref[half:2 * half, :])
    nc = x.shape[0]
    pre = ya + pltpu.roll(yb, nc - 1, 0)
    o_ref[0, 0] = _dot(_silu(pre).astype(MXU_DTYPE), w2_ref[...]).astype(o_ref.dtype)


def _compress(xblk, pe, w1, w2):
    B, G, NC, W = xblk.shape
    hd = w2.shape[0]
    return pl.pallas_call(
        _compress_kernel,
        out_shape=jax.ShapeDtypeStruct((B, G, NC, hd), MXU_DTYPE),
        grid=(B, G),
        in_specs=[
            pl.BlockSpec((1, 1, NC, W), lambda b, g: (b, g, 0, 0)),
            pl.BlockSpec((2, W), lambda b, g: (0, 0)),
            pl.BlockSpec((2 * W, hd), lambda b, g: (0, 0)),
            pl.BlockSpec((hd, hd), lambda b, g: (0, 0)),
        ],
        out_specs=pl.BlockSpec((1, 1, NC, hd), lambda b, g: (b, g, 0, 0)),
        compiler_params=_params("parallel", "parallel"),
        name="nsa_compress",
    )(xblk, pe.reshape(2, W).astype(f32), w1.astype(MXU_DTYPE), w2.astype(MXU_DTYPE))


def _softmax2_cols(s, ok):
    s = jnp.where(ok, s, NEG_INF)
    m = jnp.max(s, axis=0, keepdims=True)
    e = jnp.where(ok, jnp.exp2(s - m), 0.0)
    den = jnp.sum(e, axis=0, keepdims=True)
    return e * jnp.where(den > 0.0, 1.0 / den, 0.0)


def _mask_heads(ok, s, R, Q):
    return jnp.concatenate([jnp.where(ok, s[:, r * Q:(r + 1) * Q], NEG_INF) for r in range(R)], axis=1)


def _nsa_kernel(q_ref, kc_ref, auxc_ref, vct_ref, ks_ref, aux_ref, vst_ref, kw_ref, vwt_ref, gt_ref, sl_ref,
                z_ref, o_ref, qa_ref, sel_ref, m_ref, acc_ref, flag_ref, *, S):
    Q, R = Q_BLOCK, NSA_REP
    HD = q_ref.shape[1] // R
    RQ = R * Q
    NC = S // CMP_STRIDE
    NS = S // SEL_LEN
    KT = SEL_KV_TILE
    BPT = KT // SEL_LEN
    NT = S // KT
    MR = HD + MASK_COL0
    qb = pl.program_id(2)
    start = qb * Q

    q = q_ref[...]
    qa_ref[0:HD, :] = jnp.concatenate([q[:, r * HD:(r + 1) * HD].T for r in range(R)], axis=1)
    qa_ref[HD:MR, :] = sl_ref[0]
    qa_ref[MR:, :] = jnp.zeros((AUX_W - MASK_COL0, RQ), qa_ref.dtype)
    qa = qa_ref[...]
    lane = lax.broadcasted_iota(jnp.int32, (1, RQ), 1)
    tq = start + jnp.bitwise_and(lane, Q - 1)

    n_idx = lax.broadcasted_iota(jnp.int32, (NC, RQ), 0)
    ok_c = n_idx * CMP_STRIDE + (CMP_LEN - 1) <= tq
    s_c = _dot(jnp.concatenate([kc_ref[0, 0], auxc_ref[...]], axis=1), qa)
    p_c = _softmax2_cols(s_c, ok_c)
    o_cmp = _dot(vct_ref[0, 0], p_c.astype(MXU_DTYPE))

    p_sum = p_c[:, 0:Q]
    for r in range(1, R):
        p_sum = p_sum + p_c[:, r * Q:(r + 1) * Q]
    jj = lax.broadcasted_iota(jnp.int32, (NS, NC), 0) * SEL_LEN
    nn = lax.broadcasted_iota(jnp.int32, (NS, NC), 1) * CMP_STRIDE
    ov = jnp.where((nn < jj + SEL_LEN) & (nn + (CMP_LEN - 1) >= jj), 1.0, 0.0).astype(MXU_DTYPE)
    ps_hi, ps_mid, ps_lo = _split3(p_sum)
    imp = _dot(ov, ps_hi) + _dot(ov, ps_mid) + _dot(ov, ps_lo)

    blk = lax.broadcasted_iota(jnp.int32, (NS, Q), 0)
    tq1 = start + lax.broadcasted_iota(jnp.int32, (NS, Q), 1)
    cur = jnp.right_shift(tq1, SEL_LEN.bit_length() - 1)
    forced = (blk == 0) | (blk == cur) | (blk == cur - 1)
    score = jnp.where(forced, FORCE_SCORE, jnp.where(blk * SEL_LEN <= tq1, imp, -1.0))
    blk_f = blk.astype(f32)
    sel = jnp.zeros((NS, Q), f32)
    for _ in range(min(SEL_TOPK, NS)):
        m = jnp.max(score, axis=0, keepdims=True)
        first = jnp.min(jnp.where(score == m, blk_f, float(NS)), axis=0, keepdims=True)
        hit = blk_f == first
        score = jnp.where(hit, -jnp.inf, score)
        sel = jnp.where(hit, 1.0, sel)
    sel_ref[...] = sel
    for i in range(NT):
        flag_ref[i] = (jnp.max(sel[i * BPT:(i + 1) * BPT, :]) > 0.0).astype(jnp.int32)

    WK = WINDOW + Q
    ws = pl.multiple_of(jnp.maximum(start - WINDOW, 0), Q)
    dist_w = (start - ws) + (lax.broadcasted_iota(jnp.int32, (WK, Q), 1)
                             - lax.broadcasted_iota(jnp.int32, (WK, Q), 0))
    ok_w = (dist_w >= 0) & (dist_w < WINDOW)
    s_w = _dot(jnp.concatenate([kw_ref[pl.ds(ws, WK), :], aux_ref[pl.ds(ws, WK), :]], axis=1), qa)
    s_w = _mask_heads(ok_w, s_w, R, Q)
    e_w = jnp.exp2(s_w - jnp.max(s_w, axis=0, keepdims=True)).astype(MXU_DTYPE)
    acc_w = _dot(vwt_ref[0, 0, :, pl.ds(ws, WK)], e_w)
    o_win = acc_w[0:HD, :] * (1.0 / acc_w[HD:HD + 1, :])

    m_ref[...] = jnp.full(m_ref.shape, NEG_INF, f32)
    acc_ref[...] = jnp.zeros(acc_ref.shape, f32)
    kq = lax.broadcasted_iota(jnp.int32, (KT, Q), 1) - lax.broadcasted_iota(jnp.int32, (KT, Q), 0)

    def sel_tile(i, causal):
        k0 = pl.multiple_of(i * KT, KT)
        b0 = pl.multiple_of(i * BPT, BPT)
        mrow = (sel_ref[pl.ds(b0, BPT), :] - 1.0) * MASK_BIG
        mrow = jnp.concatenate([mrow] * R, axis=1)
        qa_ref[MR:MR + 2 * BPT, :] = jnp.concatenate([mrow, jnp.zeros_like(mrow)], axis=0).astype(qa_ref.dtype)
        s = _dot(jnp.concatenate([ks_ref[pl.ds(k0, KT), :], aux_ref[pl.ds(k0, KT), :]], axis=1), qa_ref[...])
        if causal:
            s = _mask_heads(kq + (start - k0) >= 0, s, R, Q)
        m_old = m_ref[...]
        m_new = jnp.maximum(m_old, jnp.max(s, axis=0, keepdims=True))
        p = jnp.exp2(s - m_new).astype(MXU_DTYPE)
        acc_ref[...] = jnp.exp2(m_old - m_new) * acc_ref[...] + _dot(vst_ref[0, 0, :, pl.ds(k0, KT)], p)
        m_ref[...] = m_new

    i_last = (start + Q - 1) // KT

    def maybe_tile(i, c):
        @pl.when(flag_ref[i] > 0)
        def _():
            sel_tile(i, False)
        return c

    lax.fori_loop(0, i_last, maybe_tile, 0)
    sel_tile(i_last, True)
    acc_s = acc_ref[...]
    o_sel = acc_s[0:HD, :] * (1.0 / acc_s[HD:HD + 1, :])

    gates = _sigmoid(gt_ref[0])
    for r in range(R):
        cs = slice(r * Q, (r + 1) * Q)
        o_r = (gates[3 * r:3 * r + 1, :] * o_cmp[:, cs] + gates[3 * r + 1:3 * r + 2, :] * o_sel[:, cs]
               + gates[3 * r + 2:3 * r + 3, :] * o_win[:, cs])
        hs = slice(r * HD, (r + 1) * HD)
        o_ref[:, hs] = (o_r.T * _silu(z_ref[:, hs].astype(f32))).astype(o_ref.dtype)


def _aux_table(pos, onehot):
    hi = (pos // SEL_LEN) * SEL_LEN
    lo = pos % SEL_LEN
    col = jnp.arange(AUX_W)[None, :]
    t = jnp.where(col < 3, hi[:, None], jnp.where(col < 6, lo[:, None], 0)).astype(f32)
    if onehot:
        blk = (pos // SEL_LEN) % (SEL_KV_TILE // SEL_LEN)
        t = t + jnp.where(col == MASK_COL0 + blk[:, None], 1.0, 0.0)
    return t.astype(MXU_DTYPE)


def _slope_rows(R, G):
    sl = jnp.exp2(-8.0 * (jnp.arange(NSA_HEADS, dtype=f32) + 1.0) / NSA_HEADS) * LOG2E
    parts = _split3(sl)
    rows = jnp.stack(parts + parts, axis=0).astype(f32)
    rows = jnp.repeat(rows.reshape(6, G, R).transpose(1, 0, 2), Q_BLOCK, axis=2)
    pad = jnp.zeros((G, MASK_COL0 - 6, R * Q_BLOCK), f32)
    return jnp.concatenate([rows, pad], axis=1).astype(MXU_DTYPE)


def _with_ones_row(v_t):
    shp = v_t.shape[:-2] + (1, v_t.shape[-1])
    zshp = v_t.shape[:-2] + (V_AUG - 1, v_t.shape[-1])
    return jnp.concatenate([v_t, jnp.ones(shp, v_t.dtype), jnp.zeros(zshp, v_t.dtype)], axis=-2)


def _nsa_attention(h_nsa, k_cmp, v_cmp_t, vs_t, vw_t, gates_t, B, S):
    T = B * S
    G, R = NSA_GROUPS, NSA_REP
    bw = h_nsa.shape[1] * 2 // 7
    HD = bw // NSA_HEADS
    NQ = S // Q_BLOCK
    NC = S // CMP_STRIDE
    NS = S // SEL_LEN
    RQ = R * Q_BLOCK
    kv0 = bw // HD
    assert S % SEL_KV_TILE == 0 and S >= WINDOW + Q_BLOCK
    assert S + CMP_LEN <= SEL_LEN * 256 and 2 * (SEL_KV_TILE // SEL_LEN) <= AUX_W - MASK_COL0
    aux_s = _aux_table(jnp.arange(S), True)
    aux_c = _aux_table(jnp.arange(NC) * CMP_STRIDE + (CMP_LEN - 1), False)
    HA = HD + V_AUG
    return pl.pallas_call(
        functools.partial(_nsa_kernel, S=S),
        out_shape=jax.ShapeDtypeStruct((T, bw), MXU_DTYPE),
        grid=(B, G, NQ),
        in_specs=[
            pl.BlockSpec((Q_BLOCK, R * HD), lambda b, g, i: (b * NQ + i, g)),
            pl.BlockSpec((1, 1, NC, HD), lambda b, g, i: (b, g, 0, 0)),
            pl.BlockSpec((NC, AUX_W), lambda b, g, i: (0, 0)),
            pl.BlockSpec((1, 1, HD, NC), lambda b, g, i: (b, g, 0, 0)),
            pl.BlockSpec((S, HD), lambda b, g, i: (b, kv0 + 2 * G + g)),
            pl.BlockSpec((S, AUX_W), lambda b, g, i: (0, 0)),
            pl.BlockSpec((1, 1, HA, S), lambda b, g, i: (b, g, 0, 0)),
            pl.BlockSpec((S, HD), lambda b, g, i: (b, kv0 + 4 * G + g)),
            pl.BlockSpec((1, 1, HA, S), lambda b, g, i: (b, g, 0, 0)),
            pl.BlockSpec((1, 16, Q_BLOCK), lambda b, g, i: (g, 0, b * NQ + i)),
            pl.BlockSpec((1, MASK_COL0, RQ), lambda b, g, i: (g, 0, 0)),
            pl.BlockSpec((Q_BLOCK, R * HD), lambda b, g, i: (b * NQ + i, (bw + 6 * G * HD) // (R * HD) + g)),
        ],
        out_specs=pl.BlockSpec((Q_BLOCK, R * HD), lambda b, g, i: (b * NQ + i, g)),
        scratch_shapes=[
            pltpu.VMEM((HD + AUX_W, RQ), MXU_DTYPE),
            pltpu.VMEM((NS, Q_BLOCK), f32),
            pltpu.VMEM((1, RQ), f32),
            pltpu.VMEM((HA, RQ), f32),
            pltpu.SMEM((S // SEL_KV_TILE,), jnp.int32),
        ],
        compiler_params=_params("parallel", "parallel", "arbitrary"),
        name="nsa_attention",
    )(h_nsa, k_cmp, aux_c, v_cmp_t, h_nsa, aux_s, _with_ones_row(vs_t), h_nsa, _with_ones_row(vw_t),
      gates_t, _slope_rows(R, G), h_nsa)


def _mem_kernel(q_ref, z_ref, kv_ref, o_ref):
    hw = q_ref.shape[1] // MEM_HEADS
    bw = q_ref.shape[1]
    for h in range(MEM_HEADS):
        cs = slice(h * hw, (h + 1) * hw)
        s = _dot_nt(q_ref[:, cs], kv_ref[:, cs])
        m = jnp.max(s, axis=-1, keepdims=True)
        e = jnp.exp(s - m)
        p = e * (1.0 / jnp.sum(e, axis=-1, keepdims=True))
        o = _dot(p.astype(MXU_DTYPE), kv_ref[:, bw + h * hw:bw + (h + 1) * hw])
        o_ref[:, cs] = (o * _silu(z_ref[:, cs].astype(f32))).astype(o_ref.dtype)


def _mem_attention(h_mem, kv, B, S):
    T = B * S
    bw = h_mem.shape[1] // 2
    M = kv.shape[0] // B
    tq = min(512, S)
    nb = S // tq
    return pl.pallas_call(
        _mem_kernel,
        out_shape=jax.ShapeDtypeStruct((T, bw), MXU_DTYPE),
        grid=(B, nb),
        in_specs=[
            pl.BlockSpec((tq, bw), lambda b, i: (b * nb + i, 0)),
            pl.BlockSpec((tq, bw), lambda b, i: (b * nb + i, 1)),
            pl.BlockSpec((M, 2 * bw), lambda b, i: (b, 0)),
        ],
        out_specs=pl.BlockSpec((tq, bw), lambda b, i: (b * nb + i, 0)),
        compiler_params=_params("parallel", "parallel"),
        name="mem_attention",
    )(h_mem, h_mem, kv)


def _merge_kernel(og_ref, on_ref, om_ref, wg_ref, wn_ref, wm_ref, ag_ref, an_ref, am_ref, o_ref):
    y = ag_ref[...].astype(f32) * _dot(og_ref[...], wg_ref[...])
    y = y + an_ref[...].astype(f32) * _dot(on_ref[...], wn_ref[...])
    y = y + am_ref[...].astype(f32) * _dot(om_ref[...], wm_ref[...])
    o_ref[...] = y.astype(o_ref.dtype)


def _merge(o_gla, o_nsa, o_mem, w_g, w_n, w_m, a):
    T, bw = o_gla.shape
    D = w_g.shape[1]
    tm, tn = min(512, T), min(1024, D)
    nj = D // tn
    osp = pl.BlockSpec((tm, bw), lambda j, i: (i, 0))
    wsp = pl.BlockSpec((bw, tn), lambda j, i: (0, j))
    asp = lambda c: pl.BlockSpec((tm, tn), lambda j, i: (i, c * nj + j))
    return pl.pallas_call(
        _merge_kernel,
        out_shape=jax.ShapeDtypeStruct((T, D), MXU_DTYPE),
        grid=(nj, T // tm),
        in_specs=[osp, osp, osp, wsp, wsp, wsp, asp(0), asp(1), asp(2)],
        out_specs=pl.BlockSpec((tm, tn), lambda j, i: (i, j)),
        compiler_params=_params("parallel", "parallel"),
        name="branch_merge",
    )(o_gla, o_nsa, o_mem, w_g, w_n, w_m, a, a, a)


def _out_ln_kernel(m_ref, w_ref, x_ref, g_ref, b_ref, o_ref, *, alpha, tn):
    j = pl.program_id(1)
    c0 = pl.multiple_of(j * tn, tn)
    o_ref[:, pl.ds(c0, tn)] = alpha * x_ref[...] + _dot(m_ref[...], w_ref[...])

    @pl.when(j == pl.num_programs(1) - 1)
    def _():
        def ln_rows(c, _):
            rows = pl.ds(pl.multiple_of(c * LN_ROWS, LN_ROWS), LN_ROWS)
            z = o_ref[rows, :]
            mu = jnp.mean(z, axis=-1, keepdims=True)
            zc = z - mu
            var = jnp.mean(zc * zc, axis=-1, keepdims=True)
            o_ref[rows, :] = zc * lax.rsqrt(var + LN_EPS) * g_ref[...] + b_ref[...]
            return 0

        lax.fori_loop(0, o_ref.shape[0] // LN_ROWS, ln_rows, 0)


def _out_ln(merged, w_out, x2, ln_g, ln_b, alpha):
    T, D = x2.shape
    tm, tn = min(512, T), min(1024, D)
    return pl.pallas_call(
        functools.partial(_out_ln_kernel, alpha=alpha, tn=tn),
        out_shape=jax.ShapeDtypeStruct((T, D), x2.dtype),
        grid=(T // tm, D // tn),
        in_specs=[
            pl.BlockSpec((tm, D), lambda i, j: (i, 0)),
            pl.BlockSpec((D, tn), lambda i, j: (0, j)),
            pl.BlockSpec((tm, tn), lambda i, j: (i, j)),
            pl.BlockSpec((1, D), lambda i, j: (0, 0)),
            pl.BlockSpec((1, D), lambda i, j: (0, 0)),
        ],
        out_specs=pl.BlockSpec((tm, D), lambda i, j: (i, 0)),
        compiler_params=_params("parallel", "arbitrary"),
        name="out_proj_layernorm",
    )(merged, w_out, x2, ln_g.reshape(1, D), ln_b.reshape(1, D))


def _layer(x, mem, w_in, b_merge, gla_w_a2, gla_b_a, gla_norm_g, nsa_pe_k, nsa_pe_v, nsa_wk1, nsa_wk2,
           nsa_wv1, nsa_wv2, w_mem_kv, w_br_gla, w_br_nsa, w_br_mem, w_out, ln_g, ln_b, depth):
    B, S, D = x.shape
    T = B * S
    bw = D // 2
    gk = bw // 2
    G, R = NSA_GROUPS, NSA_REP
    HD = bw // NSA_HEADS
    kvw = G * HD
    cdt = MXU_DTYPE

    o_ga = 2 * gk + 2 * bw
    o_nq = o_ga + GLA_LOWRANK
    o_nbg = o_nq + bw + 6 * kvw + bw
    o_mq = o_nbg + 3 * NSA_HEADS
    o_mrg = o_mq + 2 * bw
    assert w_in.shape[1] == o_mrg + N_BRANCH * D

    x2 = x.reshape(T, D)
    xb = x2.astype(cdt)
    ones = lambda n: jnp.ones((n,), f32)

    dk = gk // GLA_HEADS
    h_gla = _project(xb, w_in[:, :o_ga].astype(cdt),
                     jnp.concatenate([jnp.full((gk,), dk ** -0.5, f32), ones(o_ga - gk)]), cdt, name="proj_gla")
    npad = SMALL_W - GLA_LOWRANK - 3 * NSA_HEADS
    w_small = jnp.concatenate([w_in[:, o_ga:o_nq], w_in[:, o_nbg:o_mq], jnp.zeros((D, npad), f32)], axis=1)
    h_small = _project(xb, w_small.astype(cdt), ones(SMALL_W), f32, name="proj_small")
    h_nsa = _project(xb, w_in[:, o_nq:o_nbg].astype(cdt),
                     jnp.concatenate([jnp.full((bw,), HD ** -0.5 * LOG2E, f32), ones(6 * kvw + bw)]), cdt,
                     name="proj_nsa")
    mhd = bw // MEM_HEADS
    h_mem = _project(xb, w_in[:, o_mq:o_mrg].astype(cdt),
                     jnp.concatenate([jnp.full((bw,), mhd ** -0.5, f32), ones(bw)]), cdt, name="proj_mem")
    a = _project(xb, w_in[:, o_mrg:].astype(cdt), b_merge, cdt, gate=True, name="proj_merge_gates")

    wa_pad = jnp.concatenate([gla_w_a2, jnp.zeros((SMALL_W - GLA_LOWRANK, gk), f32)], axis=0)
    o_gla = _gla(h_gla, h_small, wa_pad, gla_b_a, gla_norm_g, B, S)

    def grouped(c0):
        return h_nsa[:, c0:c0 + kvw].reshape(B, S, G, HD).transpose(0, 2, 1, 3)

    NC = S // CMP_STRIDE
    c_kv = bw
    k_cmp = _compress(grouped(c_kv).reshape(B, G, NC, CMP_STRIDE * HD), nsa_pe_k, nsa_wk1, nsa_wk2)
    v_cmp = _compress(grouped(c_kv + kvw).reshape(B, G, NC, CMP_STRIDE * HD), nsa_pe_v, nsa_wv1, nsa_wv2)
    v_cmp_t = v_cmp.transpose(0, 1, 3, 2)
    vs_t = grouped(c_kv + 3 * kvw).transpose(0, 1, 3, 2)
    vw_t = grouped(c_kv + 5 * kvw).transpose(0, 1, 3, 2)
    gl = h_small[:, GLA_LOWRANK:GLA_LOWRANK + 3 * NSA_HEADS].reshape(T, G, 3 * R).transpose(1, 2, 0)
    gates_t = jnp.concatenate([gl, jnp.zeros((G, 16 - 3 * R, T), f32)], axis=1)
    o_nsa = _nsa_attention(h_nsa, k_cmp, v_cmp_t, vs_t, vw_t, gates_t, B, S)

    M = mem.shape[1]
    kv = _project(mem.reshape(B * M, D).astype(cdt), w_mem_kv.astype(cdt), ones(2 * bw), cdt, name="proj_mem_kv")
    o_mem = _mem_attention(h_mem, kv, B, S)

    merged = _merge(o_gla, o_nsa, o_mem, w_br_gla.astype(cdt), w_br_nsa.astype(cdt), w_br_mem.astype(cdt), a)
    alpha = (2 * depth) ** 0.25
    return _out_ln(merged, w_out.astype(cdt), x2, ln_g, ln_b, alpha).reshape(B, S, D)


def kernel(x, mem, w_in, b_merge, gla_w_a2, gla_b_a, gla_norm_g, nsa_pe_k, nsa_pe_v, nsa_wk1, nsa_wk2, nsa_wv1, nsa_wv2, w_mem_kv, w_br_gla, w_br_nsa, w_br_mem, w_out, ln_g, ln_b):
    depth = w_in.shape[0]
    for l in range(depth):
        x = _layer(x, mem, w_in[l], b_merge[l], gla_w_a2[l], gla_b_a[l], gla_norm_g[l], nsa_pe_k[l], nsa_pe_v[l],
                   nsa_wk1[l], nsa_wk2[l], nsa_wv1[l], nsa_wv2[l], w_mem_kv[l], w_br_gla[l], w_br_nsa[l],
                   w_br_mem[l], w_out[l], ln_g[l], ln_b[l], depth)
    return x
```

```python
import functools

import jax
import jax.numpy as jnp
from jax import lax
from jax.experimental import pallas as pl
from jax.experimental.pallas import tpu as pltpu

N_BRANCH = 3
GLA_HEADS = 4
GLA_LOWRANK = 16
GLA_TAU = 16.0
GLA_CHUNK = 64
NSA_HEADS = 16
NSA_GROUPS = 4
NSA_REP = NSA_HEADS // NSA_GROUPS
CMP_LEN = 32
CMP_STRIDE = 16
SEL_LEN = 64
SEL_TOPK = 16
WINDOW = 512
Q_BLOCK = 128
FORCE_SCORE = 1e4
MEM_HEADS = 4
LN_EPS = 1e-5
RMS_EPS = 1e-6
NEG_INF = -1e30
LOG2E = 1.4426950408889634

LANES = 128
VMEM_LIMIT_BYTES = 56 * 1024 * 1024
MXU_DTYPE = jnp.bfloat16

SEL_KV_TILE = 512
NSA_Q = 256
GLA_STEP_CHUNKS = 4
LN_ROWS = 64
SMALL_W = LANES
AUX_W = LANES
MASK_COL0 = 16
V_AUG = 8
MASK_BIG = -NEG_INF

f32 = jnp.float32


def _dot(a, b):
    return jnp.dot(a, b, preferred_element_type=f32)


def _dot_nt(a, b):
    return lax.dot_general(a, b, (((1,), (1,)), ((), ())), preferred_element_type=f32)


def _dot_tn(a, b):
    return lax.dot_general(a, b, (((0,), (0,)), ((), ())), preferred_element_type=f32)


def _sigmoid(x):
    return 1.0 / (1.0 + jnp.exp(-x))


def _silu(x):
    return x * _sigmoid(x)


def _log_sigmoid(x):
    return -(jnp.maximum(-x, 0.0) + jnp.log1p(jnp.exp(-jnp.abs(x))))


def _split2(x):
    hi = x.astype(MXU_DTYPE)
    lo = (x - hi.astype(f32)).astype(MXU_DTYPE)
    return hi, lo


def _split3(x):
    hi = x.astype(MXU_DTYPE)
    r1 = x - hi.astype(f32)
    mid = r1.astype(MXU_DTYPE)
    lo = (r1 - mid.astype(f32)).astype(MXU_DTYPE)
    return hi, mid, lo


def _params(*sem):
    return pltpu.CompilerParams(dimension_semantics=sem, vmem_limit_bytes=VMEM_LIMIT_BYTES)


def _proj_scale_kernel(x_ref, w_ref, s_ref, o_ref):
    acc = _dot(x_ref[...], w_ref[...])
    o_ref[...] = (acc * s_ref[...]).astype(o_ref.dtype)


def _proj_gate_kernel(x_ref, w_ref, b_ref, o_ref):
    acc = _dot(x_ref[...], w_ref[...])
    o_ref[...] = _sigmoid(acc + b_ref[...]).astype(o_ref.dtype)


def _project(x, w, row, out_dtype, gate=False, name="proj"):
    M, K = x.shape
    N = w.shape[1]
    bm = min(1024, M)
    bn = min(1024, N)
    assert M % bm == 0 and N % bn == 0
    return pl.pallas_call(
        _proj_gate_kernel if gate else _proj_scale_kernel,
        out_shape=jax.ShapeDtypeStruct((M, N), out_dtype),
        grid=(N // bn, M // bm),
        in_specs=[
            pl.BlockSpec((bm, K), lambda j, i: (i, 0)),
            pl.BlockSpec((K, bn), lambda j, i: (0, j)),
            pl.BlockSpec((1, bn), lambda j, i: (0, j)),
        ],
        out_specs=pl.BlockSpec((bm, bn), lambda j, i: (i, j)),
        compiler_params=_params("parallel", "parallel"),
        name=name,
    )(x, w, row.reshape(1, N).astype(f32))


def _gla_kernel(q_ref, k_ref, v_ref, z_ref, ga_ref, wa_ref, ba_ref, ng_ref, o_ref, st_ref, *, dk, dv):
    C = GLA_CHUNK

    @pl.when(pl.program_id(1) == 0)
    def _():
        st_ref[...] = jnp.zeros_like(st_ref)

    row = lax.broadcasted_iota(jnp.int32, (C, C), 0)
    col = lax.broadcasted_iota(jnp.int32, (C, C), 1)
    tril = row >= col
    ltri = jnp.where(tril, 1.0, 0.0).astype(MXU_DTYPE)
    wa_hi, wa_lo = _split2(wa_ref[...])
    for c in range(GLA_STEP_CHUNKS):
        rows = slice(c * C, (c + 1) * C)
        ga_hi, ga_lo = _split2(ga_ref[rows, :])
        zz = _dot(ga_hi, wa_hi) + _dot(ga_lo, wa_hi) + _dot(ga_hi, wa_lo) + ba_ref[...]
        la = _log_sigmoid(zz) * (1.0 / GLA_TAU)
        la_hi, la_mid, la_lo = _split3(la)
        bcum = _dot(ltri, la_hi) + _dot(ltri, la_mid) + _dot(ltri, la_lo)
        for h in range(GLA_HEADS):
            kc = slice(h * dk, (h + 1) * dk)
            vc = slice(h * dv, (h + 1) * dv)
            b = bcum[:, kc]
            bl = b[C - 1:C, :]
            qh = q_ref[rows, kc].astype(f32)
            kh = k_ref[rows, kc].astype(f32)
            vh = v_ref[rows, vc]
            q_d = (qh * jnp.exp(b)).astype(MXU_DTYPE)
            k_d = (kh * jnp.exp(-b)).astype(MXU_DTYPE)
            k_e = (kh * jnp.exp(bl - b)).astype(MXU_DTYPE)
            att = jnp.where(tril, _dot_nt(q_d, k_d), 0.0)
            st = st_ref[h]
            o = _dot(att.astype(MXU_DTYPE), vh) + _dot_nt(q_d, st.astype(MXU_DTYPE))
            st_ref[h] = st * jnp.exp(bl) + _dot_tn(vh, k_e)
            ms = jnp.mean(o * o, axis=-1, keepdims=True)
            on = o * lax.rsqrt(ms + RMS_EPS) * ng_ref[...]
            zg = z_ref[rows, vc].astype(f32)
            o_ref[rows, vc] = (on * _silu(zg)).astype(o_ref.dtype)


def _gla(h_gla, h_small, wa_pad, b_a, norm_g, B, S):
    T = B * S
    kw = h_gla.shape[1] // 6
    dk, dv = kw // GLA_HEADS, 2 * kw // GLA_HEADS
    cs = GLA_STEP_CHUNKS * GLA_CHUNK
    nb = S // cs
    assert S % cs == 0
    rowmap = lambda col: (lambda b, i: (b * nb + i, col))
    return pl.pallas_call(
        functools.partial(_gla_kernel, dk=dk, dv=dv),
        out_shape=jax.ShapeDtypeStruct((T, 2 * kw), MXU_DTYPE),
        grid=(B, nb),
        in_specs=[
            pl.BlockSpec((cs, kw), rowmap(0)),
            pl.BlockSpec((cs, kw), rowmap(1)),
            pl.BlockSpec((cs, 2 * kw), rowmap(1)),
            pl.BlockSpec((cs, 2 * kw), rowmap(2)),
            pl.BlockSpec((cs, SMALL_W), rowmap(0)),
            pl.BlockSpec((SMALL_W, kw), lambda b, i: (0, 0)),
            pl.BlockSpec((1, kw), lambda b, i: (0, 0)),
            pl.BlockSpec((1, dv), lambda b, i: (0, 0)),
        ],
        out_specs=pl.BlockSpec((cs, 2 * kw), rowmap(0)),
        scratch_shapes=[pltpu.VMEM((GLA_HEADS, dv, dk), f32)],
        compiler_params=_params("parallel", "arbitrary"),
        name="gla",
    )(h_gla, h_gla, h_gla, h_gla, h_small, wa_pad, b_a.reshape(1, kw), norm_g.reshape(1, dv))


def _compress_kernel(x_ref, pe_ref, w1_ref, w2_ref, o_ref):
    x = x_ref[0, 0].astype(f32)
    half = x.shape[1]
    xa = (x + pe_ref[0:1, :]).astype(MXU_DTYPE)
    xb = (x + pe_ref[1:2, :]).astype(MXU_DTYPE)
    ya = _dot(xa, w1_ref[0:half, :])
    yb = _dot(xb, w1_ref[half:2 * half, :])
    nc = x.shape[0]
    pre = ya + pltpu.roll(yb, nc - 1, 0)
    o_ref[0, 0] = _dot(_silu(pre).astype(MXU_DTYPE), w2_ref[...]).astype(o_ref.dtype)


def _compress(xblk, pe, w1, w2):
    B, G, NC, W = xblk.shape
    hd = w2.shape[0]
    return pl.pallas_call(
        _compress_kernel,
        out_shape=jax.ShapeDtypeStruct((B, G, NC, hd), MXU_DTYPE),
        grid=(B, G),
        in_specs=[
            pl.BlockSpec((1, 1, NC, W), lambda b, g: (b, g, 0, 0)),
            pl.BlockSpec((2, W), lambda b, g: (0, 0)),
            pl.BlockSpec((2 * W, hd), lambda b, g: (0, 0)),
            pl.BlockSpec((hd, hd), lambda b, g: (0, 0)),
        ],
        out_specs=pl.BlockSpec((1, 1, NC, hd), lambda b, g: (b, g, 0, 0)),
        compiler_params=_params("parallel", "parallel"),
        name="nsa_compress",
    )(xblk, pe.reshape(2, W).astype(f32), w1.astype(MXU_DTYPE), w2.astype(MXU_DTYPE))


def _mask_heads(ok, s, R, Q, fill=NEG_INF):
    return jnp.concatenate([jnp.where(ok, s[:, r * Q:(r + 1) * Q], fill) for r in range(R)], axis=1)


def _softmax2_cols(s, ok, R, Q):
    s = _mask_heads(ok, s, R, Q)
    m = jnp.max(s, axis=0, keepdims=True)
    e = _mask_heads(ok, jnp.exp2(s - m), R, Q, 0.0)
    den = jnp.sum(e, axis=0, keepdims=True)
    return e * jnp.where(den > 0.0, 1.0 / den, 0.0)


def _nsa_kernel(q_ref, kc_ref, auxc_ref, vct_ref, ov_ref, ks_ref, aux_ref, vst_ref, kw_ref, vwt_ref, gt_ref,
                sl_ref, z_ref, o_ref, qa_ref, sel_ref, m_ref, acc_ref, s_ref, p_ref, al_ref, idx_ref, *, S):
    Q, R = NSA_Q, NSA_REP
    HD = q_ref.shape[1] // R
    RQ = R * Q
    NC = S // CMP_STRIDE
    NS = S // SEL_LEN
    KT = SEL_KV_TILE
    BPT = KT // SEL_LEN
    NT = S // KT
    MR = HD + MASK_COL0
    qb = pl.program_id(2)
    start = qb * Q

    q = q_ref[...]
    qa_ref[0:HD, :] = jnp.concatenate(
        [q[c * LANES:(c + 1) * LANES, r * HD:(r + 1) * HD].T for r in range(R) for c in range(Q // LANES)], axis=1)
    qa_ref[HD:MR, :] = sl_ref[0]
    qa_ref[MR:, :] = jnp.zeros((AUX_W - MASK_COL0, RQ), qa_ref.dtype)
    qa = qa_ref[...]

    ok_c = (lax.broadcasted_iota(jnp.int32, (NC, Q), 0) * CMP_STRIDE + (CMP_LEN - 1)
            <= start + lax.broadcasted_iota(jnp.int32, (NC, Q), 1))
    s_c = _dot(jnp.concatenate([kc_ref[0, 0], auxc_ref[...]], axis=1), qa)
    p_c = _softmax2_cols(s_c, ok_c, R, Q)
    o_cmp = _dot(vct_ref[0, 0], p_c.astype(MXU_DTYPE))

    p_sum = p_c[:, 0:Q]
    for r in range(1, R):
        p_sum = p_sum + p_c[:, r * Q:(r + 1) * Q]
    ov = ov_ref[...]
    ps_hi, ps_mid, ps_lo = _split3(p_sum)
    imp = _dot(ov, ps_hi) + _dot(ov, ps_mid) + _dot(ov, ps_lo)

    blk = lax.broadcasted_iota(jnp.int32, (NS, Q), 0)
    tq1 = start + lax.broadcasted_iota(jnp.int32, (NS, Q), 1)
    cur = jnp.right_shift(tq1, SEL_LEN.bit_length() - 1)
    forced = (blk == 0) | (blk == cur) | (blk == cur - 1)
    score = jnp.where(forced, FORCE_SCORE, jnp.where(blk * SEL_LEN <= tq1, imp, -1.0))
    blk_f = blk.astype(f32)
    sel = jnp.zeros((NS, Q), f32)
    for _ in range(min(SEL_TOPK, NS)):
        m = jnp.max(score, axis=0, keepdims=True)
        first = jnp.min(jnp.where(score == m, blk_f, float(NS)), axis=0, keepdims=True)
        hit = blk_f == first
        score = jnp.where(hit, -jnp.inf, score)
        sel = jnp.where(hit, 1.0, sel)
    sel_past = jnp.where(blk * SEL_LEN < start, sel, 0.0)
    sel_ref[0] = sel
    sel_ref[1] = sel_past
    n_act = jnp.int32(0)
    for i in range(NT):
        idx_ref[n_act] = jnp.int32(i)
        n_act = n_act + (jnp.max(sel_past[i * BPT:(i + 1) * BPT, :]) > 0.0).astype(jnp.int32)

    WK = WINDOW + Q
    ws = pl.multiple_of(jnp.maximum(start - WINDOW, 0), Q)
    dist_w = (start - ws) + (lax.broadcasted_iota(jnp.int32, (WK, Q), 1)
                             - lax.broadcasted_iota(jnp.int32, (WK, Q), 0))
    ok_w = (dist_w >= 0) & (dist_w < WINDOW)
    s_w = _dot(jnp.concatenate([kw_ref[pl.ds(ws, WK), :], aux_ref[pl.ds(ws, WK), :]], axis=1), qa)
    s_w = _mask_heads(ok_w, s_w, R, Q)
    e_w = jnp.exp2(s_w - jnp.max(s_w, axis=0, keepdims=True)).astype(MXU_DTYPE)
    acc_w = _dot(vwt_ref[0, 0, :, pl.ds(ws, WK)], e_w)
    o_win = acc_w[0:HD, :] * (1.0 / acc_w[HD:HD + 1, :])

    def mask_rows(which, b0, valid):
        mrow = jnp.where(valid, (sel_ref[which, pl.ds(b0, BPT), :] - 1.0) * MASK_BIG, -MASK_BIG)
        mrow = jnp.concatenate([mrow] * R, axis=1)
        qa_ref[MR:MR + 2 * BPT, :] = jnp.concatenate([mrow, jnp.zeros_like(mrow)], axis=0).astype(qa_ref.dtype)

    q0 = pl.multiple_of(start, Q)
    mask_rows(0, pl.multiple_of((start // KT) * BPT, BPT), True)
    s_o = _dot(jnp.concatenate([ks_ref[pl.ds(q0, Q), :], aux_ref[pl.ds(q0, Q), :]], axis=1), qa_ref[...])
    ok_o = lax.broadcasted_iota(jnp.int32, (Q, Q), 0) <= lax.broadcasted_iota(jnp.int32, (Q, Q), 1)
    s_o = _mask_heads(ok_o, s_o, R, Q)
    m_o = jnp.max(s_o, axis=0, keepdims=True)
    m_ref[...] = m_o
    acc_ref[...] = _dot(vst_ref[0, 0, :, pl.ds(q0, Q)], jnp.exp2(s_o - m_o).astype(MXU_DTYPE))

    def tile_of(j):
        return idx_ref[jnp.clip(j, 0, jnp.maximum(n_act - 1, 0))]

    def scores(j, slot):
        i = tile_of(j)
        k0 = pl.multiple_of(i * KT, KT)
        mask_rows(1, pl.multiple_of(i * BPT, BPT), j < n_act)
        s_ref[slot] = _dot(jnp.concatenate([ks_ref[pl.ds(k0, KT), :], aux_ref[pl.ds(k0, KT), :]], axis=1),
                           qa_ref[...])

    def softmax(slot):
        s = s_ref[slot]
        m_old = m_ref[...]
        m_new = jnp.maximum(m_old, jnp.max(s, axis=0, keepdims=True))
        p_ref[slot] = jnp.exp2(s - m_new).astype(p_ref.dtype)
        al_ref[slot] = jnp.exp2(m_old - m_new)
        m_ref[...] = m_new

    def accumulate(j, slot):
        k0 = pl.multiple_of(tile_of(j) * KT, KT)
        acc_ref[...] = al_ref[slot] * acc_ref[...] + _dot(vst_ref[0, 0, :, pl.ds(k0, KT)], p_ref[slot])

    scores(0, 0)
    scores(1, 1)
    softmax(0)

    def pipe(k, c):
        j = 2 * k
        scores(j + 2, 0)
        softmax(1)
        accumulate(j, 0)
        scores(j + 3, 1)
        softmax(0)
        accumulate(j + 1, 1)
        return c

    lax.fori_loop(0, (n_act + 1) // 2, pipe, 0)
    acc_s = acc_ref[...]
    o_sel = acc_s[0:HD, :] * (1.0 / acc_s[HD:HD + 1, :])

    gates = _sigmoid(gt_ref[0])
    for r in range(R):
        cs = slice(r * Q, (r + 1) * Q)
        o_r = (gates[3 * r:3 * r + 1, :] * o_cmp[:, cs] + gates[3 * r + 1:3 * r + 2, :] * o_sel[:, cs]
               + gates[3 * r + 2:3 * r + 3, :] * o_win[:, cs])
        hs = slice(r * HD, (r + 1) * HD)
        o_ref[:, hs] = (o_r.T * _silu(z_ref[:, hs].astype(f32))).astype(o_ref.dtype)


def _aux_table(pos, onehot):
    hi = (pos // SEL_LEN) * SEL_LEN
    lo = pos % SEL_LEN
    col = jnp.arange(AUX_W)[None, :]
    t = jnp.where(col < 3, hi[:, None], jnp.where(col < 6, lo[:, None], 0)).astype(f32)
    if onehot:
        blk = (pos // SEL_LEN) % (SEL_KV_TILE // SEL_LEN)
        t = t + jnp.where(col == MASK_COL0 + blk[:, None], 1.0, 0.0)
    return t.astype(MXU_DTYPE)


def _slope_rows(R, G):
    sl = jnp.exp2(-8.0 * (jnp.arange(NSA_HEADS, dtype=f32) + 1.0) / NSA_HEADS) * LOG2E
    parts = _split3(sl)
    rows = jnp.stack(parts + parts, axis=0).astype(f32)
    rows = jnp.repeat(rows.reshape(6, G, R).transpose(1, 0, 2), NSA_Q, axis=2)
    pad = jnp.zeros((G, MASK_COL0 - 6, R * NSA_Q), f32)
    return jnp.concatenate([rows, pad], axis=1).astype(MXU_DTYPE)


def _overlap_table(NS, NC):
    jj = jnp.arange(NS)[:, None] * SEL_LEN
    nn = jnp.arange(NC)[None, :] * CMP_STRIDE
    return jnp.where((nn < jj + SEL_LEN) & (nn + (CMP_LEN - 1) >= jj), 1.0, 0.0).astype(MXU_DTYPE)


def _with_ones_row(v_t):
    shp = v_t.shape[:-2] + (1, v_t.shape[-1])
    zshp = v_t.shape[:-2] + (V_AUG - 1, v_t.shape[-1])
    return jnp.concatenate([v_t, jnp.ones(shp, v_t.dtype), jnp.zeros(zshp, v_t.dtype)], axis=-2)


def _nsa_attention(h_nsa, k_cmp, v_cmp_t, vs_t, vw_t, gates_t, B, S):
    T = B * S
    G, R = NSA_GROUPS, NSA_REP
    bw = h_nsa.shape[1] * 2 // 7
    HD = bw // NSA_HEADS
    Q = NSA_Q
    NQ = S // Q
    NC = S // CMP_STRIDE
    NS = S // SEL_LEN
    RQ = R * Q
    KT = SEL_KV_TILE
    kv0 = bw // HD
    assert S % KT == 0 and S % Q == 0 and KT % Q == 0 and S >= WINDOW + Q
    assert S + CMP_LEN <= SEL_LEN * 256 and 2 * (KT // SEL_LEN) <= AUX_W - MASK_COL0
    aux_s = _aux_table(jnp.arange(S), True)
    aux_c = _aux_table(jnp.arange(NC) * CMP_STRIDE + (CMP_LEN - 1), False)
    HA = HD + V_AUG
    return pl.pallas_call(
        functools.partial(_nsa_kernel, S=S),
        out_shape=jax.ShapeDtypeStruct((T, bw), MXU_DTYPE),
        grid=(B, G, NQ),
        in_specs=[
            pl.BlockSpec((Q, R * HD), lambda b, g, i: (b * NQ + i, g)),
            pl.BlockSpec((1, 1, NC, HD), lambda b, g, i: (b, g, 0, 0)),
            pl.BlockSpec((NC, AUX_W), lambda b, g, i: (0, 0)),
            pl.BlockSpec((1, 1, HD, NC), lambda b, g, i: (b, g, 0, 0)),
            pl.BlockSpec((NS, NC), lambda b, g, i: (0, 0)),
            pl.BlockSpec((S, HD), lambda b, g, i: (b, kv0 + 2 * G + g)),
            pl.BlockSpec((S, AUX_W), lambda b, g, i: (0, 0)),
            pl.BlockSpec((1, 1, HA, S), lambda b, g, i: (b, g, 0, 0)),
            pl.BlockSpec((S, HD), lambda b, g, i: (b, kv0 + 4 * G + g)),
            pl.BlockSpec((1, 1, HA, S), lambda b, g, i: (b, g, 0, 0)),
            pl.BlockSpec((1, 16, Q), lambda b, g, i: (g, 0, b * NQ + i)),
            pl.BlockSpec((1, MASK_COL0, RQ), lambda b, g, i: (g, 0, 0)),
            pl.BlockSpec((Q, R * HD), lambda b, g, i: (b * NQ + i, (bw + 6 * G * HD) // (R * HD) + g)),
        ],
        out_specs=pl.BlockSpec((Q, R * HD), lambda b, g, i: (b * NQ + i, g)),
        scratch_shapes=[
            pltpu.VMEM((HD + AUX_W, RQ), MXU_DTYPE),
            pltpu.VMEM((2, NS, Q), f32),
            pltpu.VMEM((1, RQ), f32),
            pltpu.VMEM((HA, RQ), f32),
            pltpu.VMEM((2, KT, RQ), f32),
            pltpu.VMEM((2, KT, RQ), MXU_DTYPE),
            pltpu.VMEM((2, 1, RQ), f32),
            pltpu.SMEM((S // KT + 1,), jnp.int32),
        ],
        compiler_params=_params("parallel", "parallel", "arbitrary"),
        name="nsa_attention",
    )(h_nsa, k_cmp, aux_c, v_cmp_t, _overlap_table(NS, NC), h_nsa, aux_s, _with_ones_row(vs_t), h_nsa,
      _with_ones_row(vw_t), gates_t, _slope_rows(R, G), h_nsa)


def _mem_kernel(q_ref, z_ref, kv_ref, o_ref):
    hw = q_ref.shape[1] // MEM_HEADS
    bw = q_ref.shape[1]
    for h in range(MEM_HEADS):
        cs = slice(h * hw, (h + 1) * hw)
        s = _dot_nt(q_ref[:, cs], kv_ref[:, cs])
        m = jnp.max(s, axis=-1, keepdims=True)
        e = jnp.exp(s - m)
        p = e * (1.0 / jnp.sum(e, axis=-1, keepdims=True))
        o = _dot(p.astype(MXU_DTYPE), kv_ref[:, bw + h * hw:bw + (h + 1) * hw])
        o_ref[:, cs] = (o * _silu(z_ref[:, cs].astype(f32))).astype(o_ref.dtype)


def _mem_attention(h_mem, kv, B, S):
    T = B * S
    bw = h_mem.shape[1] // 2
    M = kv.shape[0] // B
    tq = min(512, S)
    nb = S // tq
    return pl.pallas_call(
        _mem_kernel,
        out_shape=jax.ShapeDtypeStruct((T, bw), MXU_DTYPE),
        grid=(B, nb),
        in_specs=[
            pl.BlockSpec((tq, bw), lambda b, i: (b * nb + i, 0)),
            pl.BlockSpec((tq, bw), lambda b, i: (b * nb + i, 1)),
            pl.BlockSpec((M, 2 * bw), lambda b, i: (b, 0)),
        ],
        out_specs=pl.BlockSpec((tq, bw), lambda b, i: (b * nb + i, 0)),
        compiler_params=_params("parallel", "parallel"),
        name="mem_attention",
    )(h_mem, h_mem, kv)


def _merge_kernel(og_ref, on_ref, om_ref, wg_ref, wn_ref, wm_ref, ag_ref, an_ref, am_ref, o_ref):
    y = ag_ref[...].astype(f32) * _dot(og_ref[...], wg_ref[...])
    y = y + an_ref[...].astype(f32) * _dot(on_ref[...], wn_ref[...])
    y = y + am_ref[...].astype(f32) * _dot(om_ref[...], wm_ref[...])
    o_ref[...] = y.astype(o_ref.dtype)


def _merge(o_gla, o_nsa, o_mem, w_g, w_n, w_m, a):
    T, bw = o_gla.shape
    D = w_g.shape[1]
    tm, tn = min(512, T), min(1024, D)
    nj = D // tn
    osp = pl.BlockSpec((tm, bw), lambda j, i: (i, 0))
    wsp = pl.BlockSpec((bw, tn), lambda j, i: (0, j))
    asp = lambda c: pl.BlockSpec((tm, tn), lambda j, i: (i, c * nj + j))
    return pl.pallas_call(
        _merge_kernel,
        out_shape=jax.ShapeDtypeStruct((T, D), MXU_DTYPE),
        grid=(nj, T // tm),
        in_specs=[osp, osp, osp, wsp, wsp, wsp, asp(0), asp(1), asp(2)],
        out_specs=pl.BlockSpec((tm, tn), lambda j, i: (i, j)),
        compiler_params=_params("parallel", "parallel"),
        name="branch_merge",
    )(o_gla, o_nsa, o_mem, w_g, w_n, w_m, a, a, a)


def _out_ln_kernel(m_ref, w_ref, x_ref, g_ref, b_ref, o_ref, *, alpha, tn):
    j = pl.program_id(1)
    c0 = pl.multiple_of(j * tn, tn)
    o_ref[:, pl.ds(c0, tn)] = alpha * x_ref[...] + _dot(m_ref[...], w_ref[...])

    @pl.when(j == pl.num_programs(1) - 1)
    def _():
        def ln_rows(c, _):
            rows = pl.ds(pl.multiple_of(c * LN_ROWS, LN_ROWS), LN_ROWS)
            z = o_ref[rows, :]
            mu = jnp.mean(z, axis=-1, keepdims=True)
            zc = z - mu
            var = jnp.mean(zc * zc, axis=-1, keepdims=True)
            o_ref[rows, :] = zc * lax.rsqrt(var + LN_EPS) * g_ref[...] + b_ref[...]
            return 0

        lax.fori_loop(0, o_ref.shape[0] // LN_ROWS, ln_rows, 0)


def _out_ln(merged, w_out, x2, ln_g, ln_b, alpha):
    T, D = x2.shape
    tm, tn = min(512, T), min(1024, D)
    return pl.pallas_call(
        functools.partial(_out_ln_kernel, alpha=alpha, tn=tn),
        out_shape=jax.ShapeDtypeStruct((T, D), x2.dtype),
        grid=(T // tm, D // tn),
        in_specs=[
            pl.BlockSpec((tm, D), lambda i, j: (i, 0)),
            pl.BlockSpec((D, tn), lambda i, j: (0, j)),
            pl.BlockSpec((tm, tn), lambda i, j: (i, j)),
            pl.BlockSpec((1, D), lambda i, j: (0, 0)),
            pl.BlockSpec((1, D), lambda i, j: (0, 0)),
        ],
        out_specs=pl.BlockSpec((tm, D), lambda i, j: (i, 0)),
        compiler_params=_params("parallel", "arbitrary"),
        name="out_proj_layernorm",
    )(merged, w_out, x2, ln_g.reshape(1, D), ln_b.reshape(1, D))


def _layer(x, mem, w_in, b_merge, gla_w_a2, gla_b_a, gla_norm_g, nsa_pe_k, nsa_pe_v, nsa_wk1, nsa_wk2,
           nsa_wv1, nsa_wv2, w_mem_kv, w_br_gla, w_br_nsa, w_br_mem, w_out, ln_g, ln_b, depth):
    B, S, D = x.shape
    T = B * S
    bw = D // 2
    gk = bw // 2
    G, R = NSA_GROUPS, NSA_REP
    HD = bw // NSA_HEADS
    kvw = G * HD
    cdt = MXU_DTYPE

    o_ga = 2 * gk + 2 * bw
    o_nq = o_ga + GLA_LOWRANK
    o_nbg = o_nq + bw + 6 * kvw + bw
    o_mq = o_nbg + 3 * NSA_HEADS
    o_mrg = o_mq + 2 * bw
    assert w_in.shape[1] == o_mrg + N_BRANCH * D

    x2 = x.reshape(T, D)
    xb = x2.astype(cdt)
    ones = lambda n: jnp.ones((n,), f32)

    dk = gk // GLA_HEADS
    h_gla = _project(xb, w_in[:, :o_ga].astype(cdt),
                     jnp.concatenate([jnp.full((gk,), dk ** -0.5, f32), ones(o_ga - gk)]), cdt, name="proj_gla")
    npad = SMALL_W - GLA_LOWRANK - 3 * NSA_HEADS
    w_small = jnp.concatenate([w_in[:, o_ga:o_nq], w_in[:, o_nbg:o_mq], jnp.zeros((D, npad), f32)], axis=1)
    h_small = _project(xb, w_small.astype(cdt), ones(SMALL_W), f32, name="proj_small")
    h_nsa = _project(xb, w_in[:, o_nq:o_nbg].astype(cdt),
                     jnp.concatenate([jnp.full((bw,), HD ** -0.5 * LOG2E, f32), ones(6 * kvw + bw)]), cdt,
                     name="proj_nsa")
    mhd = bw // MEM_HEADS
    h_mem = _project(xb, w_in[:, o_mq:o_mrg].astype(cdt),
                     jnp.concatenate([jnp.full((bw,), mhd ** -0.5, f32), ones(bw)]), cdt, name="proj_mem")
    a = _project(xb, w_in[:, o_mrg:].astype(cdt), b_merge, cdt, gate=True, name="proj_merge_gates")

    wa_pad = jnp.concatenate([gla_w_a2, jnp.zeros((SMALL_W - GLA_LOWRANK, gk), f32)], axis=0)
    o_gla = _gla(h_gla, h_small, wa_pad, gla_b_a, gla_norm_g, B, S)

    def grouped(c0):
        return h_nsa[:, c0:c0 + kvw].reshape(B, S, G, HD).transpose(0, 2, 1, 3)

    NC = S // CMP_STRIDE
    c_kv = bw
    k_cmp = _compress(grouped(c_kv).reshape(B, G, NC, CMP_STRIDE * HD), nsa_pe_k, nsa_wk1, nsa_wk2)
    v_cmp = _compress(grouped(c_kv + kvw).reshape(B, G, NC, CMP_STRIDE * HD), nsa_pe_v, nsa_wv1, nsa_wv2)
    v_cmp_t = v_cmp.transpose(0, 1, 3, 2)
    vs_t = grouped(c_kv + 3 * kvw).transpose(0, 1, 3, 2)
    vw_t = grouped(c_kv + 5 * kvw).transpose(0, 1, 3, 2)
    gl = h_small[:, GLA_LOWRANK:GLA_LOWRANK + 3 * NSA_HEADS].reshape(T, G, 3 * R).transpose(1, 2, 0)
    gates_t = jnp.concatenate([gl, jnp.zeros((G, 16 - 3 * R, T), f32)], axis=1)
    o_nsa = _nsa_attention(h_nsa, k_cmp, v_cmp_t, vs_t, vw_t, gates_t, B, S)

    M = mem.shape[1]
    kv = _project(mem.reshape(B * M, D).astype(cdt), w_mem_kv.astype(cdt), ones(2 * bw), cdt, name="proj_mem_kv")
    o_mem = _mem_attention(h_mem, kv, B, S)

    merged = _merge(o_gla, o_nsa, o_mem, w_br_gla.astype(cdt), w_br_nsa.astype(cdt), w_br_mem.astype(cdt), a)
    alpha = (2 * depth) ** 0.25
    return _out_ln(merged, w_out.astype(cdt), x2, ln_g, ln_b, alpha).reshape(B, S, D)


def kernel(x, mem, w_in, b_merge, gla_w_a2, gla_b_a, gla_norm_g, nsa_pe_k, nsa_pe_v, nsa_wk1, nsa_wk2, nsa_wv1, nsa_wv2, w_mem_kv, w_br_gla, w_br_nsa, w_br_mem, w_out, ln_g, ln_b):
    depth = w_in.shape[0]
    for l in range(depth):
        x = _layer(x, mem, w_in[l], b_merge[l], gla_w_a2[l], gla_b_a[l], gla_norm_g[l], nsa_pe_k[l], nsa_pe_v[l],
                   nsa_wk1[l], nsa_wk2[l], nsa_wv1[l], nsa_wv2[l], w_mem_kv[l], w_br_gla[l], w_br_nsa[l],
                   w_br_mem[l], w_out[l], ln_g[l], ln_b[l], depth)
    return x
```

```python
import functools

import jax
import jax.numpy as jnp
from jax import lax
from jax.experimental import pallas as pl
from jax.experimental.pallas import tpu as pltpu

N_BRANCH = 3
GLA_HEADS = 4
GLA_LOWRANK = 16
GLA_TAU = 16.0
GLA_CHUNK = 64
NSA_HEADS = 16
NSA_GROUPS = 4
NSA_REP = NSA_HEADS // NSA_GROUPS
CMP_LEN = 32
CMP_STRIDE = 16
SEL_LEN = 64
SEL_TOPK = 16
WINDOW = 512
Q_BLOCK = 128
FORCE_SCORE = 1e4
MEM_HEADS = 4
LN_EPS = 1e-5
RMS_EPS = 1e-6
NEG_INF = -1e30
LOG2E = 1.4426950408889634

LANES = 128
VMEM_LIMIT_BYTES = 56 * 1024 * 1024
MXU_DTYPE = jnp.bfloat16

SEL_KV_TILE = 512
NSA_Q = 256
GLA_STEP_CHUNKS = 4
LN_ROWS = 64
SMALL_W = LANES
AUX_W = LANES
MASK_COL0 = 16
V_AUG = 8
MASK_BIG = -NEG_INF

f32 = jnp.float32


def _dot(a, b):
    return jnp.dot(a, b, preferred_element_type=f32)


def _dot_nt(a, b):
    return lax.dot_general(a, b, (((1,), (1,)), ((), ())), preferred_element_type=f32)


def _dot_tn(a, b):
    return lax.dot_general(a, b, (((0,), (0,)), ((), ())), preferred_element_type=f32)


def _sigmoid(x):
    return 1.0 / (1.0 + jnp.exp(-x))


def _silu(x):
    return x * _sigmoid(x)


def _log_sigmoid(x):
    return -(jnp.maximum(-x, 0.0) + jnp.log1p(jnp.exp(-jnp.abs(x))))


def _split2(x):
    hi = x.astype(MXU_DTYPE)
    lo = (x - hi.astype(f32)).astype(MXU_DTYPE)
    return hi, lo


def _split3(x):
    hi = x.astype(MXU_DTYPE)
    r1 = x - hi.astype(f32)
    mid = r1.astype(MXU_DTYPE)
    lo = (r1 - mid.astype(f32)).astype(MXU_DTYPE)
    return hi, mid, lo


def _params(*sem):
    return pltpu.CompilerParams(dimension_semantics=sem, vmem_limit_bytes=VMEM_LIMIT_BYTES)


def _proj_kernel(x_ref, w_ref, r_ref, o_ref, *, gate_from):
    acc = _dot(x_ref[...], w_ref[...])
    j = pl.program_id(0)

    @pl.when(j < gate_from)
    def _():
        o_ref[...] = (acc * r_ref[...]).astype(o_ref.dtype)

    @pl.when(j >= gate_from)
    def _():
        o_ref[...] = _sigmoid(acc + r_ref[...]).astype(o_ref.dtype)


def _project(x, w, row, out_dtype, c0=0, n=None, gate_from=None, name="proj"):
    M, K = x.shape
    n = w.shape[1] - c0 if n is None else n
    bm = min(1024, M)
    bn = min(1024, n)
    gate_from = n if gate_from is None else gate_from
    assert M % bm == 0 and n % bn == 0 and c0 % bn == 0 and gate_from % bn == 0
    return pl.pallas_call(
        functools.partial(_proj_kernel, gate_from=gate_from // bn),
        out_shape=jax.ShapeDtypeStruct((M, n), out_dtype),
        grid=(n // bn, M // bm),
        in_specs=[
            pl.BlockSpec((bm, K), lambda j, i: (i, 0)),
            pl.BlockSpec((K, bn), lambda j, i: (0, c0 // bn + j)),
            pl.BlockSpec((1, bn), lambda j, i: (0, j)),
        ],
        out_specs=pl.BlockSpec((bm, bn), lambda j, i: (i, j)),
        compiler_params=_params("parallel", "parallel"),
        name=name,
    )(x, w, row.reshape(1, n).astype(f32))


def _repack_kernel(*refs, segs, bn, rows):
    *pieces, o_ref = refs
    j = pl.program_id(0)

    def copy(shift):
        def chunk(c, carry):
            r = pl.ds(pl.multiple_of(c * rows, rows), rows)
            cat = jnp.concatenate([p[r, :] for p in pieces], axis=1)
            o_ref[r, :] = cat[:, shift:shift + bn].astype(o_ref.dtype)
            return carry

        lax.fori_loop(0, o_ref.shape[0] // rows, chunk, 0)

    for lo, hi, _, shift in segs:
        pl.when((j >= lo) & (j < hi))(functools.partial(copy, shift))


def _repack(w, segments, bn=1024, rows=256):
    K, N = w.shape
    npieces = bn // LANES + 1
    segs, blk = [], 0
    for src, width in segments:
        assert width % bn == 0 and src + width <= N
        nb = width // bn
        segs.append((blk, blk + nb, src // LANES - blk * (bn // LANES), src % LANES))
        blk += nb

    def base(j):
        b = jnp.int32(0)
        for lo, hi, off, _ in segs:
            b = jnp.where((j >= lo) & (j < hi), j * (bn // LANES) + off, b)
        return b

    return pl.pallas_call(
        functools.partial(_repack_kernel, segs=tuple(segs), bn=bn, rows=rows),
        out_shape=jax.ShapeDtypeStruct((K, blk * bn), MXU_DTYPE),
        grid=(blk,),
        in_specs=[pl.BlockSpec((K, LANES), lambda j, t=t: (0, base(j) + t)) for t in range(npieces)],
        out_specs=pl.BlockSpec((K, bn), lambda j: (0, j)),
        compiler_params=_params("parallel"),
        name="repack_w_in",
    )(*([w] * npieces))


def _gla_kernel(q_ref, k_ref, v_ref, z_ref, ga_ref, wa_ref, ba_ref, ng_ref, o_ref, st_ref, *, dk, dv):
    C = GLA_CHUNK

    @pl.when(pl.program_id(1) == 0)
    def _():
        st_ref[...] = jnp.zeros_like(st_ref)

    row = lax.broadcasted_iota(jnp.int32, (C, C), 0)
    col = lax.broadcasted_iota(jnp.int32, (C, C), 1)
    tril = row >= col
    ltri = jnp.where(tril, 1.0, 0.0).astype(MXU_DTYPE)
    wa_hi, wa_lo = _split2(wa_ref[...])
    for c in range(GLA_STEP_CHUNKS):
        rows = slice(c * C, (c + 1) * C)
        ga_hi, ga_lo = _split2(ga_ref[rows, :])
        zz = _dot(ga_hi, wa_hi) + _dot(ga_lo, wa_hi) + _dot(ga_hi, wa_lo) + ba_ref[...]
        la = _log_sigmoid(zz) * (1.0 / GLA_TAU)
        la_hi, la_mid, la_lo = _split3(la)
        bcum = _dot(ltri, la_hi) + _dot(ltri, la_mid) + _dot(ltri, la_lo)
        for h in range(GLA_HEADS):
            kc = slice(h * dk, (h + 1) * dk)
            vc = slice(h * dv, (h + 1) * dv)
            b = bcum[:, kc]
            bl = b[C - 1:C, :]
            qh = q_ref[rows, kc].astype(f32)
            kh = k_ref[rows, kc].astype(f32)
            vh = v_ref[rows, vc]
            q_d = (qh * jnp.exp(b)).astype(MXU_DTYPE)
            k_d = (kh * jnp.exp(-b)).astype(MXU_DTYPE)
            k_e = (kh * jnp.exp(bl - b)).astype(MXU_DTYPE)
            att = jnp.where(tril, _dot_nt(q_d, k_d), 0.0)
            st = st_ref[h]
            o = _dot(att.astype(MXU_DTYPE), vh) + _dot_nt(q_d, st.astype(MXU_DTYPE))
            st_ref[h] = st * jnp.exp(bl) + _dot_tn(vh, k_e)
            ms = jnp.mean(o * o, axis=-1, keepdims=True)
            on = o * lax.rsqrt(ms + RMS_EPS) * ng_ref[...]
            zg = z_ref[rows, vc].astype(f32)
            o_ref[rows, vc] = (on * _silu(zg)).astype(o_ref.dtype)


def _gla(h, c0, kw, h_small, wa_pad, b_a, norm_g, B, S):
    T = B * S
    dk, dv = kw // GLA_HEADS, 2 * kw // GLA_HEADS
    cs = GLA_STEP_CHUNKS * GLA_CHUNK
    nb = S // cs
    assert S % cs == 0 and c0 % (2 * kw) == 0
    h_gla = h
    rowmap = lambda col: (lambda b, i: (b * nb + i, col))
    qc, vc = c0 // kw, c0 // (2 * kw)
    return pl.pallas_call(
        functools.partial(_gla_kernel, dk=dk, dv=dv),
        out_shape=jax.ShapeDtypeStruct((T, 2 * kw), MXU_DTYPE),
        grid=(B, nb),
        in_specs=[
            pl.BlockSpec((cs, kw), rowmap(qc)),
            pl.BlockSpec((cs, kw), rowmap(qc + 1)),
            pl.BlockSpec((cs, 2 * kw), rowmap(vc + 1)),
            pl.BlockSpec((cs, 2 * kw), rowmap(vc + 2)),
            pl.BlockSpec((cs, SMALL_W), rowmap(0)),
            pl.BlockSpec((SMALL_W, kw), lambda b, i: (0, 0)),
            pl.BlockSpec((1, kw), lambda b, i: (0, 0)),
            pl.BlockSpec((1, dv), lambda b, i: (0, 0)),
        ],
        out_specs=pl.BlockSpec((cs, 2 * kw), rowmap(0)),
        scratch_shapes=[pltpu.VMEM((GLA_HEADS, dv, dk), f32)],
        compiler_params=_params("parallel", "arbitrary"),
        name="gla",
    )(h_gla, h_gla, h_gla, h_gla, h_small, wa_pad, b_a.reshape(1, kw), norm_g.reshape(1, dv))


def _compress_kernel(x_ref, pe_ref, w1_ref, w2_ref, o_ref):
    x = x_ref[0, 0].astype(f32)
    half = x.shape[1]
    xa = (x + pe_ref[0:1, :]).astype(MXU_DTYPE)
    xb = (x + pe_ref[1:2, :]).astype(MXU_DTYPE)
    ya = _dot(xa, w1_ref[0:half, :])
    yb = _dot(xb, w1_ref[half:2 * half, :])
    nc = x.shape[0]
    pre = ya + pltpu.roll(yb, nc - 1, 0)
    o_ref[0, 0] = _dot(_silu(pre).astype(MXU_DTYPE), w2_ref[...]).astype(o_ref.dtype)


def _compress(xblk, pe, w1, w2):
    B, G, NC, W = xblk.shape
    hd = w2.shape[0]
    return pl.pallas_call(
        _compress_kernel,
        out_shape=jax.ShapeDtypeStruct((B, G, NC, hd), MXU_DTYPE),
        grid=(B, G),
        in_specs=[
            pl.BlockSpec((1, 1, NC, W), lambda b, g: (b, g, 0, 0)),
            pl.BlockSpec((2, W), lambda b, g: (0, 0)),
            pl.BlockSpec((2 * W, hd), lambda b, g: (0, 0)),
            pl.BlockSpec((hd, hd), lambda b, g: (0, 0)),
        ],
        out_specs=pl.BlockSpec((1, 1, NC, hd), lambda b, g: (b, g, 0, 0)),
        compiler_params=_params("parallel", "parallel"),
        name="nsa_compress",
    )(xblk, pe.reshape(2, W).astype(f32), w1.astype(MXU_DTYPE), w2.astype(MXU_DTYPE))


def _mask_heads(ok, s, R, Q, fill=NEG_INF):
    return jnp.concatenate([jnp.where(ok, s[:, r * Q:(r + 1) * Q], fill) for r in range(R)], axis=1)


def _softmax2_cols(s, ok, R, Q):
    s = _mask_heads(ok, s, R, Q)
    m = jnp.max(s, axis=0, keepdims=True)
    e = _mask_heads(ok, jnp.exp2(s - m), R, Q, 0.0)
    den = jnp.sum(e, axis=0, keepdims=True)
    return e * jnp.where(den > 0.0, 1.0 / den, 0.0)


def _nsa_kernel(q_ref, kc_ref, auxc_ref, vct_ref, ov_ref, ks_ref, aux_ref, vst_ref, kw_ref, vwt_ref, gt_ref,
                sl_ref, z_ref, o_ref, qa_ref, sel_ref, m_ref, acc_ref, s_ref, p_ref, al_ref, idx_ref, *, S):
    Q, R = NSA_Q, NSA_REP
    HD = q_ref.shape[1] // R
    RQ = R * Q
    NC = S // CMP_STRIDE
    NS = S // SEL_LEN
    KT = SEL_KV_TILE
    BPT = KT // SEL_LEN
    NT = S // KT
    MR = HD + MASK_COL0
    qb = pl.program_id(2)
    start = qb * Q

    q = q_ref[...]
    qa_ref[0:HD, :] = jnp.concatenate(
        [q[c * LANES:(c + 1) * LANES, r * HD:(r + 1) * HD].T for r in range(R) for c in range(Q // LANES)], axis=1)
    qa_ref[HD:MR, :] = sl_ref[0]
    qa_ref[MR:, :] = jnp.zeros((AUX_W - MASK_COL0, RQ), qa_ref.dtype)
    qa = qa_ref[...]

    ok_c = (lax.broadcasted_iota(jnp.int32, (NC, Q), 0) * CMP_STRIDE + (CMP_LEN - 1)
            <= start + lax.broadcasted_iota(jnp.int32, (NC, Q), 1))
    s_c = _dot(jnp.concatenate([kc_ref[0, 0], auxc_ref[...]], axis=1), qa)
    p_c = _softmax2_cols(s_c, ok_c, R, Q)
    o_cmp = _dot(vct_ref[0, 0], p_c.astype(MXU_DTYPE))

    p_sum = p_c[:, 0:Q]
    for r in range(1, R):
        p_sum = p_sum + p_c[:, r * Q:(r + 1) * Q]
    ov = ov_ref[...]
    ps_hi, ps_mid, ps_lo = _split3(p_sum)
    imp = _dot(ov, ps_hi) + _dot(ov, ps_mid) + _dot(ov, ps_lo)

    blk = lax.broadcasted_iota(jnp.int32, (NS, Q), 0)
    tq1 = start + lax.broadcasted_iota(jnp.int32, (NS, Q), 1)
    cur = jnp.right_shift(tq1, SEL_LEN.bit_length() - 1)
    forced = (blk == 0) | (blk == cur) | (blk == cur - 1)
    score = jnp.where(forced, FORCE_SCORE, jnp.where(blk * SEL_LEN <= tq1, imp, -1.0))
    blk_f = blk.astype(f32)
    sel = jnp.zeros((NS, Q), f32)
    for _ in range(min(SEL_TOPK, NS)):
        m = jnp.max(score, axis=0, keepdims=True)
        first = jnp.min(jnp.where(score == m, blk_f, float(NS)), axis=0, keepdims=True)
        hit = blk_f == first
        score = jnp.where(hit, -jnp.inf, score)
        sel = jnp.where(hit, 1.0, sel)
    sel_past = jnp.where(blk * SEL_LEN < start, sel, 0.0)
    sel_ref[0] = sel
    sel_ref[1] = sel_past
    n_act = jnp.int32(0)
    for i in range(NT):
        idx_ref[n_act] = jnp.int32(i)
        n_act = n_act + (jnp.max(sel_past[i * BPT:(i + 1) * BPT, :]) > 0.0).astype(jnp.int32)

    WK = WINDOW + Q
    ws = pl.multiple_of(jnp.maximum(start - WINDOW, 0), Q)
    dist_w = (start - ws) + (lax.broadcasted_iota(jnp.int32, (WK, Q), 1)
                             - lax.broadcasted_iota(jnp.int32, (WK, Q), 0))
    ok_w = (dist_w >= 0) & (dist_w < WINDOW)
    s_w = _dot(jnp.concatenate([kw_ref[pl.ds(ws, WK), :], aux_ref[pl.ds(ws, WK), :]], axis=1), qa)
    s_w = _mask_heads(ok_w, s_w, R, Q)
    e_w = jnp.exp2(s_w - jnp.max(s_w, axis=0, keepdims=True)).astype(MXU_DTYPE)
    acc_w = _dot(vwt_ref[0, 0, :, pl.ds(ws, WK)], e_w)
    o_win = acc_w[0:HD, :] * (1.0 / acc_w[HD:HD + 1, :])

    def mask_rows(which, b0, valid):
        mrow = jnp.where(valid, (sel_ref[which, pl.ds(b0, BPT), :] - 1.0) * MASK_BIG, -MASK_BIG)
        mrow = jnp.concatenate([mrow] * R, axis=1)
        qa_ref[MR:MR + 2 * BPT, :] = jnp.concatenate([mrow, jnp.zeros_like(mrow)], axis=0).astype(qa_ref.dtype)

    q0 = pl.multiple_of(start, Q)
    mask_rows(0, pl.multiple_of((start // KT) * BPT, BPT), True)
    s_o = _dot(jnp.concatenate([ks_ref[pl.ds(q0, Q), :], aux_ref[pl.ds(q0, Q), :]], axis=1), qa_ref[...])
    ok_o = lax.broadcasted_iota(jnp.int32, (Q, Q), 0) <= lax.broadcasted_iota(jnp.int32, (Q, Q), 1)
    s_o = _mask_heads(ok_o, s_o, R, Q)
    m_o = jnp.max(s_o, axis=0, keepdims=True)
    m_ref[...] = m_o
    acc_ref[...] = _dot(vst_ref[0, 0, :, pl.ds(q0, Q)], jnp.exp2(s_o - m_o).astype(MXU_DTYPE))

    def tile_of(j):
        return idx_ref[jnp.clip(j, 0, jnp.maximum(n_act - 1, 0))]

    def scores(j, slot):
        i = tile_of(j)
        k0 = pl.multiple_of(i * KT, KT)
        mask_rows(1, pl.multiple_of(i * BPT, BPT), j < n_act)
        s_ref[slot] = _dot(jnp.concatenate([ks_ref[pl.ds(k0, KT), :], aux_ref[pl.ds(k0, KT), :]], axis=1),
                           qa_ref[...])

    def softmax(slot):
        s = s_ref[slot]
        m_old = m_ref[...]
        m_new = jnp.maximum(m_old, jnp.max(s, axis=0, keepdims=True))
        p_ref[slot] = jnp.exp2(s - m_new).astype(p_ref.dtype)
        al_ref[slot] = jnp.exp2(m_old - m_new)
        m_ref[...] = m_new

    def accumulate(j, slot):
        k0 = pl.multiple_of(tile_of(j) * KT, KT)
        acc_ref[...] = al_ref[slot] * acc_ref[...] + _dot(vst_ref[0, 0, :, pl.ds(k0, KT)], p_ref[slot])

    scores(0, 0)
    scores(1, 1)
    softmax(0)

    def pipe(k, c):
        j = 2 * k
        scores(j + 2, 0)
        softmax(1)
        accumulate(j, 0)
        scores(j + 3, 1)
        softmax(0)
        accumulate(j + 1, 1)
        return c

    lax.fori_loop(0, (n_act + 1) // 2, pipe, 0)
    acc_s = acc_ref[...]
    o_sel = acc_s[0:HD, :] * (1.0 / acc_s[HD:HD + 1, :])

    gates = _sigmoid(gt_ref[0])
    for r in range(R):
        cs = slice(r * Q, (r + 1) * Q)
        o_r = (gates[3 * r:3 * r + 1, :] * o_cmp[:, cs] + gates[3 * r + 1:3 * r + 2, :] * o_sel[:, cs]
               + gates[3 * r + 2:3 * r + 3, :] * o_win[:, cs])
        hs = slice(r * HD, (r + 1) * HD)
        o_ref[:, hs] = (o_r.T * _silu(z_ref[:, hs].astype(f32))).astype(o_ref.dtype)


def _aux_table(pos, onehot):
    hi = (pos // SEL_LEN) * SEL_LEN
    lo = pos % SEL_LEN
    col = jnp.arange(AUX_W)[None, :]
    t = jnp.where(col < 3, hi[:, None], jnp.where(col < 6, lo[:, None], 0)).astype(f32)
    if onehot:
        blk = (pos // SEL_LEN) % (SEL_KV_TILE // SEL_LEN)
        t = t + jnp.where(col == MASK_COL0 + blk[:, None], 1.0, 0.0)
    return t.astype(MXU_DTYPE)


def _slope_rows(R, G):
    sl = jnp.exp2(-8.0 * (jnp.arange(NSA_HEADS, dtype=f32) + 1.0) / NSA_HEADS) * LOG2E
    parts = _split3(sl)
    rows = jnp.stack(parts + parts, axis=0).astype(f32)
    rows = jnp.repeat(rows.reshape(6, G, R).transpose(1, 0, 2), NSA_Q, axis=2)
    pad = jnp.zeros((G, MASK_COL0 - 6, R * NSA_Q), f32)
    return jnp.concatenate([rows, pad], axis=1).astype(MXU_DTYPE)


def _overlap_table(NS, NC):
    jj = jnp.arange(NS)[:, None] * SEL_LEN
    nn = jnp.arange(NC)[None, :] * CMP_STRIDE
    return jnp.where((nn < jj + SEL_LEN) & (nn + (CMP_LEN - 1) >= jj), 1.0, 0.0).astype(MXU_DTYPE)


def _with_ones_row(v_t):
    shp = v_t.shape[:-2] + (1, v_t.shape[-1])
    zshp = v_t.shape[:-2] + (V_AUG - 1, v_t.shape[-1])
    return jnp.concatenate([v_t, jnp.ones(shp, v_t.dtype), jnp.zeros(zshp, v_t.dtype)], axis=-2)


def _nsa_attention(h_nsa, c0, bw, k_cmp, v_cmp_t, vs_t, vw_t, gates_t, B, S):
    T = B * S
    G, R = NSA_GROUPS, NSA_REP
    HD = bw // NSA_HEADS
    assert c0 % (R * HD) == 0
    qc0 = c0 // (R * HD)
    Q = NSA_Q
    NQ = S // Q
    NC = S // CMP_STRIDE
    NS = S // SEL_LEN
    RQ = R * Q
    KT = SEL_KV_TILE
    kv0 = (c0 + bw) // HD
    assert S % KT == 0 and S % Q == 0 and KT % Q == 0 and S >= WINDOW + Q
    assert S + CMP_LEN <= SEL_LEN * 256 and 2 * (KT // SEL_LEN) <= AUX_W - MASK_COL0
    aux_s = _aux_table(jnp.arange(S), True)
    aux_c = _aux_table(jnp.arange(NC) * CMP_STRIDE + (CMP_LEN - 1), False)
    HA = HD + V_AUG
    return pl.pallas_call(
        functools.partial(_nsa_kernel, S=S),
        out_shape=jax.ShapeDtypeStruct((T, bw), MXU_DTYPE),
        grid=(B, G, NQ),
        in_specs=[
            pl.BlockSpec((Q, R * HD), lambda b, g, i: (b * NQ + i, qc0 + g)),
            pl.BlockSpec((1, 1, NC, HD), lambda b, g, i: (b, g, 0, 0)),
            pl.BlockSpec((NC, AUX_W), lambda b, g, i: (0, 0)),
            pl.BlockSpec((1, 1, HD, NC), lambda b, g, i: (b, g, 0, 0)),
            pl.BlockSpec((NS, NC), lambda b, g, i: (0, 0)),
            pl.BlockSpec((S, HD), lambda b, g, i: (b, kv0 + 2 * G + g)),
            pl.BlockSpec((S, AUX_W), lambda b, g, i: (0, 0)),
            pl.BlockSpec((1, 1, HA, S), lambda b, g, i: (b, g, 0, 0)),
            pl.BlockSpec((S, HD), lambda b, g, i: (b, kv0 + 4 * G + g)),
            pl.BlockSpec((1, 1, HA, S), lambda b, g, i: (b, g, 0, 0)),
            pl.BlockSpec((1, 16, Q), lambda b, g, i: (g, 0, b * NQ + i)),
            pl.BlockSpec((1, MASK_COL0, RQ), lambda b, g, i: (g, 0, 0)),
            pl.BlockSpec((Q, R * HD), lambda b, g, i: (b * NQ + i, qc0 + (bw + 6 * G * HD) // (R * HD) + g)),
        ],
        out_specs=pl.BlockSpec((Q, R * HD), lambda b, g, i: (b * NQ + i, g)),
        scratch_shapes=[
            pltpu.VMEM((HD + AUX_W, RQ), MXU_DTYPE),
            pltpu.VMEM((2, NS, Q), f32),
            pltpu.VMEM((1, RQ), f32),
            pltpu.VMEM((HA, RQ), f32),
            pltpu.VMEM((2, KT, RQ), f32),
            pltpu.VMEM((2, KT, RQ), MXU_DTYPE),
            pltpu.VMEM((2, 1, RQ), f32),
            pltpu.SMEM((S // KT + 1,), jnp.int32),
        ],
        compiler_params=_params("parallel", "parallel", "arbitrary"),
        name="nsa_attention",
    )(h_nsa, k_cmp, aux_c, v_cmp_t, _overlap_table(NS, NC), h_nsa, aux_s, _with_ones_row(vs_t), h_nsa,
      _with_ones_row(vw_t), gates_t, _slope_rows(R, G), h_nsa)


def _mem_kernel(q_ref, z_ref, kv_ref, o_ref):
    hw = q_ref.shape[1] // MEM_HEADS
    bw = q_ref.shape[1]
    for h in range(MEM_HEADS):
        cs = slice(h * hw, (h + 1) * hw)
        s = _dot_nt(q_ref[:, cs], kv_ref[:, cs])
        m = jnp.max(s, axis=-1, keepdims=True)
        e = jnp.exp(s - m)
        p = e * (1.0 / jnp.sum(e, axis=-1, keepdims=True))
        o = _dot(p.astype(MXU_DTYPE), kv_ref[:, bw + h * hw:bw + (h + 1) * hw])
        o_ref[:, cs] = (o * _silu(z_ref[:, cs].astype(f32))).astype(o_ref.dtype)


def _mem_attention(h_mem, c0, bw, kv, B, S):
    T = B * S
    M = kv.shape[0] // B
    tq = min(512, S)
    nb = S // tq
    assert c0 % bw == 0
    qc = c0 // bw
    return pl.pallas_call(
        _mem_kernel,
        out_shape=jax.ShapeDtypeStruct((T, bw), MXU_DTYPE),
        grid=(B, nb),
        in_specs=[
            pl.BlockSpec((tq, bw), lambda b, i: (b * nb + i, qc)),
            pl.BlockSpec((tq, bw), lambda b, i: (b * nb + i, qc + 1)),
            pl.BlockSpec((M, 2 * bw), lambda b, i: (b, 0)),
        ],
        out_specs=pl.BlockSpec((tq, bw), lambda b, i: (b * nb + i, 0)),
        compiler_params=_params("parallel", "parallel"),
        name="mem_attention",
    )(h_mem, h_mem, kv)


def _merge_kernel(og_ref, on_ref, om_ref, wg_ref, wn_ref, wm_ref, ag_ref, an_ref, am_ref, o_ref):
    y = ag_ref[...].astype(f32) * _dot(og_ref[...], wg_ref[...])
    y = y + an_ref[...].astype(f32) * _dot(on_ref[...], wn_ref[...])
    y = y + am_ref[...].astype(f32) * _dot(om_ref[...], wm_ref[...])
    o_ref[...] = y.astype(o_ref.dtype)


def _merge(o_gla, o_nsa, o_mem, w_g, w_n, w_m, a, c0):
    T, bw = o_gla.shape
    D = w_g.shape[1]
    tm, tn = min(512, T), min(1024, D)
    nj = D // tn
    assert c0 % tn == 0
    osp = pl.BlockSpec((tm, bw), lambda j, i: (i, 0))
    wsp = pl.BlockSpec((bw, tn), lambda j, i: (0, j))
    asp = lambda c: pl.BlockSpec((tm, tn), lambda j, i: (i, c0 // tn + c * nj + j))
    return pl.pallas_call(
        _merge_kernel,
        out_shape=jax.ShapeDtypeStruct((T, D), MXU_DTYPE),
        grid=(nj, T // tm),
        in_specs=[osp, osp, osp, wsp, wsp, wsp, asp(0), asp(1), asp(2)],
        out_specs=pl.BlockSpec((tm, tn), lambda j, i: (i, j)),
        compiler_params=_params("parallel", "parallel"),
        name="branch_merge",
    )(o_gla, o_nsa, o_mem, w_g, w_n, w_m, a, a, a)


def _out_ln_kernel(m_ref, w_ref, x_ref, g_ref, b_ref, o_ref, *, alpha, tn):
    j = pl.program_id(1)
    c0 = pl.multiple_of(j * tn, tn)
    o_ref[:, pl.ds(c0, tn)] = alpha * x_ref[...] + _dot(m_ref[...], w_ref[...])

    @pl.when(j == pl.num_programs(1) - 1)
    def _():
        def ln_rows(c, _):
            rows = pl.ds(pl.multiple_of(c * LN_ROWS, LN_ROWS), LN_ROWS)
            z = o_ref[rows, :]
            mu = jnp.mean(z, axis=-1, keepdims=True)
            zc = z - mu
            var = jnp.mean(zc * zc, axis=-1, keepdims=True)
            o_ref[rows, :] = zc * lax.rsqrt(var + LN_EPS) * g_ref[...] + b_ref[...]
            return 0

        lax.fori_loop(0, o_ref.shape[0] // LN_ROWS, ln_rows, 0)


def _out_ln(merged, w_out, x2, ln_g, ln_b, alpha):
    T, D = x2.shape
    tm, tn = min(512, T), min(1024, D)
    return pl.pallas_call(
        functools.partial(_out_ln_kernel, alpha=alpha, tn=tn),
        out_shape=jax.ShapeDtypeStruct((T, D), x2.dtype),
        grid=(T // tm, D // tn),
        in_specs=[
            pl.BlockSpec((tm, D), lambda i, j: (i, 0)),
            pl.BlockSpec((D, tn), lambda i, j: (0, j)),
            pl.BlockSpec((tm, tn), lambda i, j: (i, j)),
            pl.BlockSpec((1, D), lambda i, j: (0, 0)),
            pl.BlockSpec((1, D), lambda i, j: (0, 0)),
        ],
        out_specs=pl.BlockSpec((tm, D), lambda i, j: (i, 0)),
        compiler_params=_params("parallel", "arbitrary"),
        name="out_proj_layernorm",
    )(merged, w_out, x2, ln_g.reshape(1, D), ln_b.reshape(1, D))


def _layer(x, mem, w_in, b_merge, gla_w_a2, gla_b_a, gla_norm_g, nsa_pe_k, nsa_pe_v, nsa_wk1, nsa_wk2,
           nsa_wv1, nsa_wv2, w_mem_kv, w_br_gla, w_br_nsa, w_br_mem, w_out, ln_g, ln_b, depth):
    B, S, D = x.shape
    T = B * S
    bw = D // 2
    gk = bw // 2
    G, R = NSA_GROUPS, NSA_REP
    HD = bw // NSA_HEADS
    kvw = G * HD
    cdt = MXU_DTYPE

    o_ga = 2 * gk + 2 * bw
    o_nq = o_ga + GLA_LOWRANK
    o_nbg = o_nq + bw + 6 * kvw + bw
    o_mq = o_nbg + 3 * NSA_HEADS
    o_mrg = o_mq + 2 * bw
    assert w_in.shape[1] == o_mrg + N_BRANCH * D

    x2 = x.reshape(T, D)
    xb = x2.astype(cdt)
    ones = lambda n: jnp.ones((n,), f32)

    w_all = _repack(w_in, [(0, o_ga), (o_mq, 2 * bw), (o_nq, 2 * bw + 6 * kvw), (o_mrg, N_BRANCH * D)])
    npad = SMALL_W - GLA_LOWRANK - 3 * NSA_HEADS
    w_small = jnp.concatenate([w_in[:, o_ga:o_nq], w_in[:, o_nbg:o_mq], jnp.zeros((D, npad), f32)], axis=1)
    c_gla, c_mem = 0, o_ga
    c_nsa = c_mem + 2 * bw
    c_mrg = c_nsa + 2 * bw + 6 * kvw
    dk, mhd = gk // GLA_HEADS, bw // MEM_HEADS
    row = jnp.concatenate([jnp.full((gk,), dk ** -0.5, f32), ones(o_ga - gk),
                           jnp.full((bw,), mhd ** -0.5, f32), ones(bw),
                           jnp.full((bw,), HD ** -0.5 * LOG2E, f32), ones(6 * kvw + bw), b_merge])
    h = _project(xb, w_all, row, cdt, gate_from=c_mrg, name="proj_in")
    h_small = _project(xb, w_small.astype(cdt), ones(SMALL_W), f32, name="proj_small")

    wa_pad = jnp.concatenate([gla_w_a2, jnp.zeros((SMALL_W - GLA_LOWRANK, gk), f32)], axis=0)
    o_gla = _gla(h, c_gla, gk, h_small, wa_pad, gla_b_a, gla_norm_g, B, S)

    def grouped(c0):
        return h[:, c0:c0 + kvw].reshape(B, S, G, HD).transpose(0, 2, 1, 3)

    NC = S // CMP_STRIDE
    c_kv = c_nsa + bw
    k_cmp = _compress(grouped(c_kv).reshape(B, G, NC, CMP_STRIDE * HD), nsa_pe_k, nsa_wk1, nsa_wk2)
    v_cmp = _compress(grouped(c_kv + kvw).reshape(B, G, NC, CMP_STRIDE * HD), nsa_pe_v, nsa_wv1, nsa_wv2)
    v_cmp_t = v_cmp.transpose(0, 1, 3, 2)
    vs_t = grouped(c_kv + 3 * kvw).transpose(0, 1, 3, 2)
    vw_t = grouped(c_kv + 5 * kvw).transpose(0, 1, 3, 2)
    gl = h_small[:, GLA_LOWRANK:GLA_LOWRANK + 3 * NSA_HEADS].reshape(T, G, 3 * R).transpose(1, 2, 0)
    gates_t = jnp.concatenate([gl, jnp.zeros((G, 16 - 3 * R, T), f32)], axis=1)
    o_nsa = _nsa_attention(h, c_nsa, bw, k_cmp, v_cmp_t, vs_t, vw_t, gates_t, B, S)

    M = mem.shape[1]
    kv = _project(mem.reshape(B * M, D).astype(cdt), w_mem_kv.astype(cdt), ones(2 * bw), cdt, name="proj_mem_kv")
    o_mem = _mem_attention(h, c_mem, bw, kv, B, S)

    merged = _merge(o_gla, o_nsa, o_mem, w_br_gla.astype(cdt), w_br_nsa.astype(cdt), w_br_mem.astype(cdt), h, c_mrg)
    alpha = (2 * depth) ** 0.25
    return _out_ln(merged, w_out.astype(cdt), x2, ln_g, ln_b, alpha).reshape(B, S, D)


def kernel(x, mem, w_in, b_merge, gla_w_a2, gla_b_a, gla_norm_g, nsa_pe_k, nsa_pe_v, nsa_wk1, nsa_wk2, nsa_wv1, nsa_wv2, w_mem_kv, w_br_gla, w_br_nsa, w_br_mem, w_out, ln_g, ln_b):
    depth = w_in.shape[0]
    for l in range(depth):
        x = _layer(x, mem, w_in[l], b_merge[l], gla_w_a2[l], gla_b_a[l], gla_norm_g[l], nsa_pe_k[l], nsa_pe_v[l],
                   nsa_wk1[l], nsa_wk2[l], nsa_wv1[l], nsa_wv2[l], w_mem_kv[l], w_br_gla[l], w_br_nsa[l],
                   w_br_mem[l], w_out[l], ln_g[l], ln_b[l], depth)
    return x
```

```python
import functools

import jax
import jax.numpy as jnp
from jax import lax
from jax.experimental import pallas as pl
from jax.experimental.pallas import tpu as pltpu

N_BRANCH = 3
GLA_HEADS = 4
GLA_LOWRANK = 16
GLA_TAU = 16.0
GLA_CHUNK = 64
NSA_HEADS = 16
NSA_GROUPS = 4
NSA_REP = NSA_HEADS // NSA_GROUPS
CMP_LEN = 32
CMP_STRIDE = 16
SEL_LEN = 64
SEL_TOPK = 16
WINDOW = 512
Q_BLOCK = 128
FORCE_SCORE = 1e4
MEM_HEADS = 4
LN_EPS = 1e-5
RMS_EPS = 1e-6
NEG_INF = -1e30
LOG2E = 1.4426950408889634

LANES = 128
VMEM_LIMIT_BYTES = 56 * 1024 * 1024
MXU_DTYPE = jnp.bfloat16

SEL_KV_TILE = 512
NSA_Q = 256
GLA_STEP_CHUNKS = 4
LN_ROWS = 64
SMALL_W = LANES
AUX_W = LANES
MASK_COL0 = 16
V_AUG = 8
MASK_BIG = -NEG_INF

f32 = jnp.float32


def _dot(a, b):
    return jnp.dot(a, b, preferred_element_type=f32)


def _dot_nt(a, b):
    return lax.dot_general(a, b, (((1,), (1,)), ((), ())), preferred_element_type=f32)


def _dot_tn(a, b):
    return lax.dot_general(a, b, (((0,), (0,)), ((), ())), preferred_element_type=f32)


def _sigmoid(x):
    return 1.0 / (1.0 + jnp.exp(-x))


def _silu(x):
    return x * _sigmoid(x)


def _log_sigmoid(x):
    return -(jnp.maximum(-x, 0.0) + jnp.log1p(jnp.exp(-jnp.abs(x))))


def _split2(x):
    hi = x.astype(MXU_DTYPE)
    lo = (x - hi.astype(f32)).astype(MXU_DTYPE)
    return hi, lo


def _split3(x):
    hi = x.astype(MXU_DTYPE)
    r1 = x - hi.astype(f32)
    mid = r1.astype(MXU_DTYPE)
    lo = (r1 - mid.astype(f32)).astype(MXU_DTYPE)
    return hi, mid, lo


def _params(*sem):
    return pltpu.CompilerParams(dimension_semantics=sem, vmem_limit_bytes=VMEM_LIMIT_BYTES)


def _proj_kernel(x_ref, w_ref, r_ref, o_ref, *, gate):
    acc = _dot(x_ref[...], w_ref[...])
    if gate:
        o_ref[...] = _sigmoid(acc + r_ref[...]).astype(o_ref.dtype)
    else:
        o_ref[...] = (acc * r_ref[...]).astype(o_ref.dtype)


def _project(x, w, row, out_dtype, c0=0, n=None, gate=False, name="proj"):
    M, K = x.shape
    n = w.shape[1] - c0 if n is None else n
    bm = min(1024, M)
    bn = min(1024, n)
    assert M % bm == 0 and n % bn == 0 and c0 % bn == 0
    return pl.pallas_call(
        functools.partial(_proj_kernel, gate=gate),
        out_shape=jax.ShapeDtypeStruct((M, n), out_dtype),
        grid=(n // bn, M // bm),
        in_specs=[
            pl.BlockSpec((bm, K), lambda j, i: (i, 0)),
            pl.BlockSpec((K, bn), lambda j, i: (0, c0 // bn + j)),
            pl.BlockSpec((1, bn), lambda j, i: (0, j)),
        ],
        out_specs=pl.BlockSpec((bm, bn), lambda j, i: (i, j)),
        compiler_params=_params("parallel", "parallel"),
        name=name,
    )(x, w, row.reshape(1, n).astype(f32))


def _proj_small_kernel(x_ref, wa_ref, wb_ref, o_ref, *, na, nb):
    lane = lax.broadcasted_iota(jnp.int32, (1, LANES), 1)
    w = jnp.where(lane < na, wa_ref[...], jnp.where(lane < na + nb, wb_ref[...], 0.0))
    o_ref[...] = _dot(x_ref[...], w.astype(x_ref.dtype))


def _project_small(x, w, ca, na, cb, nb):
    M, K = x.shape
    bm = min(1024, M)
    assert ca % LANES == 0 and cb % LANES == na and na + nb <= LANES
    return pl.pallas_call(
        functools.partial(_proj_small_kernel, na=na, nb=nb),
        out_shape=jax.ShapeDtypeStruct((M, LANES), f32),
        grid=(M // bm,),
        in_specs=[
            pl.BlockSpec((bm, K), lambda i: (i, 0)),
            pl.BlockSpec((K, LANES), lambda i: (0, ca // LANES)),
            pl.BlockSpec((K, LANES), lambda i: (0, cb // LANES)),
        ],
        out_specs=pl.BlockSpec((bm, LANES), lambda i: (i, 0)),
        compiler_params=_params("parallel"),
        name="proj_small",
    )(x, w, w)


def _repack_kernel(*refs, segs, bn, rows):
    *pieces, o_ref = refs
    j = pl.program_id(0)

    def copy(shift):
        def chunk(c, carry):
            r = pl.ds(pl.multiple_of(c * rows, rows), rows)
            cat = jnp.concatenate([p[r, :] for p in pieces], axis=1)
            o_ref[r, :] = cat[:, shift:shift + bn].astype(o_ref.dtype)
            return carry

        lax.fori_loop(0, o_ref.shape[0] // rows, chunk, 0)

    for lo, hi, _, shift in segs:
        pl.when((j >= lo) & (j < hi))(functools.partial(copy, shift))


def _repack(w, segments, bn=1024, rows=256):
    K, N = w.shape
    npieces = bn // LANES + 1
    segs, blk = [], 0
    for src, width in segments:
        assert width % bn == 0 and src + width <= N
        nb = width // bn
        segs.append((blk, blk + nb, src // LANES - blk * (bn // LANES), src % LANES))
        blk += nb

    def base(j):
        b = jnp.int32(0)
        for lo, hi, off, _ in segs:
            b = jnp.where((j >= lo) & (j < hi), j * (bn // LANES) + off, b)
        return b

    return pl.pallas_call(
        functools.partial(_repack_kernel, segs=tuple(segs), bn=bn, rows=rows),
        out_shape=jax.ShapeDtypeStruct((K, blk * bn), MXU_DTYPE),
        grid=(blk,),
        in_specs=[pl.BlockSpec((K, LANES), lambda j, t=t: (0, base(j) + t)) for t in range(npieces)],
        out_specs=pl.BlockSpec((K, bn), lambda j: (0, j)),
        compiler_params=_params("parallel"),
        name="repack_w_in",
    )(*([w] * npieces))


def _gla_kernel(q_ref, k_ref, v_ref, z_ref, ga_ref, wa_ref, ba_ref, ng_ref, o_ref, st_ref, *, dk, dv):
    C = GLA_CHUNK

    @pl.when(pl.program_id(1) == 0)
    def _():
        st_ref[...] = jnp.zeros_like(st_ref)

    row = lax.broadcasted_iota(jnp.int32, (C, C), 0)
    col = lax.broadcasted_iota(jnp.int32, (C, C), 1)
    tril = row >= col
    ltri = jnp.where(tril, 1.0, 0.0).astype(MXU_DTYPE)
    wa_hi, wa_lo = _split2(wa_ref[...])
    for c in range(GLA_STEP_CHUNKS):
        rows = slice(c * C, (c + 1) * C)
        ga_hi, ga_lo = _split2(ga_ref[rows, :])
        zz = _dot(ga_hi, wa_hi) + _dot(ga_lo, wa_hi) + _dot(ga_hi, wa_lo) + ba_ref[...]
        la = _log_sigmoid(zz) * (1.0 / GLA_TAU)
        la_hi, la_mid, la_lo = _split3(la)
        bcum = _dot(ltri, la_hi) + _dot(ltri, la_mid) + _dot(ltri, la_lo)
        for h in range(GLA_HEADS):
            kc = slice(h * dk, (h + 1) * dk)
            vc = slice(h * dv, (h + 1) * dv)
            b = bcum[:, kc]
            bl = b[C - 1:C, :]
            qh = q_ref[rows, kc].astype(f32)
            kh = k_ref[rows, kc].astype(f32)
            vh = v_ref[rows, vc]
            q_d = (qh * jnp.exp(b)).astype(MXU_DTYPE)
            k_d = (kh * jnp.exp(-b)).astype(MXU_DTYPE)
            k_e = (kh * jnp.exp(bl - b)).astype(MXU_DTYPE)
            att = jnp.where(tril, _dot_nt(q_d, k_d), 0.0)
            st = st_ref[h]
            o = _dot(att.astype(MXU_DTYPE), vh) + _dot_nt(q_d, st.astype(MXU_DTYPE))
            st_ref[h] = st * jnp.exp(bl) + _dot_tn(vh, k_e)
            ms = jnp.mean(o * o, axis=-1, keepdims=True)
            on = o * lax.rsqrt(ms + RMS_EPS) * ng_ref[...]
            zg = z_ref[rows, vc].astype(f32)
            o_ref[rows, vc] = (on * _silu(zg)).astype(o_ref.dtype)


def _gla(h, c0, kw, h_small, wa_pad, b_a, norm_g, B, S):
    T = B * S
    dk, dv = kw // GLA_HEADS, 2 * kw // GLA_HEADS
    cs = GLA_STEP_CHUNKS * GLA_CHUNK
    nb = S // cs
    assert S % cs == 0 and c0 % (2 * kw) == 0
    h_gla = h
    rowmap = lambda col: (lambda b, i: (b * nb + i, col))
    qc, vc = c0 // kw, c0 // (2 * kw)
    return pl.pallas_call(
        functools.partial(_gla_kernel, dk=dk, dv=dv),
        out_shape=jax.ShapeDtypeStruct((T, 2 * kw), MXU_DTYPE),
        grid=(B, nb),
        in_specs=[
            pl.BlockSpec((cs, kw), rowmap(qc)),
            pl.BlockSpec((cs, kw), rowmap(qc + 1)),
            pl.BlockSpec((cs, 2 * kw), rowmap(vc + 1)),
            pl.BlockSpec((cs, 2 * kw), rowmap(vc + 2)),
            pl.BlockSpec((cs, SMALL_W), rowmap(0)),
            pl.BlockSpec((SMALL_W, kw), lambda b, i: (0, 0)),
            pl.BlockSpec((1, kw), lambda b, i: (0, 0)),
            pl.BlockSpec((1, dv), lambda b, i: (0, 0)),
        ],
        out_specs=pl.BlockSpec((cs, 2 * kw), rowmap(0)),
        scratch_shapes=[pltpu.VMEM((GLA_HEADS, dv, dk), f32)],
        compiler_params=_params("parallel", "arbitrary"),
        name="gla",
    )(h_gla, h_gla, h_gla, h_gla, h_small, wa_pad, b_a.reshape(1, kw), norm_g.reshape(1, dv))


def _compress_kernel(x_ref, pe_ref, w1_ref, w2_ref, o_ref):
    x = x_ref[0, 0].astype(f32)
    half = x.shape[1]
    xa = (x + pe_ref[0:1, :]).astype(MXU_DTYPE)
    xb = (x + pe_ref[1:2, :]).astype(MXU_DTYPE)
    ya = _dot(xa, w1_ref[0:half, :])
    yb = _dot(xb, w1_ref[half:2 * half, :])
    nc = x.shape[0]
    pre = ya + pltpu.roll(yb, nc - 1, 0)
    o_ref[0, 0] = _dot(_silu(pre).astype(MXU_DTYPE), w2_ref[...]).astype(o_ref.dtype)


def _compress(xblk, pe, w1, w2):
    B, G, NC, W = xblk.shape
    hd = w2.shape[0]
    return pl.pallas_call(
        _compress_kernel,
        out_shape=jax.ShapeDtypeStruct((B, G, NC, hd), MXU_DTYPE),
        grid=(B, G),
        in_specs=[
            pl.BlockSpec((1, 1, NC, W), lambda b, g: (b, g, 0, 0)),
            pl.BlockSpec((2, W), lambda b, g: (0, 0)),
            pl.BlockSpec((2 * W, hd), lambda b, g: (0, 0)),
            pl.BlockSpec((hd, hd), lambda b, g: (0, 0)),
        ],
        out_specs=pl.BlockSpec((1, 1, NC, hd), lambda b, g: (b, g, 0, 0)),
        compiler_params=_params("parallel", "parallel"),
        name="nsa_compress",
    )(xblk, pe.reshape(2, W).astype(f32), w1.astype(MXU_DTYPE), w2.astype(MXU_DTYPE))


def _mask_heads(ok, s, R, Q, fill=NEG_INF):
    return jnp.concatenate([jnp.where(ok, s[:, r * Q:(r + 1) * Q], fill) for r in range(R)], axis=1)


def _softmax2_cols(s, ok, R, Q):
    s = _mask_heads(ok, s, R, Q)
    m = jnp.max(s, axis=0, keepdims=True)
    e = _mask_heads(ok, jnp.exp2(s - m), R, Q, 0.0)
    den = jnp.sum(e, axis=0, keepdims=True)
    return e * jnp.where(den > 0.0, 1.0 / den, 0.0)


def _nsa_kernel(q_ref, kc_ref, auxc_ref, vct_ref, ov_ref, ks_ref, aux_ref, vst_ref, kw_ref, vwt_ref, gt_ref,
                sl_ref, z_ref, o_ref, qa_ref, sel_ref, m_ref, acc_ref, s_ref, p_ref, al_ref, idx_ref, *, S):
    Q, R = NSA_Q, NSA_REP
    HD = q_ref.shape[1] // R
    RQ = R * Q
    NC = S // CMP_STRIDE
    NS = S // SEL_LEN
    KT = SEL_KV_TILE
    BPT = KT // SEL_LEN
    NT = S // KT
    MR = HD + MASK_COL0
    qb = pl.program_id(2)
    start = qb * Q

    q = q_ref[...]
    qa_ref[0:HD, :] = jnp.concatenate(
        [q[c * LANES:(c + 1) * LANES, r * HD:(r + 1) * HD].T for r in range(R) for c in range(Q // LANES)], axis=1)
    qa_ref[HD:MR, :] = sl_ref[0]
    qa_ref[MR:, :] = jnp.zeros((AUX_W - MASK_COL0, RQ), qa_ref.dtype)
    qa = qa_ref[...]

    ok_c = (lax.broadcasted_iota(jnp.int32, (NC, Q), 0) * CMP_STRIDE + (CMP_LEN - 1)
            <= start + lax.broadcasted_iota(jnp.int32, (NC, Q), 1))
    s_c = _dot(jnp.concatenate([kc_ref[0, 0], auxc_ref[...]], axis=1), qa)
    p_c = _softmax2_cols(s_c, ok_c, R, Q)
    o_cmp = _dot(vct_ref[0, 0], p_c.astype(MXU_DTYPE))

    p_sum = p_c[:, 0:Q]
    for r in range(1, R):
        p_sum = p_sum + p_c[:, r * Q:(r + 1) * Q]
    ov = ov_ref[...]
    ps_hi, ps_mid, ps_lo = _split3(p_sum)
    imp = _dot(ov, ps_hi) + _dot(ov, ps_mid) + _dot(ov, ps_lo)

    blk = lax.broadcasted_iota(jnp.int32, (NS, Q), 0)
    tq1 = start + lax.broadcasted_iota(jnp.int32, (NS, Q), 1)
    cur = jnp.right_shift(tq1, SEL_LEN.bit_length() - 1)
    forced = (blk == 0) | (blk == cur) | (blk == cur - 1)
    score = jnp.where(forced, FORCE_SCORE, jnp.where(blk * SEL_LEN <= tq1, imp, -1.0))
    blk_f = blk.astype(f32)
    sel = jnp.zeros((NS, Q), f32)
    for _ in range(min(SEL_TOPK, NS)):
        m = jnp.max(score, axis=0, keepdims=True)
        first = jnp.min(jnp.where(score == m, blk_f, float(NS)), axis=0, keepdims=True)
        hit = blk_f == first
        score = jnp.where(hit, -jnp.inf, score)
        sel = jnp.where(hit, 1.0, sel)
    sel_past = jnp.where(blk * SEL_LEN < start, sel, 0.0)
    sel_ref[0] = sel
    sel_ref[1] = sel_past
    n_act = jnp.int32(0)
    for i in range(NT):
        idx_ref[n_act] = jnp.int32(i)
        n_act = n_act + (jnp.max(sel_past[i * BPT:(i + 1) * BPT, :]) > 0.0).astype(jnp.int32)

    WK = WINDOW + Q
    ws = pl.multiple_of(jnp.maximum(start - WINDOW, 0), Q)
    dist_w = (start - ws) + (lax.broadcasted_iota(jnp.int32, (WK, Q), 1)
                             - lax.broadcasted_iota(jnp.int32, (WK, Q), 0))
    ok_w = (dist_w >= 0) & (dist_w < WINDOW)
    s_w = _dot(jnp.concatenate([kw_ref[pl.ds(ws, WK), :], aux_ref[pl.ds(ws, WK), :]], axis=1), qa)
    s_w = _mask_heads(ok_w, s_w, R, Q)
    e_w = jnp.exp2(s_w - jnp.max(s_w, axis=0, keepdims=True)).astype(MXU_DTYPE)
    acc_w = _dot(vwt_ref[0, 0, :, pl.ds(ws, WK)], e_w)
    o_win = acc_w[0:HD, :] * (1.0 / acc_w[HD:HD + 1, :])

    def mask_rows(which, b0, valid):
        mrow = jnp.where(valid, (sel_ref[which, pl.ds(b0, BPT), :] - 1.0) * MASK_BIG, -MASK_BIG)
        mrow = jnp.concatenate([mrow] * R, axis=1)
        qa_ref[MR:MR + 2 * BPT, :] = jnp.concatenate([mrow, jnp.zeros_like(mrow)], axis=0).astype(qa_ref.dtype)

    q0 = pl.multiple_of(start, Q)
    mask_rows(0, pl.multiple_of((start // KT) * BPT, BPT), True)
    s_o = _dot(jnp.concatenate([ks_ref[pl.ds(q0, Q), :], aux_ref[pl.ds(q0, Q), :]], axis=1), qa_ref[...])
    ok_o = lax.broadcasted_iota(jnp.int32, (Q, Q), 0) <= lax.broadcasted_iota(jnp.int32, (Q, Q), 1)
    s_o = _mask_heads(ok_o, s_o, R, Q)
    m_o = jnp.max(s_o, axis=0, keepdims=True)
    m_ref[...] = m_o
    acc_ref[...] = _dot(vst_ref[0, 0, :, pl.ds(q0, Q)], jnp.exp2(s_o - m_o).astype(MXU_DTYPE))

    def tile_of(j):
        return idx_ref[jnp.clip(j, 0, jnp.maximum(n_act - 1, 0))]

    def scores(j, slot):
        i = tile_of(j)
        k0 = pl.multiple_of(i * KT, KT)
        mask_rows(1, pl.multiple_of(i * BPT, BPT), j < n_act)
        s_ref[slot] = _dot(jnp.concatenate([ks_ref[pl.ds(k0, KT), :], aux_ref[pl.ds(k0, KT), :]], axis=1),
                           qa_ref[...])

    def softmax(slot):
        s = s_ref[slot]
        m_old = m_ref[...]
        m_new = jnp.maximum(m_old, jnp.max(s, axis=0, keepdims=True))
        p_ref[slot] = jnp.exp2(s - m_new).astype(p_ref.dtype)
        al_ref[slot] = jnp.exp2(m_old - m_new)
        m_ref[...] = m_new

    def accumulate(j, slot):
        k0 = pl.multiple_of(tile_of(j) * KT, KT)
        acc_ref[...] = al_ref[slot] * acc_ref[...] + _dot(vst_ref[0, 0, :, pl.ds(k0, KT)], p_ref[slot])

    scores(0, 0)
    scores(1, 1)
    softmax(0)

    def pipe(k, c):
        j = 2 * k
        scores(j + 2, 0)
        softmax(1)
        accumulate(j, 0)
        scores(j + 3, 1)
        softmax(0)
        accumulate(j + 1, 1)
        return c

    lax.fori_loop(0, (n_act + 1) // 2, pipe, 0)
    acc_s = acc_ref[...]
    o_sel = acc_s[0:HD, :] * (1.0 / acc_s[HD:HD + 1, :])

    gates = _sigmoid(gt_ref[0])
    for r in range(R):
        cs = slice(r * Q, (r + 1) * Q)
        o_r = (gates[3 * r:3 * r + 1, :] * o_cmp[:, cs] + gates[3 * r + 1:3 * r + 2, :] * o_sel[:, cs]
               + gates[3 * r + 2:3 * r + 3, :] * o_win[:, cs])
        hs = slice(r * HD, (r + 1) * HD)
        o_ref[:, hs] = (o_r.T * _silu(z_ref[:, hs].astype(f32))).astype(o_ref.dtype)


def _aux_table(pos, onehot):
    hi = (pos // SEL_LEN) * SEL_LEN
    lo = pos % SEL_LEN
    col = jnp.arange(AUX_W)[None, :]
    t = jnp.where(col < 3, hi[:, None], jnp.where(col < 6, lo[:, None], 0)).astype(f32)
    if onehot:
        blk = (pos // SEL_LEN) % (SEL_KV_TILE // SEL_LEN)
        t = t + jnp.where(col == MASK_COL0 + blk[:, None], 1.0, 0.0)
    return t.astype(MXU_DTYPE)


def _slope_rows(R, G):
    sl = jnp.exp2(-8.0 * (jnp.arange(NSA_HEADS, dtype=f32) + 1.0) / NSA_HEADS) * LOG2E
    parts = _split3(sl)
    rows = jnp.stack(parts + parts, axis=0).astype(f32)
    rows = jnp.repeat(rows.reshape(6, G, R).transpose(1, 0, 2), NSA_Q, axis=2)
    pad = jnp.zeros((G, MASK_COL0 - 6, R * NSA_Q), f32)
    return jnp.concatenate([rows, pad], axis=1).astype(MXU_DTYPE)


def _overlap_table(NS, NC):
    jj = jnp.arange(NS)[:, None] * SEL_LEN
    nn = jnp.arange(NC)[None, :] * CMP_STRIDE
    return jnp.where((nn < jj + SEL_LEN) & (nn + (CMP_LEN - 1) >= jj), 1.0, 0.0).astype(MXU_DTYPE)


def _with_ones_row(v_t):
    shp = v_t.shape[:-2] + (1, v_t.shape[-1])
    zshp = v_t.shape[:-2] + (V_AUG - 1, v_t.shape[-1])
    return jnp.concatenate([v_t, jnp.ones(shp, v_t.dtype), jnp.zeros(zshp, v_t.dtype)], axis=-2)


def _nsa_attention(h_nsa, c0, bw, k_cmp, v_cmp_t, vs_t, vw_t, gates_t, B, S):
    T = B * S
    G, R = NSA_GROUPS, NSA_REP
    HD = bw // NSA_HEADS
    assert c0 % (R * HD) == 0
    qc0 = c0 // (R * HD)
    Q = NSA_Q
    NQ = S // Q
    NC = S // CMP_STRIDE
    NS = S // SEL_LEN
    RQ = R * Q
    KT = SEL_KV_TILE
    kv0 = (c0 + bw) // HD
    assert S % KT == 0 and S % Q == 0 and KT % Q == 0 and S >= WINDOW + Q
    assert S + CMP_LEN <= SEL_LEN * 256 and 2 * (KT // SEL_LEN) <= AUX_W - MASK_COL0
    aux_s = _aux_table(jnp.arange(S), True)
    aux_c = _aux_table(jnp.arange(NC) * CMP_STRIDE + (CMP_LEN - 1), False)
    HA = HD + V_AUG
    return pl.pallas_call(
        functools.partial(_nsa_kernel, S=S),
        out_shape=jax.ShapeDtypeStruct((T, bw), MXU_DTYPE),
        grid=(B, G, NQ),
        in_specs=[
            pl.BlockSpec((Q, R * HD), lambda b, g, i: (b * NQ + i, qc0 + g)),
            pl.BlockSpec((1, 1, NC, HD), lambda b, g, i: (b, g, 0, 0)),
            pl.BlockSpec((NC, AUX_W), lambda b, g, i: (0, 0)),
            pl.BlockSpec((1, 1, HD, NC), lambda b, g, i: (b, g, 0, 0)),
            pl.BlockSpec((NS, NC), lambda b, g, i: (0, 0)),
            pl.BlockSpec((S, HD), lambda b, g, i: (b, kv0 + 2 * G + g)),
            pl.BlockSpec((S, AUX_W), lambda b, g, i: (0, 0)),
            pl.BlockSpec((1, 1, HA, S), lambda b, g, i: (b, g, 0, 0)),
            pl.BlockSpec((S, HD), lambda b, g, i: (b, kv0 + 4 * G + g)),
            pl.BlockSpec((1, 1, HA, S), lambda b, g, i: (b, g, 0, 0)),
            pl.BlockSpec((1, 16, Q), lambda b, g, i: (g, 0, b * NQ + i)),
            pl.BlockSpec((1, MASK_COL0, RQ), lambda b, g, i: (g, 0, 0)),
            pl.BlockSpec((Q, R * HD), lambda b, g, i: (b * NQ + i, qc0 + (bw + 6 * G * HD) // (R * HD) + g)),
        ],
        out_specs=pl.BlockSpec((Q, R * HD), lambda b, g, i: (b * NQ + i, g)),
        scratch_shapes=[
            pltpu.VMEM((HD + AUX_W, RQ), MXU_DTYPE),
            pltpu.VMEM((2, NS, Q), f32),
            pltpu.VMEM((1, RQ), f32),
            pltpu.VMEM((HA, RQ), f32),
            pltpu.VMEM((2, KT, RQ), f32),
            pltpu.VMEM((2, KT, RQ), MXU_DTYPE),
            pltpu.VMEM((2, 1, RQ), f32),
            pltpu.SMEM((S // KT + 1,), jnp.int32),
        ],
        compiler_params=_params("parallel", "parallel", "arbitrary"),
        name="nsa_attention",
    )(h_nsa, k_cmp, aux_c, v_cmp_t, _overlap_table(NS, NC), h_nsa, aux_s, _with_ones_row(vs_t), h_nsa,
      _with_ones_row(vw_t), gates_t, _slope_rows(R, G), h_nsa)


def _mem_kernel(q_ref, z_ref, kv_ref, o_ref):
    hw = q_ref.shape[1] // MEM_HEADS
    bw = q_ref.shape[1]
    for h in range(MEM_HEADS):
        cs = slice(h * hw, (h + 1) * hw)
        s = _dot_nt(q_ref[:, cs], kv_ref[:, cs])
        m = jnp.max(s, axis=-1, keepdims=True)
        e = jnp.exp(s - m)
        p = e * (1.0 / jnp.sum(e, axis=-1, keepdims=True))
        o = _dot(p.astype(MXU_DTYPE), kv_ref[:, bw + h * hw:bw + (h + 1) * hw])
        o_ref[:, cs] = (o * _silu(z_ref[:, cs].astype(f32))).astype(o_ref.dtype)


def _mem_attention(h_mem, c0, bw, kv, B, S):
    T = B * S
    M = kv.shape[0] // B
    tq = min(512, S)
    nb = S // tq
    assert c0 % bw == 0
    qc = c0 // bw
    return pl.pallas_call(
        _mem_kernel,
        out_shape=jax.ShapeDtypeStruct((T, bw), MXU_DTYPE),
        grid=(B, nb),
        in_specs=[
            pl.BlockSpec((tq, bw), lambda b, i: (b * nb + i, qc)),
            pl.BlockSpec((tq, bw), lambda b, i: (b * nb + i, qc + 1)),
            pl.BlockSpec((M, 2 * bw), lambda b, i: (b, 0)),
        ],
        out_specs=pl.BlockSpec((tq, bw), lambda b, i: (b * nb + i, 0)),
        compiler_params=_params("parallel", "parallel"),
        name="mem_attention",
    )(h_mem, h_mem, kv)


def _merge_kernel(og_ref, on_ref, om_ref, wg_ref, wn_ref, wm_ref, ag_ref, an_ref, am_ref, o_ref):
    y = ag_ref[...].astype(f32) * _dot(og_ref[...], wg_ref[...])
    y = y + an_ref[...].astype(f32) * _dot(on_ref[...], wn_ref[...])
    y = y + am_ref[...].astype(f32) * _dot(om_ref[...], wm_ref[...])
    o_ref[...] = y.astype(o_ref.dtype)


def _merge(o_gla, o_nsa, o_mem, w_g, w_n, w_m, a, c0):
    T, bw = o_gla.shape
    D = w_g.shape[1]
    tm, tn = min(512, T), min(1024, D)
    nj = D // tn
    assert c0 % tn == 0
    osp = pl.BlockSpec((tm, bw), lambda j, i: (i, 0))
    wsp = pl.BlockSpec((bw, tn), lambda j, i: (0, j))
    asp = lambda c: pl.BlockSpec((tm, tn), lambda j, i: (i, c0 // tn + c * nj + j))
    return pl.pallas_call(
        _merge_kernel,
        out_shape=jax.ShapeDtypeStruct((T, D), MXU_DTYPE),
        grid=(nj, T // tm),
        in_specs=[osp, osp, osp, wsp, wsp, wsp, asp(0), asp(1), asp(2)],
        out_specs=pl.BlockSpec((tm, tn), lambda j, i: (i, j)),
        compiler_params=_params("parallel", "parallel"),
        name="branch_merge",
    )(o_gla, o_nsa, o_mem, w_g, w_n, w_m, a, a, a)


def _out_ln_kernel(m_ref, w_ref, x_ref, g_ref, b_ref, o_ref, *, alpha, tn):
    j = pl.program_id(1)
    c0 = pl.multiple_of(j * tn, tn)
    o_ref[:, pl.ds(c0, tn)] = alpha * x_ref[...] + _dot(m_ref[...], w_ref[...])

    @pl.when(j == pl.num_programs(1) - 1)
    def _():
        def ln_rows(c, _):
            rows = pl.ds(pl.multiple_of(c * LN_ROWS, LN_ROWS), LN_ROWS)
            z = o_ref[rows, :]
            mu = jnp.mean(z, axis=-1, keepdims=True)
            zc = z - mu
            var = jnp.mean(zc * zc, axis=-1, keepdims=True)
            o_ref[rows, :] = zc * lax.rsqrt(var + LN_EPS) * g_ref[...] + b_ref[...]
            return 0

        lax.fori_loop(0, o_ref.shape[0] // LN_ROWS, ln_rows, 0)


def _out_ln(merged, w_out, x2, ln_g, ln_b, alpha):
    T, D = x2.shape
    tm, tn = min(512, T), min(1024, D)
    return pl.pallas_call(
        functools.partial(_out_ln_kernel, alpha=alpha, tn=tn),
        out_shape=jax.ShapeDtypeStruct((T, D), x2.dtype),
        grid=(T // tm, D // tn),
        in_specs=[
            pl.BlockSpec((tm, D), lambda i, j: (i, 0)),
            pl.BlockSpec((D, tn), lambda i, j: (0, j)),
            pl.BlockSpec((tm, tn), lambda i, j: (i, j)),
            pl.BlockSpec((1, D), lambda i, j: (0, 0)),
            pl.BlockSpec((1, D), lambda i, j: (0, 0)),
        ],
        out_specs=pl.BlockSpec((tm, D), lambda i, j: (i, 0)),
        compiler_params=_params("parallel", "arbitrary"),
        name="out_proj_layernorm",
    )(merged, w_out, x2, ln_g.reshape(1, D), ln_b.reshape(1, D))


def _layer(x, mem, w_in, b_merge, gla_w_a2, gla_b_a, gla_norm_g, nsa_pe_k, nsa_pe_v, nsa_wk1, nsa_wk2,
           nsa_wv1, nsa_wv2, w_mem_kv, w_br_gla, w_br_nsa, w_br_mem, w_out, ln_g, ln_b, depth):
    B, S, D = x.shape
    T = B * S
    bw = D // 2
    gk = bw // 2
    G, R = NSA_GROUPS, NSA_REP
    HD = bw // NSA_HEADS
    kvw = G * HD
    cdt = MXU_DTYPE

    o_ga = 2 * gk + 2 * bw
    o_nq = o_ga + GLA_LOWRANK
    o_nbg = o_nq + bw + 6 * kvw + bw
    o_mq = o_nbg + 3 * NSA_HEADS
    o_mrg = o_mq + 2 * bw
    assert w_in.shape[1] == o_mrg + N_BRANCH * D

    x2 = x.reshape(T, D)
    xb = x2.astype(cdt)
    ones = lambda n: jnp.ones((n,), f32)

    w_all = _repack(w_in, [(0, o_ga), (o_mq, 2 * bw), (o_nq, 2 * bw + 6 * kvw), (o_mrg, N_BRANCH * D)])
    c_gla, c_mem = 0, o_ga
    c_nsa = c_mem + 2 * bw
    c_mrg = c_nsa + 2 * bw + 6 * kvw
    dk, mhd = gk // GLA_HEADS, bw // MEM_HEADS
    scale = jnp.concatenate([jnp.full((gk,), dk ** -0.5, f32), ones(o_ga - gk),
                             jnp.full((bw,), mhd ** -0.5, f32), ones(bw),
                             jnp.full((bw,), HD ** -0.5 * LOG2E, f32), ones(6 * kvw + bw)])
    h = _project(xb, w_all, scale, cdt, n=c_mrg, name="proj_in")
    a = _project(xb, w_all, b_merge, cdt, c0=c_mrg, gate=True, name="proj_merge_gates")
    h_small = _project_small(xb, w_in, o_ga, GLA_LOWRANK, o_nbg, 3 * NSA_HEADS)

    wa_pad = jnp.concatenate([gla_w_a2, jnp.zeros((SMALL_W - GLA_LOWRANK, gk), f32)], axis=0)
    o_gla = _gla(h, c_gla, gk, h_small, wa_pad, gla_b_a, gla_norm_g, B, S)

    def grouped(c0):
        return h[:, c0:c0 + kvw].reshape(B, S, G, HD).transpose(0, 2, 1, 3)

    NC = S // CMP_STRIDE
    c_kv = c_nsa + bw
    k_cmp = _compress(grouped(c_kv).reshape(B, G, NC, CMP_STRIDE * HD), nsa_pe_k, nsa_wk1, nsa_wk2)
    v_cmp = _compress(grouped(c_kv + kvw).reshape(B, G, NC, CMP_STRIDE * HD), nsa_pe_v, nsa_wv1, nsa_wv2)
    v_cmp_t = v_cmp.transpose(0, 1, 3, 2)
    vs_t = grouped(c_kv + 3 * kvw).transpose(0, 1, 3, 2)
    vw_t = grouped(c_kv + 5 * kvw).transpose(0, 1, 3, 2)
    gl = h_small[:, GLA_LOWRANK:GLA_LOWRANK + 3 * NSA_HEADS].reshape(T, G, 3 * R).transpose(1, 2, 0)
    gates_t = jnp.concatenate([gl, jnp.zeros((G, 16 - 3 * R, T), f32)], axis=1)
    o_nsa = _nsa_attention(h, c_nsa, bw, k_cmp, v_cmp_t, vs_t, vw_t, gates_t, B, S)

    M = mem.shape[1]
    kv = _project(mem.reshape(B * M, D).astype(cdt), w_mem_kv.astype(cdt), ones(2 * bw), cdt, name="proj_mem_kv")
    o_mem = _mem_attention(h, c_mem, bw, kv, B, S)

    merged = _merge(o_gla, o_nsa, o_mem, w_br_gla.astype(cdt), w_br_nsa.astype(cdt), w_br_mem.astype(cdt), a, 0)
    alpha = (2 * depth) ** 0.25
    return _out_ln(merged, w_out.astype(cdt), x2, ln_g, ln_b, alpha).reshape(B, S, D)


def kernel(x, mem, w_in, b_merge, gla_w_a2, gla_b_a, gla_norm_g, nsa_pe_k, nsa_pe_v, nsa_wk1, nsa_wk2, nsa_wv1, nsa_wv2, w_mem_kv, w_br_gla, w_br_nsa, w_br_mem, w_out, ln_g, ln_b):
    depth = w_in.shape[0]
    for l in range(depth):
        x = _layer(x, mem, w_in[l], b_merge[l], gla_w_a2[l], gla_b_a[l], gla_norm_g[l], nsa_pe_k[l], nsa_pe_v[l],
                   nsa_wk1[l], nsa_wk2[l], nsa_wv1[l], nsa_wv2[l], w_mem_kv[l], w_br_gla[l], w_br_nsa[l],
                   w_br_mem[l], w_out[l], ln_g[l], ln_b[l], depth)
    return x
```

```python
import functools

import jax
import jax.numpy as jnp
from jax import lax
from jax.experimental import pallas as pl
from jax.experimental.pallas import tpu as pltpu

N_BRANCH = 3
GLA_HEADS = 4
GLA_LOWRANK = 16
GLA_TAU = 16.0
GLA_CHUNK = 64
NSA_HEADS = 16
NSA_GROUPS = 4
NSA_REP = NSA_HEADS // NSA_GROUPS
CMP_LEN = 32
CMP_STRIDE = 16
SEL_LEN = 64
SEL_TOPK = 16
WINDOW = 512
Q_BLOCK = 128
FORCE_SCORE = 1e4
MEM_HEADS = 4
LN_EPS = 1e-5
RMS_EPS = 1e-6
NEG_INF = -1e30
LOG2E = 1.4426950408889634

LANES = 128
VMEM_LIMIT_BYTES = 56 * 1024 * 1024
MXU_DTYPE = jnp.bfloat16

SEL_KV_TILE = 512
NSA_Q = 256
GLA_STEP_CHUNKS = 4
LN_ROWS = 64
SMALL_W = LANES
AUX_W = LANES
MASK_COL0 = 16
V_AUG = 8
MASK_BIG = -NEG_INF

f32 = jnp.float32


def _dot(a, b):
    return jnp.dot(a, b, preferred_element_type=f32)


def _dot_nt(a, b):
    return lax.dot_general(a, b, (((1,), (1,)), ((), ())), preferred_element_type=f32)


def _dot_tn(a, b):
    return lax.dot_general(a, b, (((0,), (0,)), ((), ())), preferred_element_type=f32)


def _sigmoid(x):
    return 1.0 / (1.0 + jnp.exp(-x))


def _silu(x):
    return x * _sigmoid(x)


def _log_sigmoid(x):
    return -(jnp.maximum(-x, 0.0) + jnp.log1p(jnp.exp(-jnp.abs(x))))


def _split2(x):
    hi = x.astype(MXU_DTYPE)
    lo = (x - hi.astype(f32)).astype(MXU_DTYPE)
    return hi, lo


def _split3(x):
    hi = x.astype(MXU_DTYPE)
    r1 = x - hi.astype(f32)
    mid = r1.astype(MXU_DTYPE)
    lo = (r1 - mid.astype(f32)).astype(MXU_DTYPE)
    return hi, mid, lo


def _params(*sem):
    return pltpu.CompilerParams(dimension_semantics=sem, vmem_limit_bytes=VMEM_LIMIT_BYTES)


def _proj_kernel(x_ref, w_ref, r_ref, o_ref, *, gate, nt):
    acc = _dot_nt(x_ref[...], w_ref[...]) if nt else _dot(x_ref[...], w_ref[...])
    if gate:
        o_ref[...] = _sigmoid(acc + r_ref[...]).astype(o_ref.dtype)
    else:
        o_ref[...] = (acc * r_ref[...]).astype(o_ref.dtype)


def _project(x, w, row, out_dtype, c0=0, n=None, gate=False, nt=False, name="proj"):
    M, K = x.shape
    n = (w.shape[0] if nt else w.shape[1]) - c0 if n is None else n
    bm = min(1024, M)
    bn = min(1024, n)
    assert M % bm == 0 and n % bn == 0 and c0 % bn == 0
    wspec = (pl.BlockSpec((bn, K), lambda j, i: (c0 // bn + j, 0)) if nt
             else pl.BlockSpec((K, bn), lambda j, i: (0, c0 // bn + j)))
    return pl.pallas_call(
        functools.partial(_proj_kernel, gate=gate, nt=nt),
        out_shape=jax.ShapeDtypeStruct((M, n), out_dtype),
        grid=(n // bn, M // bm),
        in_specs=[pl.BlockSpec((bm, K), lambda j, i: (i, 0)), wspec, pl.BlockSpec((1, bn), lambda j, i: (0, j))],
        out_specs=pl.BlockSpec((bm, bn), lambda j, i: (i, j)),
        compiler_params=_params("parallel", "parallel"),
        name=name,
    )(x, w, row.reshape(1, n).astype(f32))


def _row_window(rows, width, start16):
    return pl.BlockSpec((pl.Element(rows), pl.Element(width)), lambda *g: (start16(*g) * 16, 0))


def _proj_small_kernel(x_ref, wa_ref, wb_ref, o_ref):
    pad = jnp.zeros((LANES - wa_ref.shape[0] - wb_ref.shape[0], wa_ref.shape[1]), f32)
    w = jnp.concatenate([wa_ref[...], wb_ref[...], pad], axis=0)
    o_ref[...] = _dot_nt(x_ref[...], w.astype(x_ref.dtype))


def _project_small(x, w_t, ra, na, rb, nb):
    M, K = x.shape
    bm = min(1024, M)
    assert ra % 16 == 0 and rb % 16 == 0 and na % 8 == 0 and nb % 8 == 0 and na + nb <= LANES
    return pl.pallas_call(
        _proj_small_kernel,
        out_shape=jax.ShapeDtypeStruct((M, LANES), f32),
        grid=(M // bm,),
        in_specs=[
            pl.BlockSpec((bm, K), lambda i: (i, 0)),
            _row_window(na, K, lambda i: ra // 16),
            _row_window(nb, K, lambda i: rb // 16),
        ],
        out_specs=pl.BlockSpec((bm, LANES), lambda i: (i, 0)),
        compiler_params=_params("parallel"),
        name="proj_small",
    )(x, w_t, w_t)


def _repack_kernel(w_ref, o_ref):
    o_ref[...] = w_ref[...].astype(o_ref.dtype)


def _repack(w_t, segments, bn=1024):
    N, K = w_t.shape
    segs, blk = [], 0
    for src, n in segments:
        assert n % bn == 0 and src % 16 == 0 and src + n <= N
        segs.append((blk, blk + n // bn, src // 16 - blk * (bn // 16)))
        blk += n // bn

    def start16(j):
        s = jnp.int32(0)
        for lo, hi, off in segs:
            s = jnp.where((j >= lo) & (j < hi), j * (bn // 16) + off, s)
        return s

    return pl.pallas_call(
        _repack_kernel,
        out_shape=jax.ShapeDtypeStruct((blk * bn, K), MXU_DTYPE),
        grid=(blk,),
        in_specs=[_row_window(bn, K, start16)],
        out_specs=pl.BlockSpec((bn, K), lambda j: (j, 0)),
        compiler_params=_params("parallel"),
        name="repack_w_in",
    )(w_t)


def _gla_kernel(q_ref, k_ref, v_ref, z_ref, ga_ref, wa_ref, ba_ref, ng_ref, o_ref, st_ref, *, dk, dv):
    C = GLA_CHUNK

    @pl.when(pl.program_id(1) == 0)
    def _():
        st_ref[...] = jnp.zeros_like(st_ref)

    row = lax.broadcasted_iota(jnp.int32, (C, C), 0)
    col = lax.broadcasted_iota(jnp.int32, (C, C), 1)
    tril = row >= col
    ltri = jnp.where(tril, 1.0, 0.0).astype(MXU_DTYPE)
    wa_hi, wa_lo = _split2(wa_ref[...])
    for c in range(GLA_STEP_CHUNKS):
        rows = slice(c * C, (c + 1) * C)
        ga_hi, ga_lo = _split2(ga_ref[rows, :])
        zz = _dot(ga_hi, wa_hi) + _dot(ga_lo, wa_hi) + _dot(ga_hi, wa_lo) + ba_ref[...]
        la = _log_sigmoid(zz) * (1.0 / GLA_TAU)
        la_hi, la_mid, la_lo = _split3(la)
        bcum = _dot(ltri, la_hi) + _dot(ltri, la_mid) + _dot(ltri, la_lo)
        for h in range(GLA_HEADS):
            kc = slice(h * dk, (h + 1) * dk)
            vc = slice(h * dv, (h + 1) * dv)
            b = bcum[:, kc]
            bl = b[C - 1:C, :]
            qh = q_ref[rows, kc].astype(f32)
            kh = k_ref[rows, kc].astype(f32)
            vh = v_ref[rows, vc]
            q_d = (qh * jnp.exp(b)).astype(MXU_DTYPE)
            k_d = (kh * jnp.exp(-b)).astype(MXU_DTYPE)
            k_e = (kh * jnp.exp(bl - b)).astype(MXU_DTYPE)
            att = jnp.where(tril, _dot_nt(q_d, k_d), 0.0)
            st = st_ref[h]
            o = _dot(att.astype(MXU_DTYPE), vh) + _dot_nt(q_d, st.astype(MXU_DTYPE))
            st_ref[h] = st * jnp.exp(bl) + _dot_tn(vh, k_e)
            ms = jnp.mean(o * o, axis=-1, keepdims=True)
            on = o * lax.rsqrt(ms + RMS_EPS) * ng_ref[...]
            zg = z_ref[rows, vc].astype(f32)
            o_ref[rows, vc] = (on * _silu(zg)).astype(o_ref.dtype)


def _gla(h, c0, kw, h_small, wa_pad, b_a, norm_g, B, S):
    T = B * S
    dk, dv = kw // GLA_HEADS, 2 * kw // GLA_HEADS
    cs = GLA_STEP_CHUNKS * GLA_CHUNK
    nb = S // cs
    assert S % cs == 0 and c0 % (2 * kw) == 0
    h_gla = h
    rowmap = lambda col: (lambda b, i: (b * nb + i, col))
    qc, vc = c0 // kw, c0 // (2 * kw)
    return pl.pallas_call(
        functools.partial(_gla_kernel, dk=dk, dv=dv),
        out_shape=jax.ShapeDtypeStruct((T, 2 * kw), MXU_DTYPE),
        grid=(B, nb),
        in_specs=[
            pl.BlockSpec((cs, kw), rowmap(qc)),
            pl.BlockSpec((cs, kw), rowmap(qc + 1)),
            pl.BlockSpec((cs, 2 * kw), rowmap(vc + 1)),
            pl.BlockSpec((cs, 2 * kw), rowmap(vc + 2)),
            pl.BlockSpec((cs, SMALL_W), rowmap(0)),
            pl.BlockSpec((SMALL_W, kw), lambda b, i: (0, 0)),
            pl.BlockSpec((1, kw), lambda b, i: (0, 0)),
            pl.BlockSpec((1, dv), lambda b, i: (0, 0)),
        ],
        out_specs=pl.BlockSpec((cs, 2 * kw), rowmap(0)),
        scratch_shapes=[pltpu.VMEM((GLA_HEADS, dv, dk), f32)],
        compiler_params=_params("parallel", "arbitrary"),
        name="gla",
    )(h_gla, h_gla, h_gla, h_gla, h_small, wa_pad, b_a.reshape(1, kw), norm_g.reshape(1, dv))


def _compress_kernel(x_ref, pe_ref, w1_ref, w2_ref, o_ref):
    x = x_ref[0, 0].astype(f32)
    half = x.shape[1]
    xa = (x + pe_ref[0:1, :]).astype(MXU_DTYPE)
    xb = (x + pe_ref[1:2, :]).astype(MXU_DTYPE)
    ya = _dot(xa, w1_ref[0:half, :])
    yb = _dot(xb, w1_ref[half:2 * half, :])
    nc = x.shape[0]
    pre = ya + pltpu.roll(yb, nc - 1, 0)
    o_ref[0, 0] = _dot(_silu(pre).astype(MXU_DTYPE), w2_ref[...]).astype(o_ref.dtype)


def _compress(xblk, pe, w1, w2):
    B, G, NC, W = xblk.shape
    hd = w2.shape[0]
    return pl.pallas_call(
        _compress_kernel,
        out_shape=jax.ShapeDtypeStruct((B, G, NC, hd), MXU_DTYPE),
        grid=(B, G),
        in_specs=[
            pl.BlockSpec((1, 1, NC, W), lambda b, g: (b, g, 0, 0)),
            pl.BlockSpec((2, W), lambda b, g: (0, 0)),
            pl.BlockSpec((2 * W, hd), lambda b, g: (0, 0)),
            pl.BlockSpec((hd, hd), lambda b, g: (0, 0)),
        ],
        out_specs=pl.BlockSpec((1, 1, NC, hd), lambda b, g: (b, g, 0, 0)),
        compiler_params=_params("parallel", "parallel"),
        name="nsa_compress",
    )(xblk, pe.reshape(2, W).astype(f32), w1.astype(MXU_DTYPE), w2.astype(MXU_DTYPE))


def _mask_heads(ok, s, R, Q, fill=NEG_INF):
    return jnp.concatenate([jnp.where(ok, s[:, r * Q:(r + 1) * Q], fill) for r in range(R)], axis=1)


def _softmax2_cols(s, ok, R, Q):
    s = _mask_heads(ok, s, R, Q)
    m = jnp.max(s, axis=0, keepdims=True)
    e = _mask_heads(ok, jnp.exp2(s - m), R, Q, 0.0)
    den = jnp.sum(e, axis=0, keepdims=True)
    return e * jnp.where(den > 0.0, 1.0 / den, 0.0)


def _nsa_kernel(q_ref, kc_ref, auxc_ref, vct_ref, ov_ref, ks_ref, aux_ref, vst_ref, kw_ref, vwt_ref, gt_ref,
                sl_ref, z_ref, o_ref, qa_ref, sel_ref, m_ref, acc_ref, s_ref, p_ref, al_ref, idx_ref, *, S):
    Q, R = NSA_Q, NSA_REP
    HD = q_ref.shape[1] // R
    RQ = R * Q
    NC = S // CMP_STRIDE
    NS = S // SEL_LEN
    KT = SEL_KV_TILE
    BPT = KT // SEL_LEN
    NT = S // KT
    MR = HD + MASK_COL0
    qb = pl.program_id(2)
    start = qb * Q

    q = q_ref[...]
    qa_ref[0:HD, :] = jnp.concatenate(
        [q[c * LANES:(c + 1) * LANES, r * HD:(r + 1) * HD].T for r in range(R) for c in range(Q // LANES)], axis=1)
    qa_ref[HD:MR, :] = sl_ref[0]
    qa_ref[MR:, :] = jnp.zeros((AUX_W - MASK_COL0, RQ), qa_ref.dtype)
    qa = qa_ref[...]

    ok_c = (lax.broadcasted_iota(jnp.int32, (NC, Q), 0) * CMP_STRIDE + (CMP_LEN - 1)
            <= start + lax.broadcasted_iota(jnp.int32, (NC, Q), 1))
    s_c = _dot(jnp.concatenate([kc_ref[0, 0], auxc_ref[...]], axis=1), qa)
    p_c = _softmax2_cols(s_c, ok_c, R, Q)
    o_cmp = _dot(vct_ref[0, 0], p_c.astype(MXU_DTYPE))

    p_sum = p_c[:, 0:Q]
    for r in range(1, R):
        p_sum = p_sum + p_c[:, r * Q:(r + 1) * Q]
    ov = ov_ref[...]
    ps_hi, ps_mid, ps_lo = _split3(p_sum)
    imp = _dot(ov, ps_hi) + _dot(ov, ps_mid) + _dot(ov, ps_lo)

    blk = lax.broadcasted_iota(jnp.int32, (NS, Q), 0)
    tq1 = start + lax.broadcasted_iota(jnp.int32, (NS, Q), 1)
    cur = jnp.right_shift(tq1, SEL_LEN.bit_length() - 1)
    forced = (blk == 0) | (blk == cur) | (blk == cur - 1)
    score = jnp.where(forced, FORCE_SCORE, jnp.where(blk * SEL_LEN <= tq1, imp, -1.0))
    blk_f = blk.astype(f32)
    sel = jnp.zeros((NS, Q), f32)
    for _ in range(min(SEL_TOPK, NS)):
        m = jnp.max(score, axis=0, keepdims=True)
        first = jnp.min(jnp.where(score == m, blk_f, float(NS)), axis=0, keepdims=True)
        hit = blk_f == first
        score = jnp.where(hit, -jnp.inf, score)
        sel = jnp.where(hit, 1.0, sel)
    sel_past = jnp.where(blk * SEL_LEN < start, sel, 0.0)
    sel_ref[0] = sel
    sel_ref[1] = sel_past
    n_act = jnp.int32(0)
    for i in range(NT):
        idx_ref[n_act] = jnp.int32(i)
        n_act = n_act + (jnp.max(sel_past[i * BPT:(i + 1) * BPT, :]) > 0.0).astype(jnp.int32)

    WK = WINDOW + Q
    ws = pl.multiple_of(jnp.maximum(start - WINDOW, 0), Q)
    dist_w = (start - ws) + (lax.broadcasted_iota(jnp.int32, (WK, Q), 1)
                             - lax.broadcasted_iota(jnp.int32, (WK, Q), 0))
    ok_w = (dist_w >= 0) & (dist_w < WINDOW)
    s_w = _dot(jnp.concatenate([kw_ref[pl.ds(ws, WK), :], aux_ref[pl.ds(ws, WK), :]], axis=1), qa)
    s_w = _mask_heads(ok_w, s_w, R, Q)
    e_w = jnp.exp2(s_w - jnp.max(s_w, axis=0, keepdims=True)).astype(MXU_DTYPE)
    acc_w = _dot(vwt_ref[0, 0, :, pl.ds(ws, WK)], e_w)
    o_win = acc_w[0:HD, :] * (1.0 / acc_w[HD:HD + 1, :])

    def mask_rows(which, b0, valid):
        mrow = jnp.where(valid, (sel_ref[which, pl.ds(b0, BPT), :] - 1.0) * MASK_BIG, -MASK_BIG)
        mrow = jnp.concatenate([mrow] * R, axis=1)
        qa_ref[MR:MR + 2 * BPT, :] = jnp.concatenate([mrow, jnp.zeros_like(mrow)], axis=0).astype(qa_ref.dtype)

    q0 = pl.multiple_of(start, Q)
    mask_rows(0, pl.multiple_of((start // KT) * BPT, BPT), True)
    s_o = _dot(jnp.concatenate([ks_ref[pl.ds(q0, Q), :], aux_ref[pl.ds(q0, Q), :]], axis=1), qa_ref[...])
    ok_o = lax.broadcasted_iota(jnp.int32, (Q, Q), 0) <= lax.broadcasted_iota(jnp.int32, (Q, Q), 1)
    s_o = _mask_heads(ok_o, s_o, R, Q)
    m_o = jnp.max(s_o, axis=0, keepdims=True)
    m_ref[...] = m_o
    acc_ref[...] = _dot(vst_ref[0, 0, :, pl.ds(q0, Q)], jnp.exp2(s_o - m_o).astype(MXU_DTYPE))

    def tile_of(j):
        return idx_ref[jnp.clip(j, 0, jnp.maximum(n_act - 1, 0))]

    def scores(j, slot):
        i = tile_of(j)
        k0 = pl.multiple_of(i * KT, KT)
        mask_rows(1, pl.multiple_of(i * BPT, BPT), j < n_act)
        s_ref[slot] = _dot(jnp.concatenate([ks_ref[pl.ds(k0, KT), :], aux_ref[pl.ds(k0, KT), :]], axis=1),
                           qa_ref[...])

    def softmax(slot):
        s = s_ref[slot]
        m_old = m_ref[...]
        m_new = jnp.maximum(m_old, jnp.max(s, axis=0, keepdims=True))
        p_ref[slot] = jnp.exp2(s - m_new).astype(p_ref.dtype)
        al_ref[slot] = jnp.exp2(m_old - m_new)
        m_ref[...] = m_new

    def accumulate(j, slot):
        k0 = pl.multiple_of(tile_of(j) * KT, KT)
        acc_ref[...] = al_ref[slot] * acc_ref[...] + _dot(vst_ref[0, 0, :, pl.ds(k0, KT)], p_ref[slot])

    scores(0, 0)
    scores(1, 1)
    softmax(0)

    def pipe(k, c):
        j = 2 * k
        scores(j + 2, 0)
        softmax(1)
        accumulate(j, 0)
        scores(j + 3, 1)
        softmax(0)
        accumulate(j + 1, 1)
        return c

    lax.fori_loop(0, (n_act + 1) // 2, pipe, 0)
    acc_s = acc_ref[...]
    o_sel = acc_s[0:HD, :] * (1.0 / acc_s[HD:HD + 1, :])

    gates = _sigmoid(gt_ref[0])
    for r in range(R):
        cs = slice(r * Q, (r + 1) * Q)
        o_r = (gates[3 * r:3 * r + 1, :] * o_cmp[:, cs] + gates[3 * r + 1:3 * r + 2, :] * o_sel[:, cs]
               + gates[3 * r + 2:3 * r + 3, :] * o_win[:, cs])
        hs = slice(r * HD, (r + 1) * HD)
        o_ref[:, hs] = (o_r.T * _silu(z_ref[:, hs].astype(f32))).astype(o_ref.dtype)


def _aux_table(pos, onehot):
    hi = (pos // SEL_LEN) * SEL_LEN
    lo = pos % SEL_LEN
    col = jnp.arange(AUX_W)[None, :]
    t = jnp.where(col < 3, hi[:, None], jnp.where(col < 6, lo[:, None], 0)).astype(f32)
    if onehot:
        blk = (pos // SEL_LEN) % (SEL_KV_TILE // SEL_LEN)
        t = t + jnp.where(col == MASK_COL0 + blk[:, None], 1.0, 0.0)
    return t.astype(MXU_DTYPE)


def _slope_rows(R, G):
    sl = jnp.exp2(-8.0 * (jnp.arange(NSA_HEADS, dtype=f32) + 1.0) / NSA_HEADS) * LOG2E
    parts = _split3(sl)
    rows = jnp.stack(parts + parts, axis=0).astype(f32)
    rows = jnp.repeat(rows.reshape(6, G, R).transpose(1, 0, 2), NSA_Q, axis=2)
    pad = jnp.zeros((G, MASK_COL0 - 6, R * NSA_Q), f32)
    return jnp.concatenate([rows, pad], axis=1).astype(MXU_DTYPE)


def _overlap_table(NS, NC):
    jj = jnp.arange(NS)[:, None] * SEL_LEN
    nn = jnp.arange(NC)[None, :] * CMP_STRIDE
    return jnp.where((nn < jj + SEL_LEN) & (nn + (CMP_LEN - 1) >= jj), 1.0, 0.0).astype(MXU_DTYPE)


def _with_ones_row(v_t):
    shp = v_t.shape[:-2] + (1, v_t.shape[-1])
    zshp = v_t.shape[:-2] + (V_AUG - 1, v_t.shape[-1])
    return jnp.concatenate([v_t, jnp.ones(shp, v_t.dtype), jnp.zeros(zshp, v_t.dtype)], axis=-2)


def _nsa_attention(h_nsa, c0, bw, k_cmp, v_cmp_t, vs_t, vw_t, gates_t, B, S):
    T = B * S
    G, R = NSA_GROUPS, NSA_REP
    HD = bw // NSA_HEADS
    assert c0 % (R * HD) == 0
    qc0 = c0 // (R * HD)
    Q = NSA_Q
    NQ = S // Q
    NC = S // CMP_STRIDE
    NS = S // SEL_LEN
    RQ = R * Q
    KT = SEL_KV_TILE
    kv0 = (c0 + bw) // HD
    assert S % KT == 0 and S % Q == 0 and KT % Q == 0 and S >= WINDOW + Q
    assert S + CMP_LEN <= SEL_LEN * 256 and 2 * (KT // SEL_LEN) <= AUX_W - MASK_COL0
    aux_s = _aux_table(jnp.arange(S), True)
    aux_c = _aux_table(jnp.arange(NC) * CMP_STRIDE + (CMP_LEN - 1), False)
    HA = HD + V_AUG
    return pl.pallas_call(
        functools.partial(_nsa_kernel, S=S),
        out_shape=jax.ShapeDtypeStruct((T, bw), MXU_DTYPE),
        grid=(B, G, NQ),
        in_specs=[
            pl.BlockSpec((Q, R * HD), lambda b, g, i: (b * NQ + i, qc0 + g)),
            pl.BlockSpec((1, 1, NC, HD), lambda b, g, i: (b, g, 0, 0)),
            pl.BlockSpec((NC, AUX_W), lambda b, g, i: (0, 0)),
            pl.BlockSpec((1, 1, HD, NC), lambda b, g, i: (b, g, 0, 0)),
            pl.BlockSpec((NS, NC), lambda b, g, i: (0, 0)),
            pl.BlockSpec((S, HD), lambda b, g, i: (b, kv0 + 2 * G + g)),
            pl.BlockSpec((S, AUX_W), lambda b, g, i: (0, 0)),
            pl.BlockSpec((1, 1, HA, S), lambda b, g, i: (b, g, 0, 0)),
            pl.BlockSpec((S, HD), lambda b, g, i: (b, kv0 + 4 * G + g)),
            pl.BlockSpec((1, 1, HA, S), lambda b, g, i: (b, g, 0, 0)),
            pl.BlockSpec((1, 16, Q), lambda b, g, i: (g, 0, b * NQ + i)),
            pl.BlockSpec((1, MASK_COL0, RQ), lambda b, g, i: (g, 0, 0)),
            pl.BlockSpec((Q, R * HD), lambda b, g, i: (b * NQ + i, qc0 + (bw + 6 * G * HD) // (R * HD) + g)),
        ],
        out_specs=pl.BlockSpec((Q, R * HD), lambda b, g, i: (b * NQ + i, g)),
        scratch_shapes=[
            pltpu.VMEM((HD + AUX_W, RQ), MXU_DTYPE),
            pltpu.VMEM((2, NS, Q), f32),
            pltpu.VMEM((1, RQ), f32),
            pltpu.VMEM((HA, RQ), f32),
            pltpu.VMEM((2, KT, RQ), f32),
            pltpu.VMEM((2, KT, RQ), MXU_DTYPE),
            pltpu.VMEM((2, 1, RQ), f32),
            pltpu.SMEM((S // KT + 1,), jnp.int32),
        ],
        compiler_params=_params("parallel", "parallel", "arbitrary"),
        name="nsa_attention",
    )(h_nsa, k_cmp, aux_c, v_cmp_t, _overlap_table(NS, NC), h_nsa, aux_s, _with_ones_row(vs_t), h_nsa,
      _with_ones_row(vw_t), gates_t, _slope_rows(R, G), h_nsa)


def _mem_kernel(q_ref, z_ref, kv_ref, o_ref):
    hw = q_ref.shape[1] // MEM_HEADS
    bw = q_ref.shape[1]
    for h in range(MEM_HEADS):
        cs = slice(h * hw, (h + 1) * hw)
        s = _dot_nt(q_ref[:, cs], kv_ref[:, cs])
        m = jnp.max(s, axis=-1, keepdims=True)
        e = jnp.exp(s - m)
        p = e * (1.0 / jnp.sum(e, axis=-1, keepdims=True))
        o = _dot(p.astype(MXU_DTYPE), kv_ref[:, bw + h * hw:bw + (h + 1) * hw])
        o_ref[:, cs] = (o * _silu(z_ref[:, cs].astype(f32))).astype(o_ref.dtype)


def _mem_attention(h_mem, c0, bw, kv, B, S):
    T = B * S
    M = kv.shape[0] // B
    tq = min(512, S)
    nb = S // tq
    assert c0 % bw == 0
    qc = c0 // bw
    return pl.pallas_call(
        _mem_kernel,
        out_shape=jax.ShapeDtypeStruct((T, bw), MXU_DTYPE),
        grid=(B, nb),
        in_specs=[
            pl.BlockSpec((tq, bw), lambda b, i: (b * nb + i, qc)),
            pl.BlockSpec((tq, bw), lambda b, i: (b * nb + i, qc + 1)),
            pl.BlockSpec((M, 2 * bw), lambda b, i: (b, 0)),
        ],
        out_specs=pl.BlockSpec((tq, bw), lambda b, i: (b * nb + i, 0)),
        compiler_params=_params("parallel", "parallel"),
        name="mem_attention",
    )(h_mem, h_mem, kv)


def _merge_kernel(og_ref, on_ref, om_ref, wg_ref, wn_ref, wm_ref, ag_ref, an_ref, am_ref, o_ref):
    y = ag_ref[...].astype(f32) * _dot(og_ref[...], wg_ref[...])
    y = y + an_ref[...].astype(f32) * _dot(on_ref[...], wn_ref[...])
    y = y + am_ref[...].astype(f32) * _dot(om_ref[...], wm_ref[...])
    o_ref[...] = y.astype(o_ref.dtype)


def _merge(o_gla, o_nsa, o_mem, w_g, w_n, w_m, a, c0):
    T, bw = o_gla.shape
    D = w_g.shape[1]
    tm, tn = min(512, T), min(1024, D)
    nj = D // tn
    assert c0 % tn == 0
    osp = pl.BlockSpec((tm, bw), lambda j, i: (i, 0))
    wsp = pl.BlockSpec((bw, tn), lambda j, i: (0, j))
    asp = lambda c: pl.BlockSpec((tm, tn), lambda j, i: (i, c0 // tn + c * nj + j))
    return pl.pallas_call(
        _merge_kernel,
        out_shape=jax.ShapeDtypeStruct((T, D), MXU_DTYPE),
        grid=(nj, T // tm),
        in_specs=[osp, osp, osp, wsp, wsp, wsp, asp(0), asp(1), asp(2)],
        out_specs=pl.BlockSpec((tm, tn), lambda j, i: (i, j)),
        compiler_params=_params("parallel", "parallel"),
        name="branch_merge",
    )(o_gla, o_nsa, o_mem, w_g, w_n, w_m, a, a, a)


def _out_ln_kernel(m_ref, w_ref, x_ref, g_ref, b_ref, o_ref, *, alpha, tn):
    j = pl.program_id(1)
    c0 = pl.multiple_of(j * tn, tn)
    o_ref[:, pl.ds(c0, tn)] = alpha * x_ref[...] + _dot(m_ref[...], w_ref[...])

    @pl.when(j == pl.num_programs(1) - 1)
    def _():
        def ln_rows(c, _):
            rows = pl.ds(pl.multiple_of(c * LN_ROWS, LN_ROWS), LN_ROWS)
            z = o_ref[rows, :]
            mu = jnp.mean(z, axis=-1, keepdims=True)
            zc = z - mu
            var = jnp.mean(zc * zc, axis=-1, keepdims=True)
            o_ref[rows, :] = zc * lax.rsqrt(var + LN_EPS) * g_ref[...] + b_ref[...]
            return 0

        lax.fori_loop(0, o_ref.shape[0] // LN_ROWS, ln_rows, 0)


def _out_ln(merged, w_out, x2, ln_g, ln_b, alpha):
    T, D = x2.shape
    tm, tn = min(512, T), min(1024, D)
    return pl.pallas_call(
        functools.partial(_out_ln_kernel, alpha=alpha, tn=tn),
        out_shape=jax.ShapeDtypeStruct((T, D), x2.dtype),
        grid=(T // tm, D // tn),
        in_specs=[
            pl.BlockSpec((tm, D), lambda i, j: (i, 0)),
            pl.BlockSpec((D, tn), lambda i, j: (0, j)),
            pl.BlockSpec((tm, tn), lambda i, j: (i, j)),
            pl.BlockSpec((1, D), lambda i, j: (0, 0)),
            pl.BlockSpec((1, D), lambda i, j: (0, 0)),
        ],
        out_specs=pl.BlockSpec((tm, D), lambda i, j: (i, 0)),
        compiler_params=_params("parallel", "arbitrary"),
        name="out_proj_layernorm",
    )(merged, w_out, x2, ln_g.reshape(1, D), ln_b.reshape(1, D))


def _layer(x, mem, w_in, b_merge, gla_w_a2, gla_b_a, gla_norm_g, nsa_pe_k, nsa_pe_v, nsa_wk1, nsa_wk2,
           nsa_wv1, nsa_wv2, w_mem_kv, w_br_gla, w_br_nsa, w_br_mem, w_out, ln_g, ln_b, depth):
    B, S, D = x.shape
    T = B * S
    bw = D // 2
    gk = bw // 2
    G, R = NSA_GROUPS, NSA_REP
    HD = bw // NSA_HEADS
    kvw = G * HD
    cdt = MXU_DTYPE

    o_ga = 2 * gk + 2 * bw
    o_nq = o_ga + GLA_LOWRANK
    o_nbg = o_nq + bw + 6 * kvw + bw
    o_mq = o_nbg + 3 * NSA_HEADS
    o_mrg = o_mq + 2 * bw
    assert w_in.shape[1] == o_mrg + N_BRANCH * D

    x2 = x.reshape(T, D)
    xb = x2.astype(cdt)
    ones = lambda n: jnp.ones((n,), f32)

    w_t = jnp.swapaxes(w_in, 0, 1)
    w_all = _repack(w_t, [(0, o_ga), (o_mq, 2 * bw), (o_nq, 2 * bw + 6 * kvw), (o_mrg, N_BRANCH * D)])
    c_gla, c_mem = 0, o_ga
    c_nsa = c_mem + 2 * bw
    c_mrg = c_nsa + 2 * bw + 6 * kvw
    dk, mhd = gk // GLA_HEADS, bw // MEM_HEADS
    scale = jnp.concatenate([jnp.full((gk,), dk ** -0.5, f32), ones(o_ga - gk),
                             jnp.full((bw,), mhd ** -0.5, f32), ones(bw),
                             jnp.full((bw,), HD ** -0.5 * LOG2E, f32), ones(6 * kvw + bw)])
    h = _project(xb, w_all, scale, cdt, n=c_mrg, nt=True, name="proj_in")
    a = _project(xb, w_all, b_merge, cdt, c0=c_mrg, gate=True, nt=True, name="proj_merge_gates")
    h_small = _project_small(xb, w_t, o_ga, GLA_LOWRANK, o_nbg, 3 * NSA_HEADS)

    wa_pad = jnp.concatenate([gla_w_a2, jnp.zeros((SMALL_W - GLA_LOWRANK, gk), f32)], axis=0)
    o_gla = _gla(h, c_gla, gk, h_small, wa_pad, gla_b_a, gla_norm_g, B, S)

    def grouped(c0):
        return h[:, c0:c0 + kvw].reshape(B, S, G, HD).transpose(0, 2, 1, 3)

    NC = S // CMP_STRIDE
    c_kv = c_nsa + bw
    k_cmp = _compress(grouped(c_kv).reshape(B, G, NC, CMP_STRIDE * HD), nsa_pe_k, nsa_wk1, nsa_wk2)
    v_cmp = _compress(grouped(c_kv + kvw).reshape(B, G, NC, CMP_STRIDE * HD), nsa_pe_v, nsa_wv1, nsa_wv2)
    v_cmp_t = v_cmp.transpose(0, 1, 3, 2)
    vs_t = grouped(c_kv + 3 * kvw).transpose(0, 1, 3, 2)
    vw_t = grouped(c_kv + 5 * kvw).transpose(0, 1, 3, 2)
    gl = h_small[:, GLA_LOWRANK:GLA_LOWRANK + 3 * NSA_HEADS].reshape(T, G, 3 * R).transpose(1, 2, 0)
    gates_t = jnp.concatenate([gl, jnp.zeros((G, 16 - 3 * R, T), f32)], axis=1)
    o_nsa = _nsa_attention(h, c_nsa, bw, k_cmp, v_cmp_t, vs_t, vw_t, gates_t, B, S)

    M = mem.shape[1]
    kv = _project(mem.reshape(B * M, D).astype(cdt), w_mem_kv.astype(cdt), ones(2 * bw), cdt, name="proj_mem_kv")
    o_mem = _mem_attention(h, c_mem, bw, kv, B, S)

    merged = _merge(o_gla, o_nsa, o_mem, w_br_gla.astype(cdt), w_br_nsa.astype(cdt), w_br_mem.astype(cdt), a, 0)
    alpha = (2 * depth) ** 0.25
    return _out_ln(merged, w_out.astype(cdt), x2, ln_g, ln_b, alpha).reshape(B, S, D)


def kernel(x, mem, w_in, b_merge, gla_w_a2, gla_b_a, gla_norm_g, nsa_pe_k, nsa_pe_v, nsa_wk1, nsa_wk2, nsa_wv1, nsa_wv2, w_mem_kv, w_br_gla, w_br_nsa, w_br_mem, w_out, ln_g, ln_b):
    depth = w_in.shape[0]
    for l in range(depth):
        x = _layer(x, mem, w_in[l], b_merge[l], gla_w_a2[l], gla_b_a[l], gla_norm_g[l], nsa_pe_k[l], nsa_pe_v[l],
                   nsa_wk1[l], nsa_wk2[l], nsa_wv1[l], nsa_wv2[l], w_mem_kv[l], w_br_gla[l], w_br_nsa[l],
                   w_br_mem[l], w_out[l], ln_g[l], ln_b[l], depth)
    return x
```

```python
import functools

import jax
import jax.numpy as jnp
from jax import lax
from jax.experimental import pallas as pl
from jax.experimental.pallas import tpu as pltpu

N_BRANCH = 3
GLA_HEADS = 4
GLA_LOWRANK = 16
GLA_TAU = 16.0
GLA_CHUNK = 64
NSA_HEADS = 16
NSA_GROUPS = 4
NSA_REP = NSA_HEADS // NSA_GROUPS
CMP_LEN = 32
CMP_STRIDE = 16
SEL_LEN = 64
SEL_TOPK = 16
WINDOW = 512
Q_BLOCK = 128
FORCE_SCORE = 1e4
MEM_HEADS = 4
LN_EPS = 1e-5
RMS_EPS = 1e-6
NEG_INF = -1e30
LOG2E = 1.4426950408889634

LANES = 128
VMEM_LIMIT_BYTES = 56 * 1024 * 1024
MXU_DTYPE = jnp.bfloat16

SEL_KV_TILE = 512
NSA_Q = 256
GLA_STEP_CHUNKS = 4
LN_ROWS = 64
SMALL_W = LANES
AUX_W = LANES
MASK_COL0 = 16
V_AUG = 16
MASK_BIG = -NEG_INF

f32 = jnp.float32


def _dot(a, b):
    return jnp.dot(a, b, preferred_element_type=f32)


def _dot_nt(a, b):
    return lax.dot_general(a, b, (((1,), (1,)), ((), ())), preferred_element_type=f32)


def _dot_tn(a, b):
    return lax.dot_general(a, b, (((0,), (0,)), ((), ())), preferred_element_type=f32)


def _sigmoid(x):
    return 1.0 / (1.0 + jnp.exp(-x))


def _silu(x):
    return x * _sigmoid(x)


def _log_sigmoid(x):
    return -(jnp.maximum(-x, 0.0) + jnp.log1p(jnp.exp(-jnp.abs(x))))


def _split2(x):
    hi = x.astype(MXU_DTYPE)
    lo = (x - hi.astype(f32)).astype(MXU_DTYPE)
    return hi, lo


def _split3(x):
    hi = x.astype(MXU_DTYPE)
    r1 = x - hi.astype(f32)
    mid = r1.astype(MXU_DTYPE)
    lo = (r1 - mid.astype(f32)).astype(MXU_DTYPE)
    return hi, mid, lo


def _params(*sem):
    return pltpu.CompilerParams(dimension_semantics=sem, vmem_limit_bytes=VMEM_LIMIT_BYTES)


def _proj_kernel(x_ref, w_ref, r_ref, o_ref, *, gate, nt, grouped):
    acc = _dot_nt(x_ref[...], w_ref[...]) if nt else _dot(x_ref[...], w_ref[...])
    res = (_sigmoid(acc + r_ref[...]) if gate else acc * r_ref[...]).astype(o_ref.dtype)
    if grouped:
        for c in range(o_ref.shape[0]):
            o_ref[c] = res[:, c * LANES:(c + 1) * LANES]
    else:
        o_ref[...] = res


def _project(x, w, row, out_dtype, c0=0, n=None, gate=False, nt=False, grouped=False, name="proj"):
    M, K = x.shape
    n = (w.shape[0] if nt else w.shape[1]) - c0 if n is None else n
    bm = min(1024, M)
    bn = min(1024, n)
    assert M % bm == 0 and n % bn == 0 and c0 % bn == 0
    wspec = (pl.BlockSpec((bn, K), lambda j, i: (c0 // bn + j, 0)) if nt
             else pl.BlockSpec((K, bn), lambda j, i: (0, c0 // bn + j)))
    if grouped:
        out_shape = jax.ShapeDtypeStruct((n // LANES, M, LANES), out_dtype)
        ospec = pl.BlockSpec((bn // LANES, bm, LANES), lambda j, i: (j, i, 0))
    else:
        out_shape = jax.ShapeDtypeStruct((M, n), out_dtype)
        ospec = pl.BlockSpec((bm, bn), lambda j, i: (i, j))
    return pl.pallas_call(
        functools.partial(_proj_kernel, gate=gate, nt=nt, grouped=grouped),
        out_shape=out_shape,
        grid=(n // bn, M // bm),
        in_specs=[pl.BlockSpec((bm, K), lambda j, i: (i, 0)), wspec, pl.BlockSpec((1, bn), lambda j, i: (0, j))],
        out_specs=ospec,
        compiler_params=_params("parallel", "parallel"),
        name=name,
    )(x, w, row.reshape(1, n).astype(f32))


def _row_window(rows, width, start16):
    return pl.BlockSpec((pl.Element(rows), pl.Element(width)), lambda *g: (start16(*g) * 16, 0))


def _proj_small_kernel(x_ref, wa_ref, wb_ref, o_ref):
    pad = jnp.zeros((LANES - wa_ref.shape[0] - wb_ref.shape[0], wa_ref.shape[1]), f32)
    w = jnp.concatenate([wa_ref[...], wb_ref[...], pad], axis=0)
    o_ref[...] = _dot_nt(x_ref[...], w.astype(x_ref.dtype))


def _project_small(x, w_t, ra, na, rb, nb):
    M, K = x.shape
    bm = min(1024, M)
    assert ra % 16 == 0 and rb % 16 == 0 and na % 8 == 0 and nb % 8 == 0 and na + nb <= LANES
    return pl.pallas_call(
        _proj_small_kernel,
        out_shape=jax.ShapeDtypeStruct((M, LANES), f32),
        grid=(M // bm,),
        in_specs=[
            pl.BlockSpec((bm, K), lambda i: (i, 0)),
            _row_window(na, K, lambda i: ra // 16),
            _row_window(nb, K, lambda i: rb // 16),
        ],
        out_specs=pl.BlockSpec((bm, LANES), lambda i: (i, 0)),
        compiler_params=_params("parallel"),
        name="proj_small",
    )(x, w_t, w_t)


def _repack_kernel(w_ref, o_ref):
    o_ref[...] = w_ref[...].astype(o_ref.dtype)


def _repack(w_t, segments, bn=1024):
    N, K = w_t.shape
    segs, blk = [], 0
    for src, n in segments:
        assert n % bn == 0 and src % 16 == 0 and src + n <= N
        segs.append((blk, blk + n // bn, src // 16 - blk * (bn // 16)))
        blk += n // bn

    def start16(j):
        s = jnp.int32(0)
        for lo, hi, off in segs:
            s = jnp.where((j >= lo) & (j < hi), j * (bn // 16) + off, s)
        return s

    return pl.pallas_call(
        _repack_kernel,
        out_shape=jax.ShapeDtypeStruct((blk * bn, K), MXU_DTYPE),
        grid=(blk,),
        in_specs=[_row_window(bn, K, start16)],
        out_specs=pl.BlockSpec((bn, K), lambda j: (j, 0)),
        compiler_params=_params("parallel"),
        name="repack_w_in",
    )(w_t)


def _gla_kernel(q_ref, k_ref, v_ref, z_ref, ga_ref, wa_ref, ba_ref, ng_ref, o_ref, st_ref, *, dk, dv):
    C = GLA_CHUNK

    @pl.when(pl.program_id(1) == 0)
    def _():
        st_ref[...] = jnp.zeros_like(st_ref)

    row = lax.broadcasted_iota(jnp.int32, (C, C), 0)
    col = lax.broadcasted_iota(jnp.int32, (C, C), 1)
    tril = row >= col
    ltri = jnp.where(tril, 1.0, 0.0).astype(MXU_DTYPE)
    wa_hi, wa_lo = _split2(wa_ref[...])
    for c in range(GLA_STEP_CHUNKS):
        rows = slice(c * C, (c + 1) * C)
        ga_hi, ga_lo = _split2(ga_ref[rows, :])
        zz = _dot(ga_hi, wa_hi) + _dot(ga_lo, wa_hi) + _dot(ga_hi, wa_lo) + ba_ref[...]
        la = _log_sigmoid(zz) * (1.0 / GLA_TAU)
        la_hi, la_mid, la_lo = _split3(la)
        bcum = _dot(ltri, la_hi) + _dot(ltri, la_mid) + _dot(ltri, la_lo)
        for h in range(GLA_HEADS):
            kc = slice(h * dk, (h + 1) * dk)
            vc = slice(h * dv, (h + 1) * dv)
            b = bcum[:, kc]
            bl = b[C - 1:C, :]
            qh = q_ref[rows, kc].astype(f32)
            kh = k_ref[rows, kc].astype(f32)
            vh = v_ref[rows, vc]
            q_d = (qh * jnp.exp(b)).astype(MXU_DTYPE)
            k_d = (kh * jnp.exp(-b)).astype(MXU_DTYPE)
            k_e = (kh * jnp.exp(bl - b)).astype(MXU_DTYPE)
            att = jnp.where(tril, _dot_nt(q_d, k_d), 0.0)
            st = st_ref[h]
            o = _dot(att.astype(MXU_DTYPE), vh) + _dot_nt(q_d, st.astype(MXU_DTYPE))
            st_ref[h] = st * jnp.exp(bl) + _dot_tn(vh, k_e)
            ms = jnp.mean(o * o, axis=-1, keepdims=True)
            on = o * lax.rsqrt(ms + RMS_EPS) * ng_ref[...]
            zg = z_ref[rows, vc].astype(f32)
            o_ref[rows, vc] = (on * _silu(zg)).astype(o_ref.dtype)


def _gla(h, c0, kw, h_small, wa_pad, b_a, norm_g, B, S):
    T = B * S
    dk, dv = kw // GLA_HEADS, 2 * kw // GLA_HEADS
    cs = GLA_STEP_CHUNKS * GLA_CHUNK
    nb = S // cs
    assert S % cs == 0 and c0 % (2 * kw) == 0
    h_gla = h
    rowmap = lambda col: (lambda b, i: (b * nb + i, col))
    qc, vc = c0 // kw, c0 // (2 * kw)
    return pl.pallas_call(
        functools.partial(_gla_kernel, dk=dk, dv=dv),
        out_shape=jax.ShapeDtypeStruct((T, 2 * kw), MXU_DTYPE),
        grid=(B, nb),
        in_specs=[
            pl.BlockSpec((cs, kw), rowmap(qc)),
            pl.BlockSpec((cs, kw), rowmap(qc + 1)),
            pl.BlockSpec((cs, 2 * kw), rowmap(vc + 1)),
            pl.BlockSpec((cs, 2 * kw), rowmap(vc + 2)),
            pl.BlockSpec((cs, SMALL_W), rowmap(0)),
            pl.BlockSpec((SMALL_W, kw), lambda b, i: (0, 0)),
            pl.BlockSpec((1, kw), lambda b, i: (0, 0)),
            pl.BlockSpec((1, dv), lambda b, i: (0, 0)),
        ],
        out_specs=pl.BlockSpec((cs, 2 * kw), rowmap(0)),
        scratch_shapes=[pltpu.VMEM((GLA_HEADS, dv, dk), f32)],
        compiler_params=_params("parallel", "arbitrary"),
        name="gla",
    )(h_gla, h_gla, h_gla, h_gla, h_small, wa_pad, b_a.reshape(1, kw), norm_g.reshape(1, dv))


def _compress_kernel(x_ref, pe_ref, w1_ref, w2_ref, o_ref, *, transpose_out):
    x = x_ref[0, 0].astype(f32)
    half = x.shape[1]
    xa = (x + pe_ref[0:1, :]).astype(MXU_DTYPE)
    xb = (x + pe_ref[1:2, :]).astype(MXU_DTYPE)
    ya = _dot(xa, w1_ref[0:half, :])
    yb = _dot(xb, w1_ref[half:2 * half, :])
    nc = x.shape[0]
    pre = ya + pltpu.roll(yb, nc - 1, 0)
    out = _dot(_silu(pre).astype(MXU_DTYPE), w2_ref[...])
    o_ref[0, 0] = (out.T if transpose_out else out).astype(o_ref.dtype)


def _compress(xblk, pe, w1, w2, transpose_out):
    G, B, NC, W = xblk.shape
    hd = w2.shape[0]
    oshape = (hd, NC) if transpose_out else (NC, hd)
    return pl.pallas_call(
        functools.partial(_compress_kernel, transpose_out=transpose_out),
        out_shape=jax.ShapeDtypeStruct((B, G) + oshape, MXU_DTYPE),
        grid=(B, G),
        in_specs=[
            pl.BlockSpec((1, 1, NC, W), lambda b, g: (g, b, 0, 0)),
            pl.BlockSpec((2, W), lambda b, g: (0, 0)),
            pl.BlockSpec((2 * W, hd), lambda b, g: (0, 0)),
            pl.BlockSpec((hd, hd), lambda b, g: (0, 0)),
        ],
        out_specs=pl.BlockSpec((1, 1) + oshape, lambda b, g: (b, g, 0, 0)),
        compiler_params=_params("parallel", "parallel"),
        name="nsa_compress",
    )(xblk, pe.reshape(2, W).astype(f32), w1.astype(MXU_DTYPE), w2.astype(MXU_DTYPE))


def _mask_heads(ok, s, R, Q, fill=NEG_INF):
    return jnp.concatenate([jnp.where(ok, s[:, r * Q:(r + 1) * Q], fill) for r in range(R)], axis=1)


def _softmax2_cols(s, ok, R, Q):
    s = _mask_heads(ok, s, R, Q)
    m = jnp.max(s, axis=0, keepdims=True)
    e = _mask_heads(ok, jnp.exp2(s - m), R, Q, 0.0)
    den = jnp.sum(e, axis=0, keepdims=True)
    return e * jnp.where(den > 0.0, 1.0 / den, 0.0)


def _nsa_kernel(q_ref, kc_ref, auxc_ref, vct_ref, ov_ref, ks3_ref, aux_ref, vs_ref, kw3_ref, vw_ref, gt_ref,
                sl_ref, z_ref, o_ref, qa_ref, sel_ref, m_ref, acc_ref, s0_ref, s1_ref, p0_ref, p1_ref, al0_ref,
                al1_ref, vst_ref, vwt_ref, idx_ref, *, S):
    Q, R = NSA_Q, NSA_REP
    s_refs, p_refs, al_refs = (s0_ref, s1_ref), (p0_ref, p1_ref), (al0_ref, al1_ref)
    ks_ref, kw_ref = ks3_ref.at[0], kw3_ref.at[0]
    HD = q_ref.shape[1] // R
    RQ = R * Q
    NC = S // CMP_STRIDE
    NS = S // SEL_LEN
    KT = SEL_KV_TILE
    BPT = KT // SEL_LEN
    NT = S // KT
    MR = HD + MASK_COL0
    qb = pl.program_id(2)
    start = qb * Q

    @pl.when(qb == 0)
    def _():
        tail = jnp.where(lax.broadcasted_iota(jnp.int32, (V_AUG, S), 0) == 0, 1.0, 0.0).astype(vst_ref.dtype)
        vst_ref[HD:, :] = tail
        vwt_ref[HD:, :] = tail

        def fill(c, carry):
            r = pl.ds(pl.multiple_of(c * LANES, LANES), LANES)
            vst_ref[0:HD, r] = vs_ref[0, r, :].T
            vwt_ref[0:HD, r] = vw_ref[0, r, :].T
            return carry

        lax.fori_loop(0, S // LANES, fill, 0)

    q = q_ref[...]
    qa_ref[0:HD, :] = jnp.concatenate(
        [q[c * LANES:(c + 1) * LANES, r * HD:(r + 1) * HD].T for r in range(R) for c in range(Q // LANES)], axis=1)
    qa_ref[HD:MR, :] = sl_ref[0]
    qa_ref[MR:, :] = jnp.zeros((AUX_W - MASK_COL0, RQ), qa_ref.dtype)
    qa = qa_ref[...]

    ok_c = (lax.broadcasted_iota(jnp.int32, (NC, Q), 0) * CMP_STRIDE + (CMP_LEN - 1)
            <= start + lax.broadcasted_iota(jnp.int32, (NC, Q), 1))
    s_c = _dot(jnp.concatenate([kc_ref[0, 0], auxc_ref[...]], axis=1), qa)
    p_c = _softmax2_cols(s_c, ok_c, R, Q)
    o_cmp = _dot(vct_ref[0, 0], p_c.astype(MXU_DTYPE))

    p_sum = p_c[:, 0:Q]
    for r in range(1, R):
        p_sum = p_sum + p_c[:, r * Q:(r + 1) * Q]
    ov = ov_ref[...]
    ps_hi, ps_mid, ps_lo = _split3(p_sum)
    imp = _dot(ov, ps_hi) + _dot(ov, ps_mid) + _dot(ov, ps_lo)

    blk = lax.broadcasted_iota(jnp.int32, (NS, Q), 0)
    tq1 = start + lax.broadcasted_iota(jnp.int32, (NS, Q), 1)
    cur = jnp.right_shift(tq1, SEL_LEN.bit_length() - 1)
    forced = (blk == 0) | (blk == cur) | (blk == cur - 1)
    score = jnp.where(forced, FORCE_SCORE, jnp.where(blk * SEL_LEN <= tq1, imp, -1.0))
    blk_f = blk.astype(f32)
    sel = jnp.zeros((NS, Q), f32)
    for _ in range(min(SEL_TOPK, NS)):
        m = jnp.max(score, axis=0, keepdims=True)
        first = jnp.min(jnp.where(score == m, blk_f, float(NS)), axis=0, keepdims=True)
        hit = blk_f == first
        score = jnp.where(hit, -jnp.inf, score)
        sel = jnp.where(hit, 1.0, sel)
    sel_past = jnp.where(blk * SEL_LEN < start, sel, 0.0)
    sel_ref[0] = sel
    sel_ref[1] = sel_past
    n_act = jnp.int32(0)
    for i in range(NT):
        idx_ref[n_act] = jnp.int32(i)
        n_act = n_act + (jnp.max(sel_past[i * BPT:(i + 1) * BPT, :]) > 0.0).astype(jnp.int32)

    WK = WINDOW + Q
    ws = pl.multiple_of(jnp.maximum(start - WINDOW, 0), Q)
    dist_w = (start - ws) + (lax.broadcasted_iota(jnp.int32, (WK, Q), 1)
                             - lax.broadcasted_iota(jnp.int32, (WK, Q), 0))
    ok_w = (dist_w >= 0) & (dist_w < WINDOW)
    s_w = _dot(jnp.concatenate([kw_ref[pl.ds(ws, WK), :], aux_ref[pl.ds(ws, WK), :]], axis=1), qa)
    s_w = _mask_heads(ok_w, s_w, R, Q)
    e_w = jnp.exp2(s_w - jnp.max(s_w, axis=0, keepdims=True)).astype(MXU_DTYPE)
    acc_w = _dot(vwt_ref[:, pl.ds(ws, WK)], e_w)
    o_win = acc_w[0:HD, :] * (1.0 / acc_w[HD:HD + 1, :])

    def mask_rows(which, b0, valid):
        mrow = jnp.where(valid, (sel_ref[which, pl.ds(b0, BPT), :] - 1.0) * MASK_BIG, -MASK_BIG)
        mrow = jnp.concatenate([mrow] * R, axis=1)
        qa_ref[MR:MR + 2 * BPT, :] = jnp.concatenate([mrow, jnp.zeros_like(mrow)], axis=0).astype(qa_ref.dtype)

    q0 = pl.multiple_of(start, Q)
    mask_rows(0, pl.multiple_of((start // KT) * BPT, BPT), True)
    s_o = _dot(jnp.concatenate([ks_ref[pl.ds(q0, Q), :], aux_ref[pl.ds(q0, Q), :]], axis=1), qa_ref[...])
    ok_o = lax.broadcasted_iota(jnp.int32, (Q, Q), 0) <= lax.broadcasted_iota(jnp.int32, (Q, Q), 1)
    s_o = _mask_heads(ok_o, s_o, R, Q)
    m_o = jnp.max(s_o, axis=0, keepdims=True)
    m_ref[...] = m_o
    acc_ref[...] = _dot(vst_ref[:, pl.ds(q0, Q)], jnp.exp2(s_o - m_o).astype(MXU_DTYPE))

    def tile_of(j):
        return idx_ref[jnp.clip(j, 0, jnp.maximum(n_act - 1, 0))]

    def scores(j, slot):
        i = tile_of(j)
        k0 = pl.multiple_of(i * KT, KT)
        mask_rows(1, pl.multiple_of(i * BPT, BPT), j < n_act)
        s_refs[slot][...] = _dot(jnp.concatenate([ks_ref[pl.ds(k0, KT), :], aux_ref[pl.ds(k0, KT), :]], axis=1),
                                 qa_ref[...])

    def softmax(slot):
        s = s_refs[slot][...]
        m_old = m_ref[...]
        m_new = jnp.maximum(m_old, jnp.max(s, axis=0, keepdims=True))
        p_refs[slot][...] = jnp.exp2(s - m_new).astype(MXU_DTYPE)
        al_refs[slot][...] = jnp.exp2(m_old - m_new)
        m_ref[...] = m_new

    def accumulate(j, slot):
        k0 = pl.multiple_of(tile_of(j) * KT, KT)
        acc_ref[...] = al_refs[slot][...] * acc_ref[...] + _dot(vst_ref[:, pl.ds(k0, KT)], p_refs[slot][...])

    scores(0, 0)
    scores(1, 1)
    softmax(0)

    def pipe(k, c):
        j = 2 * k
        softmax(1)
        accumulate(j, 0)
        scores(j + 2, 0)
        accumulate(j + 1, 1)
        scores(j + 3, 1)
        softmax(0)
        return c

    lax.fori_loop(0, (n_act + 1) // 2, pipe, 0)
    acc_s = acc_ref[...]
    o_sel = acc_s[0:HD, :] * (1.0 / acc_s[HD:HD + 1, :])

    gates = _sigmoid(gt_ref[0])
    for r in range(R):
        cs = slice(r * Q, (r + 1) * Q)
        o_r = (gates[3 * r:3 * r + 1, :] * o_cmp[:, cs] + gates[3 * r + 1:3 * r + 2, :] * o_sel[:, cs]
               + gates[3 * r + 2:3 * r + 3, :] * o_win[:, cs])
        hs = slice(r * HD, (r + 1) * HD)
        o_ref[:, hs] = (o_r.T * _silu(z_ref[:, hs].astype(f32))).astype(o_ref.dtype)


def _aux_table(pos, onehot):
    hi = (pos // SEL_LEN) * SEL_LEN
    lo = pos % SEL_LEN
    col = jnp.arange(AUX_W)[None, :]
    t = jnp.where(col < 3, hi[:, None], jnp.where(col < 6, lo[:, None], 0)).astype(f32)
    if onehot:
        blk = (pos // SEL_LEN) % (SEL_KV_TILE // SEL_LEN)
        t = t + jnp.where(col == MASK_COL0 + blk[:, None], 1.0, 0.0)
    return t.astype(MXU_DTYPE)


def _slope_rows(R, G):
    sl = jnp.exp2(-8.0 * (jnp.arange(NSA_HEADS, dtype=f32) + 1.0) / NSA_HEADS) * LOG2E
    parts = _split3(sl)
    rows = jnp.stack(parts + parts, axis=0).astype(f32)
    rows = jnp.repeat(rows.reshape(6, G, R).transpose(1, 0, 2), NSA_Q, axis=2)
    pad = jnp.zeros((G, MASK_COL0 - 6, R * NSA_Q), f32)
    return jnp.concatenate([rows, pad], axis=1).astype(MXU_DTYPE)


def _overlap_table(NS, NC):
    jj = jnp.arange(NS)[:, None] * SEL_LEN
    nn = jnp.arange(NC)[None, :] * CMP_STRIDE
    return jnp.where((nn < jj + SEL_LEN) & (nn + (CMP_LEN - 1) >= jj), 1.0, 0.0).astype(MXU_DTYPE)


def _nsa_attention(h, cq, cz, bw, kvh, k_cmp, v_cmp_t, gates_t, B, S):
    T = B * S
    G, R = NSA_GROUPS, NSA_REP
    HD = bw // NSA_HEADS
    assert cq % (R * HD) == 0 and cz % (R * HD) == 0
    qc0, zc0 = cq // (R * HD), cz // (R * HD)
    Q = NSA_Q
    NQ = S // Q
    NC = S // CMP_STRIDE
    NS = S // SEL_LEN
    RQ = R * Q
    KT = SEL_KV_TILE
    assert S % KT == 0 and S % Q == 0 and KT % Q == 0 and S >= WINDOW + Q
    assert S + CMP_LEN <= SEL_LEN * 256 and 2 * (KT // SEL_LEN) <= AUX_W - MASK_COL0
    aux_s = _aux_table(jnp.arange(S), True)
    aux_c = _aux_table(jnp.arange(NC) * CMP_STRIDE + (CMP_LEN - 1), False)
    HA = HD + V_AUG
    slab = lambda k: pl.BlockSpec((1, S, HD), lambda b, g, i: (k * G + g, b, 0))
    return pl.pallas_call(
        functools.partial(_nsa_kernel, S=S),
        out_shape=jax.ShapeDtypeStruct((T, bw), MXU_DTYPE),
        grid=(B, G, NQ),
        in_specs=[
            pl.BlockSpec((Q, R * HD), lambda b, g, i: (b * NQ + i, qc0 + g)),
            pl.BlockSpec((1, 1, NC, HD), lambda b, g, i: (b, g, 0, 0)),
            pl.BlockSpec((NC, AUX_W), lambda b, g, i: (0, 0)),
            pl.BlockSpec((1, 1, HD, NC), lambda b, g, i: (b, g, 0, 0)),
            pl.BlockSpec((NS, NC), lambda b, g, i: (0, 0)),
            slab(2),
            pl.BlockSpec((S, AUX_W), lambda b, g, i: (0, 0)),
            slab(3),
            slab(4),
            slab(5),
            pl.BlockSpec((1, 16, Q), lambda b, g, i: (g, 0, b * NQ + i)),
            pl.BlockSpec((1, MASK_COL0, RQ), lambda b, g, i: (g, 0, 0)),
            pl.BlockSpec((Q, R * HD), lambda b, g, i: (b * NQ + i, zc0 + g)),
        ],
        out_specs=pl.BlockSpec((Q, R * HD), lambda b, g, i: (b * NQ + i, g)),
        scratch_shapes=[
            pltpu.VMEM((HD + AUX_W, RQ), MXU_DTYPE),
            pltpu.VMEM((2, NS, Q), f32),
            pltpu.VMEM((1, RQ), f32),
            pltpu.VMEM((HA, RQ), f32),
            pltpu.VMEM((KT, RQ), f32), pltpu.VMEM((KT, RQ), f32),
            pltpu.VMEM((KT, RQ), MXU_DTYPE), pltpu.VMEM((KT, RQ), MXU_DTYPE),
            pltpu.VMEM((1, RQ), f32), pltpu.VMEM((1, RQ), f32),
            pltpu.VMEM((HA, S), MXU_DTYPE), pltpu.VMEM((HA, S), MXU_DTYPE),
            pltpu.SMEM((S // KT + 1,), jnp.int32),
        ],
        compiler_params=_params("parallel", "parallel", "arbitrary"),
        name="nsa_attention",
    )(h, k_cmp, aux_c, v_cmp_t, _overlap_table(NS, NC), kvh, aux_s, kvh, kvh, kvh, gates_t, _slope_rows(R, G), h)


def _mem_kernel(q_ref, z_ref, kv_ref, o_ref):
    hw = q_ref.shape[1] // MEM_HEADS
    bw = q_ref.shape[1]
    for h in range(MEM_HEADS):
        cs = slice(h * hw, (h + 1) * hw)
        s = _dot_nt(q_ref[:, cs], kv_ref[:, cs])
        m = jnp.max(s, axis=-1, keepdims=True)
        e = jnp.exp(s - m)
        p = e * (1.0 / jnp.sum(e, axis=-1, keepdims=True))
        o = _dot(p.astype(MXU_DTYPE), kv_ref[:, bw + h * hw:bw + (h + 1) * hw])
        o_ref[:, cs] = (o * _silu(z_ref[:, cs].astype(f32))).astype(o_ref.dtype)


def _mem_attention(h_mem, c0, bw, kv, B, S):
    T = B * S
    M = kv.shape[0] // B
    tq = min(512, S)
    nb = S // tq
    assert c0 % bw == 0
    qc = c0 // bw
    return pl.pallas_call(
        _mem_kernel,
        out_shape=jax.ShapeDtypeStruct((T, bw), MXU_DTYPE),
        grid=(B, nb),
        in_specs=[
            pl.BlockSpec((tq, bw), lambda b, i: (b * nb + i, qc)),
            pl.BlockSpec((tq, bw), lambda b, i: (b * nb + i, qc + 1)),
            pl.BlockSpec((M, 2 * bw), lambda b, i: (b, 0)),
        ],
        out_specs=pl.BlockSpec((tq, bw), lambda b, i: (b * nb + i, 0)),
        compiler_params=_params("parallel", "parallel"),
        name="mem_attention",
    )(h_mem, h_mem, kv)


def _merge_kernel(og_ref, on_ref, om_ref, wg_ref, wn_ref, wm_ref, ag_ref, an_ref, am_ref, o_ref):
    y = ag_ref[...].astype(f32) * _dot(og_ref[...], wg_ref[...])
    y = y + an_ref[...].astype(f32) * _dot(on_ref[...], wn_ref[...])
    y = y + am_ref[...].astype(f32) * _dot(om_ref[...], wm_ref[...])
    o_ref[...] = y.astype(o_ref.dtype)


def _merge(o_gla, o_nsa, o_mem, w_g, w_n, w_m, a, c0):
    T, bw = o_gla.shape
    D = w_g.shape[1]
    tm, tn = min(512, T), min(1024, D)
    nj = D // tn
    assert c0 % tn == 0
    osp = pl.BlockSpec((tm, bw), lambda j, i: (i, 0))
    wsp = pl.BlockSpec((bw, tn), lambda j, i: (0, j))
    asp = lambda c: pl.BlockSpec((tm, tn), lambda j, i: (i, c0 // tn + c * nj + j))
    return pl.pallas_call(
        _merge_kernel,
        out_shape=jax.ShapeDtypeStruct((T, D), MXU_DTYPE),
        grid=(nj, T // tm),
        in_specs=[osp, osp, osp, wsp, wsp, wsp, asp(0), asp(1), asp(2)],
        out_specs=pl.BlockSpec((tm, tn), lambda j, i: (i, j)),
        compiler_params=_params("parallel", "parallel"),
        name="branch_merge",
    )(o_gla, o_nsa, o_mem, w_g, w_n, w_m, a, a, a)


def _out_ln_kernel(m_ref, w_ref, x_ref, g_ref, b_ref, o_ref, *, alpha, tn):
    j = pl.program_id(1)
    c0 = pl.multiple_of(j * tn, tn)
    o_ref[:, pl.ds(c0, tn)] = alpha * x_ref[...] + _dot(m_ref[...], w_ref[...])

    @pl.when(j == pl.num_programs(1) - 1)
    def _():
        def ln_rows(c, _):
            rows = pl.ds(pl.multiple_of(c * LN_ROWS, LN_ROWS), LN_ROWS)
            z = o_ref[rows, :]
            mu = jnp.mean(z, axis=-1, keepdims=True)
            zc = z - mu
            var = jnp.mean(zc * zc, axis=-1, keepdims=True)
            o_ref[rows, :] = zc * lax.rsqrt(var + LN_EPS) * g_ref[...] + b_ref[...]
            return 0

        lax.fori_loop(0, o_ref.shape[0] // LN_ROWS, ln_rows, 0)


def _out_ln(merged, w_out, x2, ln_g, ln_b, alpha):
    T, D = x2.shape
    tm, tn = min(512, T), min(1024, D)
    return pl.pallas_call(
        functools.partial(_out_ln_kernel, alpha=alpha, tn=tn),
        out_shape=jax.ShapeDtypeStruct((T, D), x2.dtype),
        grid=(T // tm, D // tn),
        in_specs=[
            pl.BlockSpec((tm, D), lambda i, j: (i, 0)),
            pl.BlockSpec((D, tn), lambda i, j: (0, j)),
            pl.BlockSpec((tm, tn), lambda i, j: (i, j)),
            pl.BlockSpec((1, D), lambda i, j: (0, 0)),
            pl.BlockSpec((1, D), lambda i, j: (0, 0)),
        ],
        out_specs=pl.BlockSpec((tm, D), lambda i, j: (i, 0)),
        compiler_params=_params("parallel", "arbitrary"),
        name="out_proj_layernorm",
    )(merged, w_out, x2, ln_g.reshape(1, D), ln_b.reshape(1, D))


def _layer(x, mem, w_in, b_merge, gla_w_a2, gla_b_a, gla_norm_g, nsa_pe_k, nsa_pe_v, nsa_wk1, nsa_wk2,
           nsa_wv1, nsa_wv2, w_mem_kv, w_br_gla, w_br_nsa, w_br_mem, w_out, ln_g, ln_b, depth):
    B, S, D = x.shape
    T = B * S
    bw = D // 2
    gk = bw // 2
    G, R = NSA_GROUPS, NSA_REP
    HD = bw // NSA_HEADS
    kvw = G * HD
    cdt = MXU_DTYPE

    o_ga = 2 * gk + 2 * bw
    o_nq = o_ga + GLA_LOWRANK
    o_nbg = o_nq + bw + 6 * kvw + bw
    o_mq = o_nbg + 3 * NSA_HEADS
    o_mrg = o_mq + 2 * bw
    assert w_in.shape[1] == o_mrg + N_BRANCH * D

    x2 = x.reshape(T, D)
    xb = x2.astype(cdt)
    ones = lambda n: jnp.ones((n,), f32)

    w_t = jnp.swapaxes(w_in, 0, 1)
    o_nkv, o_nz = o_nq + bw, o_nq + bw + 6 * kvw
    w_all = _repack(w_t, [(0, o_ga), (o_mq, 2 * bw), (o_nq, bw), (o_nz, bw), (o_nkv, 6 * kvw), (o_mrg, N_BRANCH * D)])
    c_gla, c_mem = 0, o_ga
    c_nq = c_mem + 2 * bw
    c_nz = c_nq + bw
    c_nkv = c_nz + bw
    c_mrg = c_nkv + 6 * kvw
    dk, mhd = gk // GLA_HEADS, bw // MEM_HEADS
    scale = jnp.concatenate([jnp.full((gk,), dk ** -0.5, f32), ones(o_ga - gk),
                             jnp.full((bw,), mhd ** -0.5, f32), ones(bw),
                             jnp.full((bw,), HD ** -0.5 * LOG2E, f32), ones(bw)])
    h = _project(xb, w_all, scale, cdt, n=c_nkv, nt=True, name="proj_in")
    kvh = _project(xb, w_all, ones(6 * kvw), cdt, c0=c_nkv, n=6 * kvw, nt=True, grouped=True, name="proj_nsa_kv")
    a = _project(xb, w_all, b_merge, cdt, c0=c_mrg, gate=True, nt=True, name="proj_merge_gates")
    h_small = _project_small(xb, w_t, o_ga, GLA_LOWRANK, o_nbg, 3 * NSA_HEADS)

    wa_pad = jnp.concatenate([gla_w_a2, jnp.zeros((SMALL_W - GLA_LOWRANK, gk), f32)], axis=0)
    o_gla = _gla(h, c_gla, gk, h_small, wa_pad, gla_b_a, gla_norm_g, B, S)

    NC = S // CMP_STRIDE
    blocks = lambda k: kvh[k * G:(k + 1) * G].reshape(G, B, NC, CMP_STRIDE * HD)
    k_cmp = _compress(blocks(0), nsa_pe_k, nsa_wk1, nsa_wk2, False)
    v_cmp_t = _compress(blocks(1), nsa_pe_v, nsa_wv1, nsa_wv2, True)
    gl = h_small[:, GLA_LOWRANK:GLA_LOWRANK + 3 * NSA_HEADS].reshape(T, G, 3 * R).transpose(1, 2, 0)
    gates_t = jnp.concatenate([gl, jnp.zeros((G, 16 - 3 * R, T), f32)], axis=1)
    o_nsa = _nsa_attention(h, c_nq, c_nz, bw, kvh, k_cmp, v_cmp_t, gates_t, B, S)

    M = mem.shape[1]
    kv = _project(mem.reshape(B * M, D).astype(cdt), w_mem_kv.astype(cdt), ones(2 * bw), cdt, name="proj_mem_kv")
    o_mem = _mem_attention(h, c_mem, bw, kv, B, S)

    merged = _merge(o_gla, o_nsa, o_mem, w_br_gla.astype(cdt), w_br_nsa.astype(cdt), w_br_mem.astype(cdt), a, 0)
    alpha = (2 * depth) ** 0.25
    return _out_ln(merged, w_out.astype(cdt), x2, ln_g, ln_b, alpha).reshape(B, S, D)


def kernel(x, mem, w_in, b_merge, gla_w_a2, gla_b_a, gla_norm_g, nsa_pe_k, nsa_pe_v, nsa_wk1, nsa_wk2, nsa_wv1, nsa_wv2, w_mem_kv, w_br_gla, w_br_nsa, w_br_mem, w_out, ln_g, ln_b):
    depth = w_in.shape[0]
    for l in range(depth):
        x = _layer(x, mem, w_in[l], b_merge[l], gla_w_a2[l], gla_b_a[l], gla_norm_g[l], nsa_pe_k[l], nsa_pe_v[l],
                   nsa_wk1[l], nsa_wk2[l], nsa_wv1[l], nsa_wv2[l], w_mem_kv[l], w_br_gla[l], w_br_nsa[l],
                   w_br_mem[l], w_out[l], ln_g[l], ln_b[l], depth)
    return x
```

```python
import functools

import jax
import jax.numpy as jnp
from jax import lax
from jax.experimental import pallas as pl
from jax.experimental.pallas import tpu as pltpu

N_BRANCH = 3
GLA_HEADS = 4
GLA_LOWRANK = 16
GLA_TAU = 16.0
GLA_CHUNK = 64
NSA_HEADS = 16
NSA_GROUPS = 4
NSA_REP = NSA_HEADS // NSA_GROUPS
CMP_LEN = 32
CMP_STRIDE = 16
SEL_LEN = 64
SEL_TOPK = 16
WINDOW = 512
Q_BLOCK = 128
FORCE_SCORE = 1e4
MEM_HEADS = 4
LN_EPS = 1e-5
RMS_EPS = 1e-6
NEG_INF = -1e30
LOG2E = 1.4426950408889634

LANES = 128
VMEM_LIMIT_BYTES = 56 * 1024 * 1024
MXU_DTYPE = jnp.bfloat16

SEL_KV_TILE = 512
NSA_Q = 256
GLA_STEP_CHUNKS = 4
LN_ROWS = 64
OUT_ROWS = 256
SMALL_W = LANES
AUX_W = LANES
MASK_COL0 = 16
V_AUG = 16
MASK_BIG = -NEG_INF

f32 = jnp.float32


def _dot(a, b):
    return jnp.dot(a, b, preferred_element_type=f32)


def _dot_nt(a, b):
    return lax.dot_general(a, b, (((1,), (1,)), ((), ())), preferred_element_type=f32)


def _dot_tn(a, b):
    return lax.dot_general(a, b, (((0,), (0,)), ((), ())), preferred_element_type=f32)


def _sigmoid(x):
    return 1.0 / (1.0 + jnp.exp(-x))


def _silu(x):
    return x * _sigmoid(x)


def _log_sigmoid(x):
    return -(jnp.maximum(-x, 0.0) + jnp.log1p(jnp.exp(-jnp.abs(x))))


def _split2(x):
    hi = x.astype(MXU_DTYPE)
    lo = (x - hi.astype(f32)).astype(MXU_DTYPE)
    return hi, lo


def _split3(x):
    hi = x.astype(MXU_DTYPE)
    r1 = x - hi.astype(f32)
    mid = r1.astype(MXU_DTYPE)
    lo = (r1 - mid.astype(f32)).astype(MXU_DTYPE)
    return hi, mid, lo


def _params(*sem):
    return pltpu.CompilerParams(dimension_semantics=sem, vmem_limit_bytes=VMEM_LIMIT_BYTES)


def _proj_kernel(x_ref, w_ref, r_ref, o_ref, *, gate, nt, grouped):
    acc = _dot_nt(x_ref[...], w_ref[...]) if nt else _dot(x_ref[...], w_ref[...])
    res = (_sigmoid(acc + r_ref[...]) if gate else acc * r_ref[...]).astype(o_ref.dtype)
    if grouped:
        for c in range(o_ref.shape[0]):
            o_ref[c] = res[:, c * LANES:(c + 1) * LANES]
    else:
        o_ref[...] = res


def _project(x, w, row, out_dtype, c0=0, n=None, gate=False, nt=False, grouped=False, name="proj"):
    M, K = x.shape
    n = (w.shape[0] if nt else w.shape[1]) - c0 if n is None else n
    bm = min(1024, M)
    bn = min(1024, n)
    assert M % bm == 0 and n % bn == 0 and c0 % bn == 0
    wspec = (pl.BlockSpec((bn, K), lambda j, i: (c0 // bn + j, 0)) if nt
             else pl.BlockSpec((K, bn), lambda j, i: (0, c0 // bn + j)))
    if grouped:
        out_shape = jax.ShapeDtypeStruct((n // LANES, M, LANES), out_dtype)
        ospec = pl.BlockSpec((bn // LANES, bm, LANES), lambda j, i: (j, i, 0))
    else:
        out_shape = jax.ShapeDtypeStruct((M, n), out_dtype)
        ospec = pl.BlockSpec((bm, bn), lambda j, i: (i, j))
    return pl.pallas_call(
        functools.partial(_proj_kernel, gate=gate, nt=nt, grouped=grouped),
        out_shape=out_shape,
        grid=(n // bn, M // bm),
        in_specs=[pl.BlockSpec((bm, K), lambda j, i: (i, 0)), wspec, pl.BlockSpec((1, bn), lambda j, i: (0, j))],
        out_specs=ospec,
        compiler_params=_params("parallel", "parallel"),
        name=name,
    )(x, w, row.reshape(1, n).astype(f32))


def _row_window(rows, width, start16):
    return pl.BlockSpec((pl.Element(rows), pl.Element(width)), lambda *g: (start16(*g) * 16, 0))


def _proj_small_kernel(x_ref, wa_ref, wb_ref, o_ref):
    pad = jnp.zeros((LANES - wa_ref.shape[0] - wb_ref.shape[0], wa_ref.shape[1]), f32)
    w = jnp.concatenate([wa_ref[...], wb_ref[...], pad], axis=0)
    o_ref[...] = _dot_nt(x_ref[...], w.astype(x_ref.dtype))


def _project_small(x, w_t, ra, na, rb, nb):
    M, K = x.shape
    bm = min(1024, M)
    assert ra % 16 == 0 and rb % 16 == 0 and na % 8 == 0 and nb % 8 == 0 and na + nb <= LANES
    return pl.pallas_call(
        _proj_small_kernel,
        out_shape=jax.ShapeDtypeStruct((M, LANES), f32),
        grid=(M // bm,),
        in_specs=[
            pl.BlockSpec((bm, K), lambda i: (i, 0)),
            _row_window(na, K, lambda i: ra // 16),
            _row_window(nb, K, lambda i: rb // 16),
        ],
        out_specs=pl.BlockSpec((bm, LANES), lambda i: (i, 0)),
        compiler_params=_params("parallel"),
        name="proj_small",
    )(x, w_t, w_t)


def _repack_kernel(w_ref, o_ref):
    o_ref[...] = w_ref[...].astype(o_ref.dtype)


def _repack(w_t, segments, bn=1024):
    N, K = w_t.shape
    segs, blk = [], 0
    for src, n in segments:
        assert n % bn == 0 and src % 16 == 0 and src + n <= N
        segs.append((blk, blk + n // bn, src // 16 - blk * (bn // 16)))
        blk += n // bn

    def start16(j):
        s = jnp.int32(0)
        for lo, hi, off in segs:
            s = jnp.where((j >= lo) & (j < hi), j * (bn // 16) + off, s)
        return s

    return pl.pallas_call(
        _repack_kernel,
        out_shape=jax.ShapeDtypeStruct((blk * bn, K), MXU_DTYPE),
        grid=(blk,),
        in_specs=[_row_window(bn, K, start16)],
        out_specs=pl.BlockSpec((bn, K), lambda j: (j, 0)),
        compiler_params=_params("parallel"),
        name="repack_w_in",
    )(w_t)


def _gla_kernel(q_ref, k_ref, v_ref, z_ref, ga_ref, wa_ref, ba_ref, ng_ref, o_ref, st_ref, *, dk, dv):
    C = GLA_CHUNK

    @pl.when(pl.program_id(1) == 0)
    def _():
        st_ref[...] = jnp.zeros_like(st_ref)

    row = lax.broadcasted_iota(jnp.int32, (C, C), 0)
    col = lax.broadcasted_iota(jnp.int32, (C, C), 1)
    tril = row >= col
    ltri = jnp.where(tril, 1.0, 0.0).astype(MXU_DTYPE)
    wa_hi, wa_lo = _split2(wa_ref[...])
    for c in range(GLA_STEP_CHUNKS):
        rows = slice(c * C, (c + 1) * C)
        ga_hi, ga_lo = _split2(ga_ref[rows, :])
        zz = _dot(ga_hi, wa_hi) + _dot(ga_lo, wa_hi) + _dot(ga_hi, wa_lo) + ba_ref[...]
        la = _log_sigmoid(zz) * (1.0 / GLA_TAU)
        la_hi, la_mid, la_lo = _split3(la)
        bcum = _dot(ltri, la_hi) + _dot(ltri, la_mid) + _dot(ltri, la_lo)
        for h in range(GLA_HEADS):
            kc = slice(h * dk, (h + 1) * dk)
            vc = slice(h * dv, (h + 1) * dv)
            b = bcum[:, kc]
            bl = b[C - 1:C, :]
            qh = q_ref[rows, kc].astype(f32)
            kh = k_ref[rows, kc].astype(f32)
            vh = v_ref[rows, vc]
            q_d = (qh * jnp.exp(b)).astype(MXU_DTYPE)
            k_d = (kh * jnp.exp(-b)).astype(MXU_DTYPE)
            k_e = (kh * jnp.exp(bl - b)).astype(MXU_DTYPE)
            att = jnp.where(tril, _dot_nt(q_d, k_d), 0.0)
            st = st_ref[h]
            o = _dot(att.astype(MXU_DTYPE), vh) + _dot_nt(q_d, st.astype(MXU_DTYPE))
            st_ref[h] = st * jnp.exp(bl) + _dot_tn(vh, k_e)
            ms = jnp.mean(o * o, axis=-1, keepdims=True)
            on = o * lax.rsqrt(ms + RMS_EPS) * ng_ref[...]
            zg = z_ref[rows, vc].astype(f32)
            o_ref[rows, vc] = (on * _silu(zg)).astype(o_ref.dtype)


def _gla(h, c0, kw, h_small, wa_pad, b_a, norm_g, B, S):
    T = B * S
    dk, dv = kw // GLA_HEADS, 2 * kw // GLA_HEADS
    cs = GLA_STEP_CHUNKS * GLA_CHUNK
    nb = S // cs
    assert S % cs == 0 and c0 % (2 * kw) == 0
    h_gla = h
    rowmap = lambda col: (lambda b, i: (b * nb + i, col))
    qc, vc = c0 // kw, c0 // (2 * kw)
    return pl.pallas_call(
        functools.partial(_gla_kernel, dk=dk, dv=dv),
        out_shape=jax.ShapeDtypeStruct((T, 2 * kw), MXU_DTYPE),
        grid=(B, nb),
        in_specs=[
            pl.BlockSpec((cs, kw), rowmap(qc)),
            pl.BlockSpec((cs, kw), rowmap(qc + 1)),
            pl.BlockSpec((cs, 2 * kw), rowmap(vc + 1)),
            pl.BlockSpec((cs, 2 * kw), rowmap(vc + 2)),
            pl.BlockSpec((cs, SMALL_W), rowmap(0)),
            pl.BlockSpec((SMALL_W, kw), lambda b, i: (0, 0)),
            pl.BlockSpec((1, kw), lambda b, i: (0, 0)),
            pl.BlockSpec((1, dv), lambda b, i: (0, 0)),
        ],
        out_specs=pl.BlockSpec((cs, 2 * kw), rowmap(0)),
        scratch_shapes=[pltpu.VMEM((GLA_HEADS, dv, dk), f32)],
        compiler_params=_params("parallel", "arbitrary"),
        name="gla",
    )(h_gla, h_gla, h_gla, h_gla, h_small, wa_pad, b_a.reshape(1, kw), norm_g.reshape(1, dv))


def _compress_kernel(x_ref, pe_ref, w1_ref, w2_ref, o_ref, *, transpose_out):
    x = x_ref[0, 0].astype(f32)
    half = x.shape[1]
    xa = (x + pe_ref[0:1, :]).astype(MXU_DTYPE)
    xb = (x + pe_ref[1:2, :]).astype(MXU_DTYPE)
    ya = _dot(xa, w1_ref[0:half, :])
    yb = _dot(xb, w1_ref[half:2 * half, :])
    nc = x.shape[0]
    pre = ya + pltpu.roll(yb, nc - 1, 0)
    out = _dot(_silu(pre).astype(MXU_DTYPE), w2_ref[...])
    o_ref[0, 0] = (out.T if transpose_out else out).astype(o_ref.dtype)


def _compress(xblk, pe, w1, w2, transpose_out):
    G, B, NC, W = xblk.shape
    hd = w2.shape[0]
    oshape = (hd, NC) if transpose_out else (NC, hd)
    return pl.pallas_call(
        functools.partial(_compress_kernel, transpose_out=transpose_out),
        out_shape=jax.ShapeDtypeStruct((B, G) + oshape, MXU_DTYPE),
        grid=(B, G),
        in_specs=[
            pl.BlockSpec((1, 1, NC, W), lambda b, g: (g, b, 0, 0)),
            pl.BlockSpec((2, W), lambda b, g: (0, 0)),
            pl.BlockSpec((2 * W, hd), lambda b, g: (0, 0)),
            pl.BlockSpec((hd, hd), lambda b, g: (0, 0)),
        ],
        out_specs=pl.BlockSpec((1, 1) + oshape, lambda b, g: (b, g, 0, 0)),
        compiler_params=_params("parallel", "parallel"),
        name="nsa_compress",
    )(xblk, pe.reshape(2, W).astype(f32), w1.astype(MXU_DTYPE), w2.astype(MXU_DTYPE))


def _mask_heads(ok, s, R, Q, fill=NEG_INF):
    return jnp.concatenate([jnp.where(ok, s[:, r * Q:(r + 1) * Q], fill) for r in range(R)], axis=1)


def _softmax2_cols(s, ok, R, Q):
    s = _mask_heads(ok, s, R, Q)
    m = jnp.max(s, axis=0, keepdims=True)
    e = _mask_heads(ok, jnp.exp2(s - m), R, Q, 0.0)
    den = jnp.sum(e, axis=0, keepdims=True)
    return e * jnp.where(den > 0.0, 1.0 / den, 0.0)


def _nsa_kernel(q_ref, kc_ref, auxc_ref, vct_ref, ov_ref, ks3_ref, aux_ref, vs_ref, kw3_ref, vw_ref, gt_ref,
                sl_ref, z_ref, o_ref, qa_ref, sel_ref, m_ref, acc_ref, s0_ref, s1_ref, p0_ref, p1_ref, al0_ref,
                al1_ref, vst_ref, vwt_ref, idx_ref, *, S):
    Q, R = NSA_Q, NSA_REP
    s_refs, p_refs, al_refs = (s0_ref, s1_ref), (p0_ref, p1_ref), (al0_ref, al1_ref)
    ks_ref, kw_ref = ks3_ref.at[0], kw3_ref.at[0]
    HD = q_ref.shape[1] // R
    RQ = R * Q
    NC = S // CMP_STRIDE
    NS = S // SEL_LEN
    KT = SEL_KV_TILE
    BPT = KT // SEL_LEN
    NT = S // KT
    MR = HD + MASK_COL0
    qb = pl.program_id(2)
    start = qb * Q

    @pl.when(qb == 0)
    def _():
        tail = jnp.where(lax.broadcasted_iota(jnp.int32, (V_AUG, S), 0) == 0, 1.0, 0.0).astype(vst_ref.dtype)
        vst_ref[HD:, :] = tail
        vwt_ref[HD:, :] = tail

        def fill(c, carry):
            r = pl.ds(pl.multiple_of(c * LANES, LANES), LANES)
            vst_ref[0:HD, r] = vs_ref[0, r, :].T
            vwt_ref[0:HD, r] = vw_ref[0, r, :].T
            return carry

        lax.fori_loop(0, S // LANES, fill, 0)

    q = q_ref[...]
    qa_ref[0:HD, :] = jnp.concatenate(
        [q[c * LANES:(c + 1) * LANES, r * HD:(r + 1) * HD].T for r in range(R) for c in range(Q // LANES)], axis=1)
    qa_ref[HD:MR, :] = sl_ref[0]
    qa_ref[MR:, :] = jnp.zeros((AUX_W - MASK_COL0, RQ), qa_ref.dtype)
    qa = qa_ref[...]

    ok_c = (lax.broadcasted_iota(jnp.int32, (NC, Q), 0) * CMP_STRIDE + (CMP_LEN - 1)
            <= start + lax.broadcasted_iota(jnp.int32, (NC, Q), 1))
    s_c = _dot(jnp.concatenate([kc_ref[0, 0], auxc_ref[...]], axis=1), qa)
    p_c = _softmax2_cols(s_c, ok_c, R, Q)
    o_cmp = _dot(vct_ref[0, 0], p_c.astype(MXU_DTYPE))

    p_sum = p_c[:, 0:Q]
    for r in range(1, R):
        p_sum = p_sum + p_c[:, r * Q:(r + 1) * Q]
    ov = ov_ref[...]
    ps_hi, ps_mid, ps_lo = _split3(p_sum)
    imp = _dot(ov, ps_hi) + _dot(ov, ps_mid) + _dot(ov, ps_lo)

    blk = lax.broadcasted_iota(jnp.int32, (NS, Q), 0)
    tq1 = start + lax.broadcasted_iota(jnp.int32, (NS, Q), 1)
    cur = jnp.right_shift(tq1, SEL_LEN.bit_length() - 1)
    forced = (blk == 0) | (blk == cur) | (blk == cur - 1)
    score = jnp.where(forced, FORCE_SCORE, jnp.where(blk * SEL_LEN <= tq1, imp, -1.0))
    blk_f = blk.astype(f32)
    sel = jnp.zeros((NS, Q), f32)
    for _ in range(min(SEL_TOPK, NS)):
        m = jnp.max(score, axis=0, keepdims=True)
        first = jnp.min(jnp.where(score == m, blk_f, float(NS)), axis=0, keepdims=True)
        hit = blk_f == first
        score = jnp.where(hit, -jnp.inf, score)
        sel = jnp.where(hit, 1.0, sel)
    sel_past = jnp.where(blk * SEL_LEN < start, sel, 0.0)
    sel_ref[0] = sel
    sel_ref[1] = sel_past
    n_act = jnp.int32(0)
    for i in range(NT):
        idx_ref[n_act] = jnp.int32(i)
        n_act = n_act + (jnp.max(sel_past[i * BPT:(i + 1) * BPT, :]) > 0.0).astype(jnp.int32)

    WK = WINDOW + Q
    ws = pl.multiple_of(jnp.maximum(start - WINDOW, 0), Q)
    dist_w = (start - ws) + (lax.broadcasted_iota(jnp.int32, (WK, Q), 1)
                             - lax.broadcasted_iota(jnp.int32, (WK, Q), 0))
    ok_w = (dist_w >= 0) & (dist_w < WINDOW)
    s_w = _dot(jnp.concatenate([kw_ref[pl.ds(ws, WK), :], aux_ref[pl.ds(ws, WK), :]], axis=1), qa)
    s_w = _mask_heads(ok_w, s_w, R, Q)
    e_w = jnp.exp2(s_w - jnp.max(s_w, axis=0, keepdims=True)).astype(MXU_DTYPE)
    acc_w = _dot(vwt_ref[:, pl.ds(ws, WK)], e_w)
    o_win = acc_w[0:HD, :] * (1.0 / acc_w[HD:HD + 1, :])

    def mask_rows(which, b0, valid):
        mrow = jnp.where(valid, (sel_ref[which, pl.ds(b0, BPT), :] - 1.0) * MASK_BIG, -MASK_BIG)
        mrow = jnp.concatenate([mrow] * R, axis=1)
        qa_ref[MR:MR + 2 * BPT, :] = jnp.concatenate([mrow, jnp.zeros_like(mrow)], axis=0).astype(qa_ref.dtype)

    q0 = pl.multiple_of(start, Q)
    mask_rows(0, pl.multiple_of((start // KT) * BPT, BPT), True)
    s_o = _dot(jnp.concatenate([ks_ref[pl.ds(q0, Q), :], aux_ref[pl.ds(q0, Q), :]], axis=1), qa_ref[...])
    ok_o = lax.broadcasted_iota(jnp.int32, (Q, Q), 0) <= lax.broadcasted_iota(jnp.int32, (Q, Q), 1)
    s_o = _mask_heads(ok_o, s_o, R, Q)
    m_o = jnp.max(s_o, axis=0, keepdims=True)
    m_ref[...] = m_o
    acc_ref[...] = _dot(vst_ref[:, pl.ds(q0, Q)], jnp.exp2(s_o - m_o).astype(MXU_DTYPE))

    def tile_of(j):
        return idx_ref[jnp.clip(j, 0, jnp.maximum(n_act - 1, 0))]

    def scores(j, slot):
        i = tile_of(j)
        k0 = pl.multiple_of(i * KT, KT)
        mask_rows(1, pl.multiple_of(i * BPT, BPT), j < n_act)
        s_refs[slot][...] = _dot(jnp.concatenate([ks_ref[pl.ds(k0, KT), :], aux_ref[pl.ds(k0, KT), :]], axis=1),
                                 qa_ref[...])

    def softmax(slot):
        s = s_refs[slot][...]
        m_old = m_ref[...]
        m_new = jnp.maximum(m_old, jnp.max(s, axis=0, keepdims=True))
        p_refs[slot][...] = jnp.exp2(s - m_new).astype(MXU_DTYPE)
        al_refs[slot][...] = jnp.exp2(m_old - m_new)
        m_ref[...] = m_new

    def accumulate(j, slot):
        k0 = pl.multiple_of(tile_of(j) * KT, KT)
        acc_ref[...] = al_refs[slot][...] * acc_ref[...] + _dot(vst_ref[:, pl.ds(k0, KT)], p_refs[slot][...])

    scores(0, 0)
    scores(1, 1)
    softmax(0)

    def pipe(k, c):
        j = 2 * k
        softmax(1)
        accumulate(j, 0)
        scores(j + 2, 0)
        accumulate(j + 1, 1)
        scores(j + 3, 1)
        softmax(0)
        return c

    lax.fori_loop(0, (n_act + 1) // 2, pipe, 0)
    acc_s = acc_ref[...]
    o_sel = acc_s[0:HD, :] * (1.0 / acc_s[HD:HD + 1, :])

    gates = _sigmoid(gt_ref[0])
    for r in range(R):
        cs = slice(r * Q, (r + 1) * Q)
        o_r = (gates[3 * r:3 * r + 1, :] * o_cmp[:, cs] + gates[3 * r + 1:3 * r + 2, :] * o_sel[:, cs]
               + gates[3 * r + 2:3 * r + 3, :] * o_win[:, cs])
        hs = slice(r * HD, (r + 1) * HD)
        o_ref[:, hs] = (o_r.T * _silu(z_ref[:, hs].astype(f32))).astype(o_ref.dtype)


def _aux_table(pos, onehot):
    hi = (pos // SEL_LEN) * SEL_LEN
    lo = pos % SEL_LEN
    col = jnp.arange(AUX_W)[None, :]
    t = jnp.where(col < 3, hi[:, None], jnp.where(col < 6, lo[:, None], 0)).astype(f32)
    if onehot:
        blk = (pos // SEL_LEN) % (SEL_KV_TILE // SEL_LEN)
        t = t + jnp.where(col == MASK_COL0 + blk[:, None], 1.0, 0.0)
    return t.astype(MXU_DTYPE)


def _slope_rows(R, G):
    sl = jnp.exp2(-8.0 * (jnp.arange(NSA_HEADS, dtype=f32) + 1.0) / NSA_HEADS) * LOG2E
    parts = _split3(sl)
    rows = jnp.stack(parts + parts, axis=0).astype(f32)
    rows = jnp.repeat(rows.reshape(6, G, R).transpose(1, 0, 2), NSA_Q, axis=2)
    pad = jnp.zeros((G, MASK_COL0 - 6, R * NSA_Q), f32)
    return jnp.concatenate([rows, pad], axis=1).astype(MXU_DTYPE)


def _overlap_table(NS, NC):
    jj = jnp.arange(NS)[:, None] * SEL_LEN
    nn = jnp.arange(NC)[None, :] * CMP_STRIDE
    return jnp.where((nn < jj + SEL_LEN) & (nn + (CMP_LEN - 1) >= jj), 1.0, 0.0).astype(MXU_DTYPE)


def _nsa_attention(h, cq, cz, bw, kvh, k_cmp, v_cmp_t, gates_t, B, S):
    T = B * S
    G, R = NSA_GROUPS, NSA_REP
    HD = bw // NSA_HEADS
    assert cq % (R * HD) == 0 and cz % (R * HD) == 0
    qc0, zc0 = cq // (R * HD), cz // (R * HD)
    Q = NSA_Q
    NQ = S // Q
    NC = S // CMP_STRIDE
    NS = S // SEL_LEN
    RQ = R * Q
    KT = SEL_KV_TILE
    assert S % KT == 0 and S % Q == 0 and KT % Q == 0 and S >= WINDOW + Q
    assert S + CMP_LEN <= SEL_LEN * 256 and 2 * (KT // SEL_LEN) <= AUX_W - MASK_COL0
    aux_s = _aux_table(jnp.arange(S), True)
    aux_c = _aux_table(jnp.arange(NC) * CMP_STRIDE + (CMP_LEN - 1), False)
    HA = HD + V_AUG
    slab = lambda k: pl.BlockSpec((1, S, HD), lambda b, g, i: (k * G + g, b, 0))
    return pl.pallas_call(
        functools.partial(_nsa_kernel, S=S),
        out_shape=jax.ShapeDtypeStruct((T, bw), MXU_DTYPE),
        grid=(B, G, NQ),
        in_specs=[
            pl.BlockSpec((Q, R * HD), lambda b, g, i: (b * NQ + i, qc0 + g)),
            pl.BlockSpec((1, 1, NC, HD), lambda b, g, i: (b, g, 0, 0)),
            pl.BlockSpec((NC, AUX_W), lambda b, g, i: (0, 0)),
            pl.BlockSpec((1, 1, HD, NC), lambda b, g, i: (b, g, 0, 0)),
            pl.BlockSpec((NS, NC), lambda b, g, i: (0, 0)),
            slab(2),
            pl.BlockSpec((S, AUX_W), lambda b, g, i: (0, 0)),
            slab(3),
            slab(4),
            slab(5),
            pl.BlockSpec((1, 16, Q), lambda b, g, i: (g, 0, b * NQ + i)),
            pl.BlockSpec((1, MASK_COL0, RQ), lambda b, g, i: (g, 0, 0)),
            pl.BlockSpec((Q, R * HD), lambda b, g, i: (b * NQ + i, zc0 + g)),
        ],
        out_specs=pl.BlockSpec((Q, R * HD), lambda b, g, i: (b * NQ + i, g)),
        scratch_shapes=[
            pltpu.VMEM((HD + AUX_W, RQ), MXU_DTYPE),
            pltpu.VMEM((2, NS, Q), f32),
            pltpu.VMEM((1, RQ), f32),
            pltpu.VMEM((HA, RQ), f32),
            pltpu.VMEM((KT, RQ), f32), pltpu.VMEM((KT, RQ), f32),
            pltpu.VMEM((KT, RQ), MXU_DTYPE), pltpu.VMEM((KT, RQ), MXU_DTYPE),
            pltpu.VMEM((1, RQ), f32), pltpu.VMEM((1, RQ), f32),
            pltpu.VMEM((HA, S), MXU_DTYPE), pltpu.VMEM((HA, S), MXU_DTYPE),
            pltpu.SMEM((S // KT + 1,), jnp.int32),
        ],
        compiler_params=_params("parallel", "parallel", "arbitrary"),
        name="nsa_attention",
    )(h, k_cmp, aux_c, v_cmp_t, _overlap_table(NS, NC), kvh, aux_s, kvh, kvh, kvh, gates_t, _slope_rows(R, G), h)


def _mem_kernel(q_ref, z_ref, kv_ref, o_ref):
    hw = q_ref.shape[1] // MEM_HEADS
    bw = q_ref.shape[1]
    for h in range(MEM_HEADS):
        cs = slice(h * hw, (h + 1) * hw)
        s = _dot_nt(q_ref[:, cs], kv_ref[:, cs])
        m = jnp.max(s, axis=-1, keepdims=True)
        e = jnp.exp(s - m)
        p = e * (1.0 / jnp.sum(e, axis=-1, keepdims=True))
        o = _dot(p.astype(MXU_DTYPE), kv_ref[:, bw + h * hw:bw + (h + 1) * hw])
        o_ref[:, cs] = (o * _silu(z_ref[:, cs].astype(f32))).astype(o_ref.dtype)


def _mem_attention(h_mem, c0, bw, kv, B, S):
    T = B * S
    M = kv.shape[0] // B
    tq = min(512, S)
    nb = S // tq
    assert c0 % bw == 0
    qc = c0 // bw
    return pl.pallas_call(
        _mem_kernel,
        out_shape=jax.ShapeDtypeStruct((T, bw), MXU_DTYPE),
        grid=(B, nb),
        in_specs=[
            pl.BlockSpec((tq, bw), lambda b, i: (b * nb + i, qc)),
            pl.BlockSpec((tq, bw), lambda b, i: (b * nb + i, qc + 1)),
            pl.BlockSpec((M, 2 * bw), lambda b, i: (b, 0)),
        ],
        out_specs=pl.BlockSpec((tq, bw), lambda b, i: (b * nb + i, 0)),
        compiler_params=_params("parallel", "parallel"),
        name="mem_attention",
    )(h_mem, h_mem, kv)


def _merge_kernel(og_ref, on_ref, om_ref, wg_ref, wn_ref, wm_ref, ag_ref, an_ref, am_ref, o_ref):
    y = ag_ref[...].astype(f32) * _dot(og_ref[...], wg_ref[...])
    y = y + an_ref[...].astype(f32) * _dot(on_ref[...], wn_ref[...])
    y = y + am_ref[...].astype(f32) * _dot(om_ref[...], wm_ref[...])
    o_ref[...] = y.astype(o_ref.dtype)


def _merge(o_gla, o_nsa, o_mem, w_g, w_n, w_m, a, c0):
    T, bw = o_gla.shape
    D = w_g.shape[1]
    tm, tn = min(512, T), min(1024, D)
    nj = D // tn
    assert c0 % tn == 0
    osp = pl.BlockSpec((tm, bw), lambda j, i: (i, 0))
    wsp = pl.BlockSpec((bw, tn), lambda j, i: (0, j))
    asp = lambda c: pl.BlockSpec((tm, tn), lambda j, i: (i, c0 // tn + c * nj + j))
    return pl.pallas_call(
        _merge_kernel,
        out_shape=jax.ShapeDtypeStruct((T, D), MXU_DTYPE),
        grid=(nj, T // tm),
        in_specs=[osp, osp, osp, wsp, wsp, wsp, asp(0), asp(1), asp(2)],
        out_specs=pl.BlockSpec((tm, tn), lambda j, i: (i, j)),
        compiler_params=_params("parallel", "parallel"),
        name="branch_merge",
    )(o_gla, o_nsa, o_mem, w_g, w_n, w_m, a, a, a)


def _out_ln_kernel(m_ref, w_ref, x_ref, g_ref, b_ref, o_ref, *, alpha):
    o_ref[...] = alpha * x_ref[...] + _dot(m_ref[...], w_ref[...])

    def ln_rows(c, _):
        rows = pl.ds(pl.multiple_of(c * LN_ROWS, LN_ROWS), LN_ROWS)
        z = o_ref[rows, :]
        mu = jnp.mean(z, axis=-1, keepdims=True)
        zc = z - mu
        var = jnp.mean(zc * zc, axis=-1, keepdims=True)
        o_ref[rows, :] = zc * lax.rsqrt(var + LN_EPS) * g_ref[...] + b_ref[...]
        return 0

    lax.fori_loop(0, o_ref.shape[0] // LN_ROWS, ln_rows, 0)


def _out_ln(merged, w_out, x2, ln_g, ln_b, alpha):
    T, D = x2.shape
    tm = min(OUT_ROWS, T)
    return pl.pallas_call(
        functools.partial(_out_ln_kernel, alpha=alpha),
        out_shape=jax.ShapeDtypeStruct((T, D), x2.dtype),
        grid=(T // tm,),
        in_specs=[
            pl.BlockSpec((tm, D), lambda i: (i, 0)),
            pl.BlockSpec((D, D), lambda i: (0, 0), pipeline_mode=pl.Buffered(1)),
            pl.BlockSpec((tm, D), lambda i: (i, 0)),
            pl.BlockSpec((1, D), lambda i: (0, 0)),
            pl.BlockSpec((1, D), lambda i: (0, 0)),
        ],
        out_specs=pl.BlockSpec((tm, D), lambda i: (i, 0)),
        compiler_params=_params("parallel"),
        name="out_proj_layernorm",
    )(merged, w_out, x2, ln_g.reshape(1, D), ln_b.reshape(1, D))


def _layer(x, mem, w_in, b_merge, gla_w_a2, gla_b_a, gla_norm_g, nsa_pe_k, nsa_pe_v, nsa_wk1, nsa_wk2,
           nsa_wv1, nsa_wv2, w_mem_kv, w_br_gla, w_br_nsa, w_br_mem, w_out, ln_g, ln_b, depth):
    B, S, D = x.shape
    T = B * S
    bw = D // 2
    gk = bw // 2
    G, R = NSA_GROUPS, NSA_REP
    HD = bw // NSA_HEADS
    kvw = G * HD
    cdt = MXU_DTYPE

    o_ga = 2 * gk + 2 * bw
    o_nq = o_ga + GLA_LOWRANK
    o_nbg = o_nq + bw + 6 * kvw + bw
    o_mq = o_nbg + 3 * NSA_HEADS
    o_mrg = o_mq + 2 * bw
    assert w_in.shape[1] == o_mrg + N_BRANCH * D

    x2 = x.reshape(T, D)
    xb = x2.astype(cdt)
    ones = lambda n: jnp.ones((n,), f32)

    w_t = jnp.swapaxes(w_in, 0, 1)
    o_nkv, o_nz = o_nq + bw, o_nq + bw + 6 * kvw
    w_all = _repack(w_t, [(0, o_ga), (o_mq, 2 * bw), (o_nq, bw), (o_nz, bw), (o_nkv, 6 * kvw), (o_mrg, N_BRANCH * D)])
    c_gla, c_mem = 0, o_ga
    c_nq = c_mem + 2 * bw
    c_nz = c_nq + bw
    c_nkv = c_nz + bw
    c_mrg = c_nkv + 6 * kvw
    dk, mhd = gk // GLA_HEADS, bw // MEM_HEADS
    scale = jnp.concatenate([jnp.full((gk,), dk ** -0.5, f32), ones(o_ga - gk),
                             jnp.full((bw,), mhd ** -0.5, f32), ones(bw),
                             jnp.full((bw,), HD ** -0.5 * LOG2E, f32), ones(bw)])
    h = _project(xb, w_all, scale, cdt, n=c_nkv, nt=True, name="proj_in")
    kvh = _project(xb, w_all, ones(6 * kvw), cdt, c0=c_nkv, n=6 * kvw, nt=True, grouped=True, name="proj_nsa_kv")
    a = _project(xb, w_all, b_merge, cdt, c0=c_mrg, gate=True, nt=True, name="proj_merge_gates")
    h_small = _project_small(xb, w_t, o_ga, GLA_LOWRANK, o_nbg, 3 * NSA_HEADS)

    wa_pad = jnp.concatenate([gla_w_a2, jnp.zeros((SMALL_W - GLA_LOWRANK, gk), f32)], axis=0)
    o_gla = _gla(h, c_gla, gk, h_small, wa_pad, gla_b_a, gla_norm_g, B, S)

    NC = S // CMP_STRIDE
    blocks = lambda k: kvh[k * G:(k + 1) * G].reshape(G, B, NC, CMP_STRIDE * HD)
    k_cmp = _compress(blocks(0), nsa_pe_k, nsa_wk1, nsa_wk2, False)
    v_cmp_t = _compress(blocks(1), nsa_pe_v, nsa_wv1, nsa_wv2, True)
    gl = h_small[:, GLA_LOWRANK:GLA_LOWRANK + 3 * NSA_HEADS].reshape(T, G, 3 * R).transpose(1, 2, 0)
    gates_t = jnp.concatenate([gl, jnp.zeros((G, 16 - 3 * R, T), f32)], axis=1)
    o_nsa = _nsa_attention(h, c_nq, c_nz, bw, kvh, k_cmp, v_cmp_t, gates_t, B, S)

    M = mem.shape[1]
    kv = _project(mem.reshape(B * M, D).astype(cdt), w_mem_kv.astype(cdt), ones(2 * bw), cdt, name="proj_mem_kv")
    o_mem = _mem_attention(h, c_mem, bw, kv, B, S)

    merged = _merge(o_gla, o_nsa, o_mem, w_br_gla.astype(cdt), w_br_nsa.astype(cdt), w_br_mem.astype(cdt), a, 0)
    alpha = (2 * depth) ** 0.25
    return _out_ln(merged, w_out.astype(cdt), x2, ln_g, ln_b, alpha).reshape(B, S, D)


def kernel(x, mem, w_in, b_merge, gla_w_a2, gla_b_a, gla_norm_g, nsa_pe_k, nsa_pe_v, nsa_wk1, nsa_wk2, nsa_wv1, nsa_wv2, w_mem_kv, w_br_gla, w_br_nsa, w_br_mem, w_out, ln_g, ln_b):
    depth = w_in.shape[0]
    for l in range(depth):
        x = _layer(x, mem, w_in[l], b_merge[l], gla_w_a2[l], gla_b_a[l], gla_norm_g[l], nsa_pe_k[l], nsa_pe_v[l],
                   nsa_wk1[l], nsa_wk2[l], nsa_wv1[l], nsa_wv2[l], w_mem_kv[l], w_br_gla[l], w_br_nsa[l],
                   w_br_mem[l], w_out[l], ln_g[l], ln_b[l], depth)
    return x
```

```python
import functools

import jax
import jax.numpy as jnp
from jax import lax
from jax.experimental import pallas as pl
from jax.experimental.pallas import tpu as pltpu

N_BRANCH = 3
GLA_HEADS = 4
GLA_LOWRANK = 16
GLA_TAU = 16.0
GLA_CHUNK = 64
NSA_HEADS = 16
NSA_GROUPS = 4
NSA_REP = NSA_HEADS // NSA_GROUPS
CMP_LEN = 32
CMP_STRIDE = 16
SEL_LEN = 64
SEL_TOPK = 16
WINDOW = 512
Q_BLOCK = 128
FORCE_SCORE = 1e4
MEM_HEADS = 4
LN_EPS = 1e-5
RMS_EPS = 1e-6
NEG_INF = -1e30
LOG2E = 1.4426950408889634

LANES = 128
VMEM_LIMIT_BYTES = 56 * 1024 * 1024
MXU_DTYPE = jnp.bfloat16

SEL_KV_TILE = 512
NSA_Q = 256
GLA_STEP_CHUNKS = 4
LN_ROWS = 64
OUT_ROWS = 256
SMALL_W = LANES
AUX_W = LANES
MASK_COL0 = 16
V_AUG = 16
MASK_BIG = -NEG_INF

f32 = jnp.float32


def _dot(a, b):
    return jnp.dot(a, b, preferred_element_type=f32)


def _dot_nt(a, b):
    return lax.dot_general(a, b, (((1,), (1,)), ((), ())), preferred_element_type=f32)


def _dot_tn(a, b):
    return lax.dot_general(a, b, (((0,), (0,)), ((), ())), preferred_element_type=f32)


def _sigmoid(x):
    return 1.0 / (1.0 + jnp.exp(-x))


def _silu(x):
    return x * _sigmoid(x)


def _log_sigmoid(x):
    return -(jnp.maximum(-x, 0.0) + jnp.log(1.0 + jnp.exp(-jnp.abs(x))))


def _split2(x):
    hi = x.astype(MXU_DTYPE)
    lo = (x - hi.astype(f32)).astype(MXU_DTYPE)
    return hi, lo


def _split3(x):
    hi = x.astype(MXU_DTYPE)
    r1 = x - hi.astype(f32)
    mid = r1.astype(MXU_DTYPE)
    lo = (r1 - mid.astype(f32)).astype(MXU_DTYPE)
    return hi, mid, lo


def _params(*sem):
    return pltpu.CompilerParams(dimension_semantics=sem, vmem_limit_bytes=VMEM_LIMIT_BYTES)


def _proj_kernel(x_ref, w_ref, r_ref, o_ref, *, gate, nt, grouped):
    acc = _dot_nt(x_ref[...], w_ref[...]) if nt else _dot(x_ref[...], w_ref[...])
    res = (_sigmoid(acc + r_ref[...]) if gate else acc * r_ref[...]).astype(o_ref.dtype)
    if grouped:
        for c in range(o_ref.shape[0]):
            o_ref[c] = res[:, c * LANES:(c + 1) * LANES]
    else:
        o_ref[...] = res


def _project(x, w, row, out_dtype, c0=0, n=None, gate=False, nt=False, grouped=False, name="proj"):
    M, K = x.shape
    n = (w.shape[0] if nt else w.shape[1]) - c0 if n is None else n
    bm = min(1024, M)
    bn = min(1024, n)
    assert M % bm == 0 and n % bn == 0 and c0 % bn == 0
    wspec = (pl.BlockSpec((bn, K), lambda j, i: (c0 // bn + j, 0)) if nt
             else pl.BlockSpec((K, bn), lambda j, i: (0, c0 // bn + j)))
    if grouped:
        out_shape = jax.ShapeDtypeStruct((n // LANES, M, LANES), out_dtype)
        ospec = pl.BlockSpec((bn // LANES, bm, LANES), lambda j, i: (j, i, 0))
    else:
        out_shape = jax.ShapeDtypeStruct((M, n), out_dtype)
        ospec = pl.BlockSpec((bm, bn), lambda j, i: (i, j))
    return pl.pallas_call(
        functools.partial(_proj_kernel, gate=gate, nt=nt, grouped=grouped),
        out_shape=out_shape,
        grid=(n // bn, M // bm),
        in_specs=[pl.BlockSpec((bm, K), lambda j, i: (i, 0)), wspec, pl.BlockSpec((1, bn), lambda j, i: (0, j))],
        out_specs=ospec,
        compiler_params=_params("parallel", "parallel"),
        name=name,
    )(x, w, row.reshape(1, n).astype(f32))


def _row_window(rows, width, start16):
    return pl.BlockSpec((pl.Element(rows), pl.Element(width)), lambda *g: (start16(*g) * 16, 0))


def _proj_small_kernel(x_ref, wa_ref, wb_ref, o_ref):
    pad = jnp.zeros((LANES - wa_ref.shape[0] - wb_ref.shape[0], wa_ref.shape[1]), f32)
    w = jnp.concatenate([wa_ref[...], wb_ref[...], pad], axis=0)
    o_ref[...] = _dot_nt(x_ref[...], w.astype(x_ref.dtype))


def _project_small(x, w_t, ra, na, rb, nb):
    M, K = x.shape
    bm = min(1024, M)
    assert ra % 16 == 0 and rb % 16 == 0 and na % 8 == 0 and nb % 8 == 0 and na + nb <= LANES
    return pl.pallas_call(
        _proj_small_kernel,
        out_shape=jax.ShapeDtypeStruct((M, LANES), f32),
        grid=(M // bm,),
        in_specs=[
            pl.BlockSpec((bm, K), lambda i: (i, 0)),
            _row_window(na, K, lambda i: ra // 16),
            _row_window(nb, K, lambda i: rb // 16),
        ],
        out_specs=pl.BlockSpec((bm, LANES), lambda i: (i, 0)),
        compiler_params=_params("parallel"),
        name="proj_small",
    )(x, w_t, w_t)


def _repack_kernel(w_ref, o_ref):
    o_ref[...] = w_ref[...].astype(o_ref.dtype)


def _repack(w_t, segments, bn=1024):
    N, K = w_t.shape
    segs, blk = [], 0
    for src, n in segments:
        assert n % bn == 0 and src % 16 == 0 and src + n <= N
        segs.append((blk, blk + n // bn, src // 16 - blk * (bn // 16)))
        blk += n // bn

    def start16(j):
        s = jnp.int32(0)
        for lo, hi, off in segs:
            s = jnp.where((j >= lo) & (j < hi), j * (bn // 16) + off, s)
        return s

    return pl.pallas_call(
        _repack_kernel,
        out_shape=jax.ShapeDtypeStruct((blk * bn, K), MXU_DTYPE),
        grid=(blk,),
        in_specs=[_row_window(bn, K, start16)],
        out_specs=pl.BlockSpec((bn, K), lambda j: (j, 0)),
        compiler_params=_params("parallel"),
        name="repack_w_in",
    )(w_t)


def _gla_kernel(q_ref, k_ref, v_ref, z_ref, ga_ref, wa_ref, ba_ref, ng_ref, o_ref, st_ref, *, dk, dv):
    C = GLA_CHUNK

    @pl.when(pl.program_id(1) == 0)
    def _():
        st_ref[...] = jnp.zeros_like(st_ref)

    row = lax.broadcasted_iota(jnp.int32, (C, C), 0)
    col = lax.broadcasted_iota(jnp.int32, (C, C), 1)
    tril = row >= col
    ltri = jnp.where(tril, 1.0, 0.0).astype(MXU_DTYPE)
    wa_hi, wa_lo = _split2(wa_ref[...])
    for c in range(GLA_STEP_CHUNKS):
        rows = slice(c * C, (c + 1) * C)
        ga_hi, ga_lo = _split2(ga_ref[rows, :])
        zz = _dot(ga_hi, wa_hi) + _dot(ga_lo, wa_hi) + _dot(ga_hi, wa_lo) + ba_ref[...]
        la = _log_sigmoid(zz) * (1.0 / GLA_TAU)
        la_hi, la_mid, la_lo = _split3(la)
        bcum = _dot(ltri, la_hi) + _dot(ltri, la_mid) + _dot(ltri, la_lo)
        for h in range(GLA_HEADS):
            kc = slice(h * dk, (h + 1) * dk)
            vc = slice(h * dv, (h + 1) * dv)
            b = bcum[:, kc]
            bl = b[C - 1:C, :]
            qh = q_ref[rows, kc].astype(f32)
            kh = k_ref[rows, kc].astype(f32)
            vh = v_ref[rows, vc]
            q_d = (qh * jnp.exp(b)).astype(MXU_DTYPE)
            k_d = (kh * jnp.exp(-b)).astype(MXU_DTYPE)
            k_e = (kh * jnp.exp(bl - b)).astype(MXU_DTYPE)
            att = jnp.where(tril, _dot_nt(q_d, k_d), 0.0)
            st = st_ref[h]
            o = _dot(att.astype(MXU_DTYPE), vh) + _dot_nt(q_d, st.astype(MXU_DTYPE))
            st_ref[h] = st * jnp.exp(bl) + _dot_tn(vh, k_e)
            ms = jnp.mean(o * o, axis=-1, keepdims=True)
            on = o * lax.rsqrt(ms + RMS_EPS) * ng_ref[...]
            zg = z_ref[rows, vc].astype(f32)
            o_ref[rows, vc] = (on * _silu(zg)).astype(o_ref.dtype)


def _gla(h, c0, kw, h_small, wa_pad, b_a, norm_g, B, S):
    T = B * S
    dk, dv = kw // GLA_HEADS, 2 * kw // GLA_HEADS
    cs = GLA_STEP_CHUNKS * GLA_CHUNK
    nb = S // cs
    assert S % cs == 0 and c0 % (2 * kw) == 0
    h_gla = h
    rowmap = lambda col: (lambda b, i: (b * nb + i, col))
    qc, vc = c0 // kw, c0 // (2 * kw)
    return pl.pallas_call(
        functools.partial(_gla_kernel, dk=dk, dv=dv),
        out_shape=jax.ShapeDtypeStruct((T, 2 * kw), MXU_DTYPE),
        grid=(B, nb),
        in_specs=[
            pl.BlockSpec((cs, kw), rowmap(qc)),
            pl.BlockSpec((cs, kw), rowmap(qc + 1)),
            pl.BlockSpec((cs, 2 * kw), rowmap(vc + 1)),
            pl.BlockSpec((cs, 2 * kw), rowmap(vc + 2)),
            pl.BlockSpec((cs, SMALL_W), rowmap(0)),
            pl.BlockSpec((SMALL_W, kw), lambda b, i: (0, 0)),
            pl.BlockSpec((1, kw), lambda b, i: (0, 0)),
            pl.BlockSpec((1, dv), lambda b, i: (0, 0)),
        ],
        out_specs=pl.BlockSpec((cs, 2 * kw), rowmap(0)),
        scratch_shapes=[pltpu.VMEM((GLA_HEADS, dv, dk), f32)],
        compiler_params=_params("parallel", "arbitrary"),
        name="gla",
    )(h_gla, h_gla, h_gla, h_gla, h_small, wa_pad, b_a.reshape(1, kw), norm_g.reshape(1, dv))


def _compress_kernel(x_ref, pe_ref, w1_ref, w2_ref, o_ref, *, transpose_out):
    x = x_ref[0, 0].astype(f32)
    half = x.shape[1]
    xa = (x + pe_ref[0:1, :]).astype(MXU_DTYPE)
    xb = (x + pe_ref[1:2, :]).astype(MXU_DTYPE)
    ya = _dot(xa, w1_ref[0:half, :])
    yb = _dot(xb, w1_ref[half:2 * half, :])
    nc = x.shape[0]
    pre = ya + pltpu.roll(yb, nc - 1, 0)
    out = _dot(_silu(pre).astype(MXU_DTYPE), w2_ref[...])
    o_ref[0, 0] = (out.T if transpose_out else out).astype(o_ref.dtype)


def _compress(xblk, pe, w1, w2, transpose_out):
    G, B, NC, W = xblk.shape
    hd = w2.shape[0]
    oshape = (hd, NC) if transpose_out else (NC, hd)
    return pl.pallas_call(
        functools.partial(_compress_kernel, transpose_out=transpose_out),
        out_shape=jax.ShapeDtypeStruct((B, G) + oshape, MXU_DTYPE),
        grid=(B, G),
        in_specs=[
            pl.BlockSpec((1, 1, NC, W), lambda b, g: (g, b, 0, 0)),
            pl.BlockSpec((2, W), lambda b, g: (0, 0)),
            pl.BlockSpec((2 * W, hd), lambda b, g: (0, 0)),
            pl.BlockSpec((hd, hd), lambda b, g: (0, 0)),
        ],
        out_specs=pl.BlockSpec((1, 1) + oshape, lambda b, g: (b, g, 0, 0)),
        compiler_params=_params("parallel", "parallel"),
        name="nsa_compress",
    )(xblk, pe.reshape(2, W).astype(f32), w1.astype(MXU_DTYPE), w2.astype(MXU_DTYPE))


def _mask_heads(ok, s, R, Q, fill=NEG_INF):
    return jnp.concatenate([jnp.where(ok, s[:, r * Q:(r + 1) * Q], fill) for r in range(R)], axis=1)


def _softmax2_cols(s, ok, R, Q):
    s = _mask_heads(ok, s, R, Q)
    m = jnp.max(s, axis=0, keepdims=True)
    e = _mask_heads(ok, jnp.exp2(s - m), R, Q, 0.0)
    den = jnp.sum(e, axis=0, keepdims=True)
    return e * jnp.where(den > 0.0, 1.0 / den, 0.0)


def _nsa_kernel(q_ref, kc_ref, auxc_ref, vct_ref, ov_ref, ks3_ref, aux_ref, vs_ref, kw3_ref, vw_ref, gt_ref,
                sl_ref, z_ref, o_ref, qa_ref, sel_ref, m_ref, acc_ref, s0_ref, s1_ref, p0_ref, p1_ref, al0_ref,
                al1_ref, vst_ref, vwt_ref, idx_ref, *, S):
    Q, R = NSA_Q, NSA_REP
    s_refs, p_refs, al_refs = (s0_ref, s1_ref), (p0_ref, p1_ref), (al0_ref, al1_ref)
    ks_ref, kw_ref = ks3_ref.at[0], kw3_ref.at[0]
    HD = q_ref.shape[1] // R
    RQ = R * Q
    NC = S // CMP_STRIDE
    NS = S // SEL_LEN
    KT = SEL_KV_TILE
    BPT = KT // SEL_LEN
    NT = S // KT
    MR = HD + MASK_COL0
    qb = pl.program_id(2)
    start = qb * Q

    @pl.when(qb == 0)
    def _():
        tail = jnp.where(lax.broadcasted_iota(jnp.int32, (V_AUG, S), 0) == 0, 1.0, 0.0).astype(vst_ref.dtype)
        vst_ref[HD:, :] = tail
        vwt_ref[HD:, :] = tail

        def fill(c, carry):
            r = pl.ds(pl.multiple_of(c * LANES, LANES), LANES)
            vst_ref[0:HD, r] = vs_ref[0, r, :].T
            vwt_ref[0:HD, r] = vw_ref[0, r, :].T
            return carry

        lax.fori_loop(0, S // LANES, fill, 0)

    q = q_ref[...]
    qa_ref[0:HD, :] = jnp.concatenate(
        [q[c * LANES:(c + 1) * LANES, r * HD:(r + 1) * HD].T for r in range(R) for c in range(Q // LANES)], axis=1)
    qa_ref[HD:MR, :] = sl_ref[0]
    qa_ref[MR:, :] = jnp.zeros((AUX_W - MASK_COL0, RQ), qa_ref.dtype)
    qa = qa_ref[...]

    ok_c = (lax.broadcasted_iota(jnp.int32, (NC, Q), 0) * CMP_STRIDE + (CMP_LEN - 1)
            <= start + lax.broadcasted_iota(jnp.int32, (NC, Q), 1))
    s_c = _dot(jnp.concatenate([kc_ref[0, 0], auxc_ref[...]], axis=1), qa)
    p_c = _softmax2_cols(s_c, ok_c, R, Q)
    o_cmp = _dot(vct_ref[0, 0], p_c.astype(MXU_DTYPE))

    p_sum = p_c[:, 0:Q]
    for r in range(1, R):
        p_sum = p_sum + p_c[:, r * Q:(r + 1) * Q]
    ov = ov_ref[...]
    ps_hi, ps_mid, ps_lo = _split3(p_sum)
    imp = _dot(ov, ps_hi) + _dot(ov, ps_mid) + _dot(ov, ps_lo)

    blk = lax.broadcasted_iota(jnp.int32, (NS, Q), 0)
    tq1 = start + lax.broadcasted_iota(jnp.int32, (NS, Q), 1)
    cur = jnp.right_shift(tq1, SEL_LEN.bit_length() - 1)
    forced = (blk == 0) | (blk == cur) | (blk == cur - 1)
    score = jnp.where(forced, FORCE_SCORE, jnp.where(blk * SEL_LEN <= tq1, imp, -1.0))
    blk_f = blk.astype(f32)
    for _ in range(min(SEL_TOPK, NS)):
        m = jnp.max(score, axis=0, keepdims=True)
        first = jnp.min(jnp.where(score == m, blk_f, float(NS)), axis=0, keepdims=True)
        score = jnp.where(blk_f == first, -jnp.inf, score)
    sel = jnp.where(score == -jnp.inf, 1.0, 0.0)
    sel_past = jnp.where(blk * SEL_LEN < start, sel, 0.0)
    sel_ref[0] = sel
    sel_ref[1] = sel_past
    n_act = jnp.int32(0)
    for i in range(NT):
        idx_ref[n_act] = jnp.int32(i)
        n_act = n_act + (jnp.max(sel_past[i * BPT:(i + 1) * BPT, :]) > 0.0).astype(jnp.int32)

    WK = WINDOW + Q
    ws = pl.multiple_of(jnp.maximum(start - WINDOW, 0), Q)
    dist_w = (start - ws) + (lax.broadcasted_iota(jnp.int32, (WK, Q), 1)
                             - lax.broadcasted_iota(jnp.int32, (WK, Q), 0))
    ok_w = (dist_w >= 0) & (dist_w < WINDOW)
    s_w = _dot(jnp.concatenate([kw_ref[pl.ds(ws, WK), :], aux_ref[pl.ds(ws, WK), :]], axis=1), qa)
    s_w = _mask_heads(ok_w, s_w, R, Q)
    e_w = jnp.exp2(s_w - jnp.max(s_w, axis=0, keepdims=True)).astype(MXU_DTYPE)
    acc_w = _dot(vwt_ref[:, pl.ds(ws, WK)], e_w)
    o_win = acc_w[0:HD, :] * (1.0 / acc_w[HD:HD + 1, :])

    def mask_rows(which, b0, valid):
        mrow = jnp.where(valid, (sel_ref[which, pl.ds(b0, BPT), :] - 1.0) * MASK_BIG, -MASK_BIG)
        mrow = jnp.concatenate([mrow] * R, axis=1)
        qa_ref[MR:MR + 2 * BPT, :] = jnp.concatenate([mrow, jnp.zeros_like(mrow)], axis=0).astype(qa_ref.dtype)

    q0 = pl.multiple_of(start, Q)
    mask_rows(0, pl.multiple_of((start // KT) * BPT, BPT), True)
    s_o = _dot(jnp.concatenate([ks_ref[pl.ds(q0, Q), :], aux_ref[pl.ds(q0, Q), :]], axis=1), qa_ref[...])
    ok_o = lax.broadcasted_iota(jnp.int32, (Q, Q), 0) <= lax.broadcasted_iota(jnp.int32, (Q, Q), 1)
    s_o = _mask_heads(ok_o, s_o, R, Q)
    m_o = jnp.max(s_o, axis=0, keepdims=True)
    m_ref[...] = m_o
    acc_ref[...] = _dot(vst_ref[:, pl.ds(q0, Q)], jnp.exp2(s_o - m_o).astype(MXU_DTYPE))

    def tile_of(j):
        return idx_ref[jnp.clip(j, 0, jnp.maximum(n_act - 1, 0))]

    def scores(j, slot):
        i = tile_of(j)
        k0 = pl.multiple_of(i * KT, KT)
        mask_rows(1, pl.multiple_of(i * BPT, BPT), j < n_act)
        s_refs[slot][...] = _dot(jnp.concatenate([ks_ref[pl.ds(k0, KT), :], aux_ref[pl.ds(k0, KT), :]], axis=1),
                                 qa_ref[...])

    def softmax(slot):
        s = s_refs[slot][...]
        m_old = m_ref[...]
        m_new = jnp.maximum(m_old, jnp.max(s, axis=0, keepdims=True))
        p_refs[slot][...] = jnp.exp2(s - m_new).astype(MXU_DTYPE)
        al_refs[slot][...] = jnp.exp2(m_old - m_new)
        m_ref[...] = m_new

    def accumulate(j, slot):
        k0 = pl.multiple_of(tile_of(j) * KT, KT)
        acc_ref[...] = al_refs[slot][...] * acc_ref[...] + _dot(vst_ref[:, pl.ds(k0, KT)], p_refs[slot][...])

    scores(0, 0)
    scores(1, 1)
    softmax(0)

    def pipe(k, c):
        j = 2 * k
        softmax(1)
        accumulate(j, 0)
        scores(j + 2, 0)
        accumulate(j + 1, 1)
        scores(j + 3, 1)
        softmax(0)
        return c

    n_pairs = n_act // 2
    lax.fori_loop(0, n_pairs, pipe, 0)
    accumulate(2 * n_pairs, 0)
    acc_s = acc_ref[...]
    o_sel = acc_s[0:HD, :] * (1.0 / acc_s[HD:HD + 1, :])

    gates = _sigmoid(gt_ref[0])
    for r in range(R):
        cs = slice(r * Q, (r + 1) * Q)
        o_r = (gates[3 * r:3 * r + 1, :] * o_cmp[:, cs] + gates[3 * r + 1:3 * r + 2, :] * o_sel[:, cs]
               + gates[3 * r + 2:3 * r + 3, :] * o_win[:, cs])
        hs = slice(r * HD, (r + 1) * HD)
        o_ref[:, hs] = (o_r.T * _silu(z_ref[:, hs].astype(f32))).astype(o_ref.dtype)


def _aux_table(pos, onehot):
    hi = (pos // SEL_LEN) * SEL_LEN
    lo = pos % SEL_LEN
    col = jnp.arange(AUX_W)[None, :]
    t = jnp.where(col < 3, hi[:, None], jnp.where(col < 6, lo[:, None], 0)).astype(f32)
    if onehot:
        blk = (pos // SEL_LEN) % (SEL_KV_TILE // SEL_LEN)
        t = t + jnp.where(col == MASK_COL0 + blk[:, None], 1.0, 0.0)
    return t.astype(MXU_DTYPE)


def _slope_rows(R, G):
    sl = jnp.exp2(-8.0 * (jnp.arange(NSA_HEADS, dtype=f32) + 1.0) / NSA_HEADS) * LOG2E
    parts = _split3(sl)
    rows = jnp.stack(parts + parts, axis=0).astype(f32)
    rows = jnp.repeat(rows.reshape(6, G, R).transpose(1, 0, 2), NSA_Q, axis=2)
    pad = jnp.zeros((G, MASK_COL0 - 6, R * NSA_Q), f32)
    return jnp.concatenate([rows, pad], axis=1).astype(MXU_DTYPE)


def _overlap_table(NS, NC):
    jj = jnp.arange(NS)[:, None] * SEL_LEN
    nn = jnp.arange(NC)[None, :] * CMP_STRIDE
    return jnp.where((nn < jj + SEL_LEN) & (nn + (CMP_LEN - 1) >= jj), 1.0, 0.0).astype(MXU_DTYPE)


def _nsa_attention(h, cq, cz, bw, kvh, k_cmp, v_cmp_t, gates_t, B, S):
    T = B * S
    G, R = NSA_GROUPS, NSA_REP
    HD = bw // NSA_HEADS
    assert cq % (R * HD) == 0 and cz % (R * HD) == 0
    qc0, zc0 = cq // (R * HD), cz // (R * HD)
    Q = NSA_Q
    NQ = S // Q
    NC = S // CMP_STRIDE
    NS = S // SEL_LEN
    RQ = R * Q
    KT = SEL_KV_TILE
    assert S % KT == 0 and S % Q == 0 and KT % Q == 0 and S >= WINDOW + Q
    assert S + CMP_LEN <= SEL_LEN * 256 and 2 * (KT // SEL_LEN) <= AUX_W - MASK_COL0
    aux_s = _aux_table(jnp.arange(S), True)
    aux_c = _aux_table(jnp.arange(NC) * CMP_STRIDE + (CMP_LEN - 1), False)
    HA = HD + V_AUG
    slab = lambda k: pl.BlockSpec((1, S, HD), lambda b, g, i: (k * G + g, b, 0))
    return pl.pallas_call(
        functools.partial(_nsa_kernel, S=S),
        out_shape=jax.ShapeDtypeStruct((T, bw), MXU_DTYPE),
        grid=(B, G, NQ),
        in_specs=[
            pl.BlockSpec((Q, R * HD), lambda b, g, i: (b * NQ + i, qc0 + g)),
            pl.BlockSpec((1, 1, NC, HD), lambda b, g, i: (b, g, 0, 0)),
            pl.BlockSpec((NC, AUX_W), lambda b, g, i: (0, 0)),
            pl.BlockSpec((1, 1, HD, NC), lambda b, g, i: (b, g, 0, 0)),
            pl.BlockSpec((NS, NC), lambda b, g, i: (0, 0)),
            slab(2),
            pl.BlockSpec((S, AUX_W), lambda b, g, i: (0, 0)),
            slab(3),
            slab(4),
            slab(5),
            pl.BlockSpec((1, 16, Q), lambda b, g, i: (g, 0, b * NQ + i)),
            pl.BlockSpec((1, MASK_COL0, RQ), lambda b, g, i: (g, 0, 0)),
            pl.BlockSpec((Q, R * HD), lambda b, g, i: (b * NQ + i, zc0 + g)),
        ],
        out_specs=pl.BlockSpec((Q, R * HD), lambda b, g, i: (b * NQ + i, g)),
        scratch_shapes=[
            pltpu.VMEM((HD + AUX_W, RQ), MXU_DTYPE),
            pltpu.VMEM((2, NS, Q), f32),
            pltpu.VMEM((1, RQ), f32),
            pltpu.VMEM((HA, RQ), f32),
            pltpu.VMEM((KT, RQ), f32), pltpu.VMEM((KT, RQ), f32),
            pltpu.VMEM((KT, RQ), MXU_DTYPE), pltpu.VMEM((KT, RQ), MXU_DTYPE),
            pltpu.VMEM((1, RQ), f32), pltpu.VMEM((1, RQ), f32),
            pltpu.VMEM((HA, S), MXU_DTYPE), pltpu.VMEM((HA, S), MXU_DTYPE),
            pltpu.SMEM((S // KT + 1,), jnp.int32),
        ],
        compiler_params=_params("parallel", "parallel", "arbitrary"),
        name="nsa_attention",
    )(h, k_cmp, aux_c, v_cmp_t, _overlap_table(NS, NC), kvh, aux_s, kvh, kvh, kvh, gates_t, _slope_rows(R, G), h)


def _mem_kernel(q_ref, z_ref, kv_ref, o_ref):
    hw = q_ref.shape[1] // MEM_HEADS
    bw = q_ref.shape[1]
    for h in range(MEM_HEADS):
        cs = slice(h * hw, (h + 1) * hw)
        s = _dot_nt(q_ref[:, cs], kv_ref[:, cs])
        m = jnp.max(s, axis=-1, keepdims=True)
        e = jnp.exp(s - m)
        p = e * (1.0 / jnp.sum(e, axis=-1, keepdims=True))
        o = _dot(p.astype(MXU_DTYPE), kv_ref[:, bw + h * hw:bw + (h + 1) * hw])
        o_ref[:, cs] = (o * _silu(z_ref[:, cs].astype(f32))).astype(o_ref.dtype)


def _mem_attention(h_mem, c0, bw, kv, B, S):
    T = B * S
    M = kv.shape[0] // B
    tq = min(512, S)
    nb = S // tq
    assert c0 % bw == 0
    qc = c0 // bw
    return pl.pallas_call(
        _mem_kernel,
        out_shape=jax.ShapeDtypeStruct((T, bw), MXU_DTYPE),
        grid=(B, nb),
        in_specs=[
            pl.BlockSpec((tq, bw), lambda b, i: (b * nb + i, qc)),
            pl.BlockSpec((tq, bw), lambda b, i: (b * nb + i, qc + 1)),
            pl.BlockSpec((M, 2 * bw), lambda b, i: (b, 0)),
        ],
        out_specs=pl.BlockSpec((tq, bw), lambda b, i: (b * nb + i, 0)),
        compiler_params=_params("parallel", "parallel"),
        name="mem_attention",
    )(h_mem, h_mem, kv)


def _merge_kernel(og_ref, on_ref, om_ref, wg_ref, wn_ref, wm_ref, ag_ref, an_ref, am_ref, o_ref):
    y = ag_ref[...].astype(f32) * _dot(og_ref[...], wg_ref[...])
    y = y + an_ref[...].astype(f32) * _dot(on_ref[...], wn_ref[...])
    y = y + am_ref[...].astype(f32) * _dot(om_ref[...], wm_ref[...])
    o_ref[...] = y.astype(o_ref.dtype)


def _merge(o_gla, o_nsa, o_mem, w_g, w_n, w_m, a, c0):
    T, bw = o_gla.shape
    D = w_g.shape[1]
    tm, tn = min(512, T), min(1024, D)
    nj = D // tn
    assert c0 % tn == 0
    osp = pl.BlockSpec((tm, bw), lambda j, i: (i, 0))
    wsp = pl.BlockSpec((bw, tn), lambda j, i: (0, j))
    asp = lambda c: pl.BlockSpec((tm, tn), lambda j, i: (i, c0 // tn + c * nj + j))
    return pl.pallas_call(
        _merge_kernel,
        out_shape=jax.ShapeDtypeStruct((T, D), MXU_DTYPE),
        grid=(nj, T // tm),
        in_specs=[osp, osp, osp, wsp, wsp, wsp, asp(0), asp(1), asp(2)],
        out_specs=pl.BlockSpec((tm, tn), lambda j, i: (i, j)),
        compiler_params=_params("parallel", "parallel"),
        name="branch_merge",
    )(o_gla, o_nsa, o_mem, w_g, w_n, w_m, a, a, a)


def _out_ln_kernel(m_ref, w_ref, x_ref, g_ref, b_ref, o_ref, *, alpha):
    o_ref[...] = alpha * x_ref[...] + _dot(m_ref[...], w_ref[...])

    def ln_rows(c, _):
        rows = pl.ds(pl.multiple_of(c * LN_ROWS, LN_ROWS), LN_ROWS)
        z = o_ref[rows, :]
        mu = jnp.mean(z, axis=-1, keepdims=True)
        zc = z - mu
        var = jnp.mean(zc * zc, axis=-1, keepdims=True)
        o_ref[rows, :] = zc * lax.rsqrt(var + LN_EPS) * g_ref[...] + b_ref[...]
        return 0

    lax.fori_loop(0, o_ref.shape[0] // LN_ROWS, ln_rows, 0)


def _out_ln(merged, w_out, x2, ln_g, ln_b, alpha):
    T, D = x2.shape
    tm = min(OUT_ROWS, T)
    return pl.pallas_call(
        functools.partial(_out_ln_kernel, alpha=alpha),
        out_shape=jax.ShapeDtypeStruct((T, D), x2.dtype),
        grid=(T // tm,),
        in_specs=[
            pl.BlockSpec((tm, D), lambda i: (i, 0)),
            pl.BlockSpec((D, D), lambda i: (0, 0), pipeline_mode=pl.Buffered(1)),
            pl.BlockSpec((tm, D), lambda i: (i, 0)),
            pl.BlockSpec((1, D), lambda i: (0, 0)),
            pl.BlockSpec((1, D), lambda i: (0, 0)),
        ],
        out_specs=pl.BlockSpec((tm, D), lambda i: (i, 0)),
        compiler_params=_params("parallel"),
        name="out_proj_layernorm",
    )(merged, w_out, x2, ln_g.reshape(1, D), ln_b.reshape(1, D))


def _layer(x, mem, w_in, b_merge, gla_w_a2, gla_b_a, gla_norm_g, nsa_pe_k, nsa_pe_v, nsa_wk1, nsa_wk2,
           nsa_wv1, nsa_wv2, w_mem_kv, w_br_gla, w_br_nsa, w_br_mem, w_out, ln_g, ln_b, depth):
    B, S, D = x.shape
    T = B * S
    bw = D // 2
    gk = bw // 2
    G, R = NSA_GROUPS, NSA_REP
    HD = bw // NSA_HEADS
    kvw = G * HD
    cdt = MXU_DTYPE

    o_ga = 2 * gk + 2 * bw
    o_nq = o_ga + GLA_LOWRANK
    o_nbg = o_nq + bw + 6 * kvw + bw
    o_mq = o_nbg + 3 * NSA_HEADS
    o_mrg = o_mq + 2 * bw
    assert w_in.shape[1] == o_mrg + N_BRANCH * D

    x2 = x.reshape(T, D)
    xb = x2.astype(cdt)
    ones = lambda n: jnp.ones((n,), f32)

    w_t = jnp.swapaxes(w_in, 0, 1)
    o_nkv, o_nz = o_nq + bw, o_nq + bw + 6 * kvw
    w_all = _repack(w_t, [(0, o_ga), (o_mq, 2 * bw), (o_nq, bw), (o_nz, bw), (o_nkv, 6 * kvw), (o_mrg, N_BRANCH * D)])
    c_gla, c_mem = 0, o_ga
    c_nq = c_mem + 2 * bw
    c_nz = c_nq + bw
    c_nkv = c_nz + bw
    c_mrg = c_nkv + 6 * kvw
    dk, mhd = gk // GLA_HEADS, bw // MEM_HEADS
    scale = jnp.concatenate([jnp.full((gk,), dk ** -0.5, f32), ones(o_ga - gk),
                             jnp.full((bw,), mhd ** -0.5, f32), ones(bw),
                             jnp.full((bw,), HD ** -0.5 * LOG2E, f32), ones(bw)])
    h = _project(xb, w_all, scale, cdt, n=c_nkv, nt=True, name="proj_in")
    kvh = _project(xb, w_all, ones(6 * kvw), cdt, c0=c_nkv, n=6 * kvw, nt=True, grouped=True, name="proj_nsa_kv")
    a = _project(xb, w_all, b_merge, cdt, c0=c_mrg, gate=True, nt=True, name="proj_merge_gates")
    h_small = _project_small(xb, w_t, o_ga, GLA_LOWRANK, o_nbg, 3 * NSA_HEADS)

    wa_pad = jnp.concatenate([gla_w_a2, jnp.zeros((SMALL_W - GLA_LOWRANK, gk), f32)], axis=0)
    o_gla = _gla(h, c_gla, gk, h_small, wa_pad, gla_b_a, gla_norm_g, B, S)

    NC = S // CMP_STRIDE
    blocks = lambda k: kvh[k * G:(k + 1) * G].reshape(G, B, NC, CMP_STRIDE * HD)
    k_cmp = _compress(blocks(0), nsa_pe_k, nsa_wk1, nsa_wk2, False)
    v_cmp_t = _compress(blocks(1), nsa_pe_v, nsa_wv1, nsa_wv2, True)
    gl = h_small[:, GLA_LOWRANK:GLA_LOWRANK + 3 * NSA_HEADS].reshape(T, G, 3 * R).transpose(1, 2, 0)
    gates_t = jnp.concatenate([gl, jnp.zeros((G, 16 - 3 * R, T), f32)], axis=1)
    o_nsa = _nsa_attention(h, c_nq, c_nz, bw, kvh, k_cmp, v_cmp_t, gates_t, B, S)

    M = mem.shape[1]
    kv = _project(mem.reshape(B * M, D).astype(cdt), w_mem_kv.astype(cdt), ones(2 * bw), cdt, name="proj_mem_kv")
    o_mem = _mem_attention(h, c_mem, bw, kv, B, S)

    merged = _merge(o_gla, o_nsa, o_mem, w_br_gla.astype(cdt), w_br_nsa.astype(cdt), w_br_mem.astype(cdt), a, 0)
    alpha = (2 * depth) ** 0.25
    return _out_ln(merged, w_out.astype(cdt), x2, ln_g, ln_b, alpha).reshape(B, S, D)


def kernel(x, mem, w_in, b_merge, gla_w_a2, gla_b_a, gla_norm_g, nsa_pe_k, nsa_pe_v, nsa_wk1, nsa_wk2, nsa_wv1, nsa_wv2, w_mem_kv, w_br_gla, w_br_nsa, w_br_mem, w_out, ln_g, ln_b):
    depth = w_in.shape[0]
    for l in range(depth):
        x = _layer(x, mem, w_in[l], b_merge[l], gla_w_a2[l], gla_b_a[l], gla_norm_g[l], nsa_pe_k[l], nsa_pe_v[l],
                   nsa_wk1[l], nsa_wk2[l], nsa_wv1[l], nsa_wv2[l], w_mem_kv[l], w_br_gla[l], w_br_nsa[l],
                   w_br_mem[l], w_out[l], ln_g[l], ln_b[l], depth)
    return x
```

```python
import functools

import jax
import jax.numpy as jnp
from jax import lax
from jax.experimental import pallas as pl
from jax.experimental.pallas import tpu as pltpu

N_BRANCH = 3
GLA_HEADS = 4
GLA_LOWRANK = 16
GLA_TAU = 16.0
GLA_CHUNK = 64
NSA_HEADS = 16
NSA_GROUPS = 4
NSA_REP = NSA_HEADS // NSA_GROUPS
CMP_LEN = 32
CMP_STRIDE = 16
SEL_LEN = 64
SEL_TOPK = 16
WINDOW = 512
FORCE_SCORE = 1e4
MEM_HEADS = 4
LN_EPS = 1e-5
RMS_EPS = 1e-6
NEG_INF = -1e30
LOG2E = 1.4426950408889634

LANES = 128
VMEM_LIMIT_BYTES = 56 * 1024 * 1024
MXU_DTYPE = jnp.bfloat16

SEL_KV_TILE = 512
NSA_Q = 256
GLA_STEP_CHUNKS = 4
LN_ROWS = 64
OUT_ROWS = 256
SMALL_W = LANES
AUX_W = LANES
MASK_COL0 = 16
V_AUG = 16
MASK_BIG = -NEG_INF

f32 = jnp.float32


def _dot(a, b):
    return jnp.dot(a, b, preferred_element_type=f32)


def _dot_nt(a, b):
    return lax.dot_general(a, b, (((1,), (1,)), ((), ())), preferred_element_type=f32)


def _dot_tn(a, b):
    return lax.dot_general(a, b, (((0,), (0,)), ((), ())), preferred_element_type=f32)


def _sigmoid(x):
    return 1.0 / (1.0 + jnp.exp(-x))


def _silu(x):
    return x * _sigmoid(x)


def _log_sigmoid(x):
    return -(jnp.maximum(-x, 0.0) + jnp.log(1.0 + jnp.exp(-jnp.abs(x))))


def _split2(x):
    hi = x.astype(MXU_DTYPE)
    lo = (x - hi.astype(f32)).astype(MXU_DTYPE)
    return hi, lo


def _split3(x):
    hi = x.astype(MXU_DTYPE)
    r1 = x - hi.astype(f32)
    mid = r1.astype(MXU_DTYPE)
    lo = (r1 - mid.astype(f32)).astype(MXU_DTYPE)
    return hi, mid, lo


def _params(*sem):
    return pltpu.CompilerParams(dimension_semantics=sem, vmem_limit_bytes=VMEM_LIMIT_BYTES)


def _proj_kernel(x_ref, w_ref, r_ref, o_ref, *, gate, nt, grouped):
    acc = _dot_nt(x_ref[...], w_ref[...]) if nt else _dot(x_ref[...], w_ref[...])
    res = (_sigmoid(acc + r_ref[...]) if gate else acc * r_ref[...]).astype(o_ref.dtype)
    if grouped:
        for c in range(o_ref.shape[0]):
            o_ref[c] = res[:, c * LANES:(c + 1) * LANES]
    else:
        o_ref[...] = res


def _project(x, w, row, out_dtype, c0=0, n=None, gate=False, nt=False, grouped=False, name="proj"):
    M, K = x.shape
    n = (w.shape[0] if nt else w.shape[1]) - c0 if n is None else n
    bm = min(1024, M)
    bn = min(1024, n)
    assert M % bm == 0 and n % bn == 0 and c0 % bn == 0
    wspec = (pl.BlockSpec((bn, K), lambda j, i: (c0 // bn + j, 0)) if nt
             else pl.BlockSpec((K, bn), lambda j, i: (0, c0 // bn + j)))
    if grouped:
        out_shape = jax.ShapeDtypeStruct((n // LANES, M, LANES), out_dtype)
        ospec = pl.BlockSpec((bn // LANES, bm, LANES), lambda j, i: (j, i, 0))
    else:
        out_shape = jax.ShapeDtypeStruct((M, n), out_dtype)
        ospec = pl.BlockSpec((bm, bn), lambda j, i: (i, j))
    return pl.pallas_call(
        functools.partial(_proj_kernel, gate=gate, nt=nt, grouped=grouped),
        out_shape=out_shape,
        grid=(n // bn, M // bm),
        in_specs=[pl.BlockSpec((bm, K), lambda j, i: (i, 0)), wspec, pl.BlockSpec((1, bn), lambda j, i: (0, j))],
        out_specs=ospec,
        compiler_params=_params("parallel", "parallel"),
        name=name,
    )(x, w, row.reshape(1, n).astype(f32))


def _row_window(rows, width, start16):
    return pl.BlockSpec((pl.Element(rows), pl.Element(width)), lambda *g: (start16(*g) * 16, 0))


def _proj_small_kernel(x_ref, wa_ref, wb_ref, o_ref, xb_ref):
    xb = x_ref[...].astype(xb_ref.dtype)
    xb_ref[...] = xb
    pad = jnp.zeros((LANES - wa_ref.shape[0] - wb_ref.shape[0], wa_ref.shape[1]), f32)
    w = jnp.concatenate([wa_ref[...], wb_ref[...], pad], axis=0)
    o_ref[...] = _dot_nt(xb, w.astype(xb.dtype))


def _project_small(x, w_t, ra, na, rb, nb):
    M, K = x.shape
    bm = min(512, M)
    assert ra % 16 == 0 and rb % 16 == 0 and na % 8 == 0 and nb % 8 == 0 and na + nb <= LANES
    return pl.pallas_call(
        _proj_small_kernel,
        out_shape=(jax.ShapeDtypeStruct((M, LANES), f32), jax.ShapeDtypeStruct((M, K), MXU_DTYPE)),
        grid=(M // bm,),
        in_specs=[
            pl.BlockSpec((bm, K), lambda i: (i, 0)),
            _row_window(na, K, lambda i: ra // 16),
            _row_window(nb, K, lambda i: rb // 16),
        ],
        out_specs=(pl.BlockSpec((bm, LANES), lambda i: (i, 0)), pl.BlockSpec((bm, K), lambda i: (i, 0))),
        compiler_params=_params("parallel"),
        name="proj_small",
    )(x, w_t, w_t)


def _repack_kernel(w_ref, o_ref):
    o_ref[...] = w_ref[...].astype(o_ref.dtype)


def _repack(w_t, segments, bn=1024):
    N, K = w_t.shape
    segs, blk = [], 0
    for src, n in segments:
        assert n % bn == 0 and src % 16 == 0 and src + n <= N
        segs.append((blk, blk + n // bn, src // 16 - blk * (bn // 16)))
        blk += n // bn

    def start16(j):
        s = jnp.int32(0)
        for lo, hi, off in segs:
            s = jnp.where((j >= lo) & (j < hi), j * (bn // 16) + off, s)
        return s

    return pl.pallas_call(
        _repack_kernel,
        out_shape=jax.ShapeDtypeStruct((blk * bn, K), MXU_DTYPE),
        grid=(blk,),
        in_specs=[_row_window(bn, K, start16)],
        out_specs=pl.BlockSpec((bn, K), lambda j: (j, 0)),
        compiler_params=_params("parallel"),
        name="repack_w_in",
    )(w_t)


def _gla_kernel(q_ref, k_ref, v_ref, z_ref, ga_ref, wa_ref, ba_ref, ng_ref, o_ref, st_ref, *, dk, dv):
    C = GLA_CHUNK

    @pl.when(pl.program_id(1) == 0)
    def _():
        st_ref[...] = jnp.zeros_like(st_ref)

    row = lax.broadcasted_iota(jnp.int32, (C, C), 0)
    col = lax.broadcasted_iota(jnp.int32, (C, C), 1)
    tril = row >= col
    ltri = jnp.where(tril, 1.0, 0.0).astype(MXU_DTYPE)
    wa_hi, wa_lo = _split2(wa_ref[...])
    for c in range(GLA_STEP_CHUNKS):
        rows = slice(c * C, (c + 1) * C)
        ga_hi, ga_lo = _split2(ga_ref[rows, :])
        zz = _dot(ga_hi, wa_hi) + _dot(ga_lo, wa_hi) + _dot(ga_hi, wa_lo) + ba_ref[...]
        la = _log_sigmoid(zz) * (1.0 / GLA_TAU)
        la_hi, la_mid, la_lo = _split3(la)
        bcum = _dot(ltri, la_hi) + _dot(ltri, la_mid) + _dot(ltri, la_lo)
        for h in range(GLA_HEADS):
            kc = slice(h * dk, (h + 1) * dk)
            vc = slice(h * dv, (h + 1) * dv)
            b = bcum[:, kc]
            bl = b[C - 1:C, :]
            qh = q_ref[rows, kc].astype(f32)
            kh = k_ref[rows, kc].astype(f32)
            vh = v_ref[rows, vc]
            q_d = (qh * jnp.exp(b)).astype(MXU_DTYPE)
            k_d = (kh * jnp.exp(-b)).astype(MXU_DTYPE)
            k_e = (kh * jnp.exp(bl - b)).astype(MXU_DTYPE)
            att = jnp.where(tril, _dot_nt(q_d, k_d), 0.0)
            st = st_ref[h]
            o = _dot(att.astype(MXU_DTYPE), vh) + _dot_nt(q_d, st.astype(MXU_DTYPE))
            st_ref[h] = st * jnp.exp(bl) + _dot_tn(vh, k_e)
            ms = jnp.mean(o * o, axis=-1, keepdims=True)
            on = o * lax.rsqrt(ms + RMS_EPS) * ng_ref[...]
            zg = z_ref[rows, vc].astype(f32)
            o_ref[rows, vc] = (on * _silu(zg)).astype(o_ref.dtype)


def _gla(h, c0, kw, h_small, wa_pad, b_a, norm_g, B, S):
    T = B * S
    dk, dv = kw // GLA_HEADS, 2 * kw // GLA_HEADS
    cs = GLA_STEP_CHUNKS * GLA_CHUNK
    nb = S // cs
    assert S % cs == 0 and c0 % (2 * kw) == 0
    h_gla = h
    rowmap = lambda col: (lambda b, i: (b * nb + i, col))
    qc, vc = c0 // kw, c0 // (2 * kw)
    return pl.pallas_call(
        functools.partial(_gla_kernel, dk=dk, dv=dv),
        out_shape=jax.ShapeDtypeStruct((T, 2 * kw), MXU_DTYPE),
        grid=(B, nb),
        in_specs=[
            pl.BlockSpec((cs, kw), rowmap(qc)),
            pl.BlockSpec((cs, kw), rowmap(qc + 1)),
            pl.BlockSpec((cs, 2 * kw), rowmap(vc + 1)),
            pl.BlockSpec((cs, 2 * kw), rowmap(vc + 2)),
            pl.BlockSpec((cs, SMALL_W), rowmap(0)),
            pl.BlockSpec((SMALL_W, kw), lambda b, i: (0, 0)),
            pl.BlockSpec((1, kw), lambda b, i: (0, 0)),
            pl.BlockSpec((1, dv), lambda b, i: (0, 0)),
        ],
        out_specs=pl.BlockSpec((cs, 2 * kw), rowmap(0)),
        scratch_shapes=[pltpu.VMEM((GLA_HEADS, dv, dk), f32)],
        compiler_params=_params("parallel", "arbitrary"),
        name="gla",
    )(h_gla, h_gla, h_gla, h_gla, h_small, wa_pad, b_a.reshape(1, kw), norm_g.reshape(1, dv))


def _compress_kernel(x_ref, pe_ref, w1_ref, w2_ref, o_ref, *, transpose_out):
    x = x_ref[0, 0].astype(f32)
    half = x.shape[1]
    xa = (x + pe_ref[0:1, :]).astype(MXU_DTYPE)
    xb = (x + pe_ref[1:2, :]).astype(MXU_DTYPE)
    ya = _dot(xa, w1_ref[0:half, :])
    yb = _dot(xb, w1_ref[half:2 * half, :])
    nc = x.shape[0]
    pre = ya + pltpu.roll(yb, nc - 1, 0)
    out = _dot(_silu(pre).astype(MXU_DTYPE), w2_ref[...])
    o_ref[0, 0] = (out.T if transpose_out else out).astype(o_ref.dtype)


def _compress(xblk, pe, w1, w2, transpose_out):
    G, B, NC, W = xblk.shape
    hd = w2.shape[0]
    oshape = (hd, NC) if transpose_out else (NC, hd)
    return pl.pallas_call(
        functools.partial(_compress_kernel, transpose_out=transpose_out),
        out_shape=jax.ShapeDtypeStruct((B, G) + oshape, MXU_DTYPE),
        grid=(B, G),
        in_specs=[
            pl.BlockSpec((1, 1, NC, W), lambda b, g: (g, b, 0, 0)),
            pl.BlockSpec((2, W), lambda b, g: (0, 0)),
            pl.BlockSpec((2 * W, hd), lambda b, g: (0, 0)),
            pl.BlockSpec((hd, hd), lambda b, g: (0, 0)),
        ],
        out_specs=pl.BlockSpec((1, 1) + oshape, lambda b, g: (b, g, 0, 0)),
        compiler_params=_params("parallel", "parallel"),
        name="nsa_compress",
    )(xblk, pe.reshape(2, W).astype(f32), w1.astype(MXU_DTYPE), w2.astype(MXU_DTYPE))


def _mask_heads(ok, s, R, Q, fill=NEG_INF):
    return jnp.concatenate([jnp.where(ok, s[:, r * Q:(r + 1) * Q], fill) for r in range(R)], axis=1)


def _softmax2_cols(s, ok, R, Q):
    s = _mask_heads(ok, s, R, Q)
    m = jnp.max(s, axis=0, keepdims=True)
    e = _mask_heads(ok, jnp.exp2(s - m), R, Q, 0.0)
    den = jnp.sum(e, axis=0, keepdims=True)
    return e * jnp.where(den > 0.0, 1.0 / den, 0.0)


def _nsa_kernel(q_ref, kc_ref, auxc_ref, vct_ref, ov_ref, ks3_ref, aux_ref, vs_ref, kw3_ref, vw_ref, gt_ref,
                sl_ref, z_ref, o_ref, qa_ref, sel_ref, m_ref, acc_ref, s0_ref, s1_ref, p0_ref, p1_ref, al0_ref,
                al1_ref, vst_ref, vwt_ref, idx_ref, *, S):
    Q, R = NSA_Q, NSA_REP
    s_refs, p_refs, al_refs = (s0_ref, s1_ref), (p0_ref, p1_ref), (al0_ref, al1_ref)
    ks_ref, kw_ref = ks3_ref.at[0], kw3_ref.at[0]
    HD = q_ref.shape[1] // R
    RQ = R * Q
    NC = S // CMP_STRIDE
    NS = S // SEL_LEN
    KT = SEL_KV_TILE
    BPT = KT // SEL_LEN
    NT = S // KT
    MR = HD + MASK_COL0
    qb = pl.program_id(2)
    start = qb * Q

    @pl.when(qb == 0)
    def _():
        tail = jnp.where(lax.broadcasted_iota(jnp.int32, (V_AUG, S), 0) == 0, 1.0, 0.0).astype(vst_ref.dtype)
        vst_ref[HD:, :] = tail
        vwt_ref[HD:, :] = tail

        def fill(c, carry):
            r = pl.ds(pl.multiple_of(c * LANES, LANES), LANES)
            vst_ref[0:HD, r] = vs_ref[0, r, :].T
            vwt_ref[0:HD, r] = vw_ref[0, r, :].T
            return carry

        lax.fori_loop(0, S // LANES, fill, 0)

    q = q_ref[...]
    qa_ref[0:HD, :] = jnp.concatenate(
        [q[c * LANES:(c + 1) * LANES, r * HD:(r + 1) * HD].T for r in range(R) for c in range(Q // LANES)], axis=1)
    qa_ref[HD:MR, :] = sl_ref[0]
    qa_ref[MR:, :] = jnp.zeros((AUX_W - MASK_COL0, RQ), qa_ref.dtype)
    qa = qa_ref[...]

    ok_c = (lax.broadcasted_iota(jnp.int32, (NC, Q), 0) * CMP_STRIDE + (CMP_LEN - 1)
            <= start + lax.broadcasted_iota(jnp.int32, (NC, Q), 1))
    s_c = _dot(jnp.concatenate([kc_ref[0, 0], auxc_ref[...]], axis=1), qa)
    p_c = _softmax2_cols(s_c, ok_c, R, Q)
    o_cmp = _dot(vct_ref[0, 0], p_c.astype(MXU_DTYPE))

    p_sum = p_c[:, 0:Q]
    for r in range(1, R):
        p_sum = p_sum + p_c[:, r * Q:(r + 1) * Q]
    ov = ov_ref[...]
    ps_hi, ps_mid, ps_lo = _split3(p_sum)
    imp = _dot(ov, ps_hi) + _dot(ov, ps_mid) + _dot(ov, ps_lo)

    blk = lax.broadcasted_iota(jnp.int32, (NS, Q), 0)
    tq1 = start + lax.broadcasted_iota(jnp.int32, (NS, Q), 1)
    cur = jnp.right_shift(tq1, SEL_LEN.bit_length() - 1)
    forced = (blk == 0) | (blk == cur) | (blk == cur - 1)
    score = jnp.where(forced, FORCE_SCORE, jnp.where(blk * SEL_LEN <= tq1, imp, -1.0))
    blk_f = blk.astype(f32)
    for _ in range(min(SEL_TOPK, NS)):
        m = jnp.max(score, axis=0, keepdims=True)
        first = jnp.min(jnp.where(score == m, blk_f, float(NS)), axis=0, keepdims=True)
        score = jnp.where(blk_f == first, -jnp.inf, score)
    sel = jnp.where(score == -jnp.inf, 1.0, 0.0)
    sel_past = jnp.where(blk * SEL_LEN < start, sel, 0.0)
    sel_ref[0] = sel
    sel_ref[1] = sel_past
    n_act = jnp.int32(0)
    for i in range(NT):
        idx_ref[n_act] = jnp.int32(i)
        n_act = n_act + (jnp.max(sel_past[i * BPT:(i + 1) * BPT, :]) > 0.0).astype(jnp.int32)

    WK = WINDOW + Q
    ws = pl.multiple_of(jnp.maximum(start - WINDOW, 0), Q)
    dist_w = (start - ws) + (lax.broadcasted_iota(jnp.int32, (WK, Q), 1)
                             - lax.broadcasted_iota(jnp.int32, (WK, Q), 0))
    ok_w = (dist_w >= 0) & (dist_w < WINDOW)
    s_w = _dot(jnp.concatenate([kw_ref[pl.ds(ws, WK), :], aux_ref[pl.ds(ws, WK), :]], axis=1), qa)
    s_w = _mask_heads(ok_w, s_w, R, Q)
    e_w = jnp.exp2(s_w - jnp.max(s_w, axis=0, keepdims=True)).astype(MXU_DTYPE)
    acc_w = _dot(vwt_ref[:, pl.ds(ws, WK)], e_w)
    o_win = acc_w[0:HD, :] * (1.0 / acc_w[HD:HD + 1, :])

    def mask_rows(which, b0, valid):
        mrow = jnp.where(valid, (sel_ref[which, pl.ds(b0, BPT), :] - 1.0) * MASK_BIG, -MASK_BIG)
        mrow = jnp.concatenate([mrow] * R, axis=1)
        qa_ref[MR:MR + 2 * BPT, :] = jnp.concatenate([mrow, jnp.zeros_like(mrow)], axis=0).astype(qa_ref.dtype)

    q0 = pl.multiple_of(start, Q)
    mask_rows(0, pl.multiple_of((start // KT) * BPT, BPT), True)
    s_o = _dot(jnp.concatenate([ks_ref[pl.ds(q0, Q), :], aux_ref[pl.ds(q0, Q), :]], axis=1), qa_ref[...])
    ok_o = lax.broadcasted_iota(jnp.int32, (Q, Q), 0) <= lax.broadcasted_iota(jnp.int32, (Q, Q), 1)
    s_o = _mask_heads(ok_o, s_o, R, Q)
    m_o = jnp.max(s_o, axis=0, keepdims=True)
    m_ref[...] = m_o
    acc_ref[...] = _dot(vst_ref[:, pl.ds(q0, Q)], jnp.exp2(s_o - m_o).astype(MXU_DTYPE))

    def tile_of(j):
        return idx_ref[jnp.clip(j, 0, jnp.maximum(n_act - 1, 0))]

    def scores(j, slot):
        i = tile_of(j)
        k0 = pl.multiple_of(i * KT, KT)
        mask_rows(1, pl.multiple_of(i * BPT, BPT), j < n_act)
        s_refs[slot][...] = _dot(jnp.concatenate([ks_ref[pl.ds(k0, KT), :], aux_ref[pl.ds(k0, KT), :]], axis=1),
                                 qa_ref[...])

    def softmax(slot):
        s = s_refs[slot][...]
        m_old = m_ref[...]
        m_new = jnp.maximum(m_old, jnp.max(s, axis=0, keepdims=True))
        p_refs[slot][...] = jnp.exp2(s - m_new).astype(MXU_DTYPE)
        al_refs[slot][...] = jnp.exp2(m_old - m_new)
        m_ref[...] = m_new

    def accumulate(j, slot):
        k0 = pl.multiple_of(tile_of(j) * KT, KT)
        acc_ref[...] = al_refs[slot][...] * acc_ref[...] + _dot(vst_ref[:, pl.ds(k0, KT)], p_refs[slot][...])

    scores(0, 0)
    scores(1, 1)
    softmax(0)

    def pipe(k, c):
        j = 2 * k
        softmax(1)
        accumulate(j, 0)
        scores(j + 2, 0)
        accumulate(j + 1, 1)
        scores(j + 3, 1)
        softmax(0)
        return c

    n_pairs = n_act // 2
    lax.fori_loop(0, n_pairs, pipe, 0)
    accumulate(2 * n_pairs, 0)
    acc_s = acc_ref[...]
    o_sel = acc_s[0:HD, :] * (1.0 / acc_s[HD:HD + 1, :])

    gates = _sigmoid(gt_ref[0])
    for r in range(R):
        cs = slice(r * Q, (r + 1) * Q)
        o_r = (gates[3 * r:3 * r + 1, :] * o_cmp[:, cs] + gates[3 * r + 1:3 * r + 2, :] * o_sel[:, cs]
               + gates[3 * r + 2:3 * r + 3, :] * o_win[:, cs])
        hs = slice(r * HD, (r + 1) * HD)
        o_ref[:, hs] = (o_r.T * _silu(z_ref[:, hs].astype(f32))).astype(o_ref.dtype)


def _aux_table(pos, onehot):
    hi = (pos // SEL_LEN) * SEL_LEN
    lo = pos % SEL_LEN
    col = jnp.arange(AUX_W)[None, :]
    t = jnp.where(col < 3, hi[:, None], jnp.where(col < 6, lo[:, None], 0)).astype(f32)
    if onehot:
        blk = (pos // SEL_LEN) % (SEL_KV_TILE // SEL_LEN)
        t = t + jnp.where(col == MASK_COL0 + blk[:, None], 1.0, 0.0)
    return t.astype(MXU_DTYPE)


def _slope_rows(R, G):
    sl = jnp.exp2(-8.0 * (jnp.arange(NSA_HEADS, dtype=f32) + 1.0) / NSA_HEADS) * LOG2E
    parts = _split3(sl)
    rows = jnp.stack(parts + parts, axis=0).astype(f32)
    rows = jnp.repeat(rows.reshape(6, G, R).transpose(1, 0, 2), NSA_Q, axis=2)
    pad = jnp.zeros((G, MASK_COL0 - 6, R * NSA_Q), f32)
    return jnp.concatenate([rows, pad], axis=1).astype(MXU_DTYPE)


def _overlap_table(NS, NC):
    jj = jnp.arange(NS)[:, None] * SEL_LEN
    nn = jnp.arange(NC)[None, :] * CMP_STRIDE
    return jnp.where((nn < jj + SEL_LEN) & (nn + (CMP_LEN - 1) >= jj), 1.0, 0.0).astype(MXU_DTYPE)


def _nsa_attention(h, cq, cz, bw, kvh, k_cmp, v_cmp_t, gates_t, B, S):
    T = B * S
    G, R = NSA_GROUPS, NSA_REP
    HD = bw // NSA_HEADS
    assert cq % (R * HD) == 0 and cz % (R * HD) == 0
    qc0, zc0 = cq // (R * HD), cz // (R * HD)
    Q = NSA_Q
    NQ = S // Q
    NC = S // CMP_STRIDE
    NS = S // SEL_LEN
    RQ = R * Q
    KT = SEL_KV_TILE
    assert S % KT == 0 and S % Q == 0 and KT % Q == 0 and S >= WINDOW + Q
    assert S + CMP_LEN <= SEL_LEN * 256 and 2 * (KT // SEL_LEN) <= AUX_W - MASK_COL0
    aux_s = _aux_table(jnp.arange(S), True)
    aux_c = _aux_table(jnp.arange(NC) * CMP_STRIDE + (CMP_LEN - 1), False)
    HA = HD + V_AUG
    slab = lambda k: pl.BlockSpec((1, S, HD), lambda b, g, i: (k * G + g, b, 0))
    return pl.pallas_call(
        functools.partial(_nsa_kernel, S=S),
        out_shape=jax.ShapeDtypeStruct((T, bw), MXU_DTYPE),
        grid=(B, G, NQ),
        in_specs=[
            pl.BlockSpec((Q, R * HD), lambda b, g, i: (b * NQ + i, qc0 + g)),
            pl.BlockSpec((1, 1, NC, HD), lambda b, g, i: (b, g, 0, 0)),
            pl.BlockSpec((NC, AUX_W), lambda b, g, i: (0, 0)),
            pl.BlockSpec((1, 1, HD, NC), lambda b, g, i: (b, g, 0, 0)),
            pl.BlockSpec((NS, NC), lambda b, g, i: (0, 0)),
            slab(2),
            pl.BlockSpec((S, AUX_W), lambda b, g, i: (0, 0)),
            slab(3),
            slab(4),
            slab(5),
            pl.BlockSpec((1, 16, Q), lambda b, g, i: (g, 0, b * NQ + i)),
            pl.BlockSpec((1, MASK_COL0, RQ), lambda b, g, i: (g, 0, 0)),
            pl.BlockSpec((Q, R * HD), lambda b, g, i: (b * NQ + i, zc0 + g)),
        ],
        out_specs=pl.BlockSpec((Q, R * HD), lambda b, g, i: (b * NQ + i, g)),
        scratch_shapes=[
            pltpu.VMEM((HD + AUX_W, RQ), MXU_DTYPE),
            pltpu.VMEM((2, NS, Q), f32),
            pltpu.VMEM((1, RQ), f32),
            pltpu.VMEM((HA, RQ), f32),
            pltpu.VMEM((KT, RQ), f32), pltpu.VMEM((KT, RQ), f32),
            pltpu.VMEM((KT, RQ), MXU_DTYPE), pltpu.VMEM((KT, RQ), MXU_DTYPE),
            pltpu.VMEM((1, RQ), f32), pltpu.VMEM((1, RQ), f32),
            pltpu.VMEM((HA, S), MXU_DTYPE), pltpu.VMEM((HA, S), MXU_DTYPE),
            pltpu.SMEM((S // KT + 1,), jnp.int32),
        ],
        compiler_params=_params("parallel", "parallel", "arbitrary"),
        name="nsa_attention",
    )(h, k_cmp, aux_c, v_cmp_t, _overlap_table(NS, NC), kvh, aux_s, kvh, kvh, kvh, gates_t, _slope_rows(R, G), h)


def _mem_kernel(q_ref, z_ref, kv_ref, o_ref):
    hw = q_ref.shape[1] // MEM_HEADS
    bw = q_ref.shape[1]
    for h in range(MEM_HEADS):
        cs = slice(h * hw, (h + 1) * hw)
        s = _dot_nt(q_ref[:, cs], kv_ref[:, cs])
        m = jnp.max(s, axis=-1, keepdims=True)
        e = jnp.exp(s - m)
        p = e * (1.0 / jnp.sum(e, axis=-1, keepdims=True))
        o = _dot(p.astype(MXU_DTYPE), kv_ref[:, bw + h * hw:bw + (h + 1) * hw])
        o_ref[:, cs] = (o * _silu(z_ref[:, cs].astype(f32))).astype(o_ref.dtype)


def _mem_attention(h_mem, c0, bw, kv, B, S):
    T = B * S
    M = kv.shape[0] // B
    tq = min(512, S)
    nb = S // tq
    assert c0 % bw == 0
    qc = c0 // bw
    return pl.pallas_call(
        _mem_kernel,
        out_shape=jax.ShapeDtypeStruct((T, bw), MXU_DTYPE),
        grid=(B, nb),
        in_specs=[
            pl.BlockSpec((tq, bw), lambda b, i: (b * nb + i, qc)),
            pl.BlockSpec((tq, bw), lambda b, i: (b * nb + i, qc + 1)),
            pl.BlockSpec((M, 2 * bw), lambda b, i: (b, 0)),
        ],
        out_specs=pl.BlockSpec((tq, bw), lambda b, i: (b * nb + i, 0)),
        compiler_params=_params("parallel", "parallel"),
        name="mem_attention",
    )(h_mem, h_mem, kv)


def _merge_kernel(og_ref, on_ref, om_ref, wg_ref, wn_ref, wm_ref, ag_ref, an_ref, am_ref, o_ref):
    y = ag_ref[...].astype(f32) * _dot(og_ref[...], wg_ref[...])
    y = y + an_ref[...].astype(f32) * _dot(on_ref[...], wn_ref[...])
    y = y + am_ref[...].astype(f32) * _dot(om_ref[...], wm_ref[...])
    o_ref[...] = y.astype(o_ref.dtype)


def _merge(o_gla, o_nsa, o_mem, w_g, w_n, w_m, a, c0):
    T, bw = o_gla.shape
    D = w_g.shape[1]
    tm, tn = min(512, T), min(1024, D)
    nj = D // tn
    assert c0 % tn == 0
    osp = pl.BlockSpec((tm, bw), lambda j, i: (i, 0))
    wsp = pl.BlockSpec((bw, tn), lambda j, i: (0, j))
    asp = lambda c: pl.BlockSpec((tm, tn), lambda j, i: (i, c0 // tn + c * nj + j))
    return pl.pallas_call(
        _merge_kernel,
        out_shape=jax.ShapeDtypeStruct((T, D), MXU_DTYPE),
        grid=(nj, T // tm),
        in_specs=[osp, osp, osp, wsp, wsp, wsp, asp(0), asp(1), asp(2)],
        out_specs=pl.BlockSpec((tm, tn), lambda j, i: (i, j)),
        compiler_params=_params("parallel", "parallel"),
        name="branch_merge",
    )(o_gla, o_nsa, o_mem, w_g, w_n, w_m, a, a, a)


def _out_ln_kernel(m_ref, w_ref, x_ref, g_ref, b_ref, o_ref, *, alpha):
    o_ref[...] = alpha * x_ref[...] + _dot(m_ref[...], w_ref[...])

    def ln_rows(c, _):
        rows = pl.ds(pl.multiple_of(c * LN_ROWS, LN_ROWS), LN_ROWS)
        z = o_ref[rows, :]
        mu = jnp.mean(z, axis=-1, keepdims=True)
        zc = z - mu
        var = jnp.mean(zc * zc, axis=-1, keepdims=True)
        o_ref[rows, :] = zc * lax.rsqrt(var + LN_EPS) * g_ref[...] + b_ref[...]
        return 0

    lax.fori_loop(0, o_ref.shape[0] // LN_ROWS, ln_rows, 0)


def _out_ln(merged, w_out, x2, ln_g, ln_b, alpha):
    T, D = x2.shape
    tm = min(OUT_ROWS, T)
    return pl.pallas_call(
        functools.partial(_out_ln_kernel, alpha=alpha),
        out_shape=jax.ShapeDtypeStruct((T, D), x2.dtype),
        grid=(T // tm,),
        in_specs=[
            pl.BlockSpec((tm, D), lambda i: (i, 0)),
            pl.BlockSpec((D, D), lambda i: (0, 0), pipeline_mode=pl.Buffered(1)),
            pl.BlockSpec((tm, D), lambda i: (i, 0)),
            pl.BlockSpec((1, D), lambda i: (0, 0)),
            pl.BlockSpec((1, D), lambda i: (0, 0)),
        ],
        out_specs=pl.BlockSpec((tm, D), lambda i: (i, 0)),
        compiler_params=_params("parallel"),
        name="out_proj_layernorm",
    )(merged, w_out, x2, ln_g.reshape(1, D), ln_b.reshape(1, D))


def _layer(x, mem, w_in, b_merge, gla_w_a2, gla_b_a, gla_norm_g, nsa_pe_k, nsa_pe_v, nsa_wk1, nsa_wk2,
           nsa_wv1, nsa_wv2, w_mem_kv, w_br_gla, w_br_nsa, w_br_mem, w_out, ln_g, ln_b, depth):
    B, S, D = x.shape
    T = B * S
    bw = D // 2
    gk = bw // 2
    G, R = NSA_GROUPS, NSA_REP
    HD = bw // NSA_HEADS
    kvw = G * HD
    cdt = MXU_DTYPE

    o_ga = 2 * gk + 2 * bw
    o_nq = o_ga + GLA_LOWRANK
    o_nbg = o_nq + bw + 6 * kvw + bw
    o_mq = o_nbg + 3 * NSA_HEADS
    o_mrg = o_mq + 2 * bw
    assert w_in.shape[1] == o_mrg + N_BRANCH * D

    x2 = x.reshape(T, D)
    ones = lambda n: jnp.ones((n,), f32)

    w_t = jnp.swapaxes(w_in, 0, 1)
    h_small, xb = _project_small(x2, w_t, o_ga, GLA_LOWRANK, o_nbg, 3 * NSA_HEADS)
    o_nkv, o_nz = o_nq + bw, o_nq + bw + 6 * kvw
    w_all = _repack(w_t, [(0, o_ga), (o_mq, 2 * bw), (o_nq, bw), (o_nz, bw), (o_nkv, 6 * kvw), (o_mrg, N_BRANCH * D)])
    c_gla, c_mem = 0, o_ga
    c_nq = c_mem + 2 * bw
    c_nz = c_nq + bw
    c_nkv = c_nz + bw
    c_mrg = c_nkv + 6 * kvw
    dk, mhd = gk // GLA_HEADS, bw // MEM_HEADS
    scale = jnp.concatenate([jnp.full((gk,), dk ** -0.5, f32), ones(o_ga - gk),
                             jnp.full((bw,), mhd ** -0.5, f32), ones(bw),
                             jnp.full((bw,), HD ** -0.5 * LOG2E, f32), ones(bw)])
    h = _project(xb, w_all, scale, cdt, n=c_nkv, nt=True, name="proj_in")
    kvh = _project(xb, w_all, ones(6 * kvw), cdt, c0=c_nkv, n=6 * kvw, nt=True, grouped=True, name="proj_nsa_kv")
    a = _project(xb, w_all, b_merge, cdt, c0=c_mrg, gate=True, nt=True, name="proj_merge_gates")

    wa_pad = jnp.concatenate([gla_w_a2, jnp.zeros((SMALL_W - GLA_LOWRANK, gk), f32)], axis=0)
    o_gla = _gla(h, c_gla, gk, h_small, wa_pad, gla_b_a, gla_norm_g, B, S)

    NC = S // CMP_STRIDE
    blocks = lambda k: kvh[k * G:(k + 1) * G].reshape(G, B, NC, CMP_STRIDE * HD)
    k_cmp = _compress(blocks(0), nsa_pe_k, nsa_wk1, nsa_wk2, False)
    v_cmp_t = _compress(blocks(1), nsa_pe_v, nsa_wv1, nsa_wv2, True)
    gl = h_small[:, GLA_LOWRANK:GLA_LOWRANK + 3 * NSA_HEADS].reshape(T, G, 3 * R).transpose(1, 2, 0)
    gates_t = jnp.concatenate([gl, jnp.zeros((G, 16 - 3 * R, T), f32)], axis=1)
    o_nsa = _nsa_attention(h, c_nq, c_nz, bw, kvh, k_cmp, v_cmp_t, gates_t, B, S)

    M = mem.shape[1]
    kv = _project(mem.reshape(B * M, D).astype(cdt), w_mem_kv.astype(cdt), ones(2 * bw), cdt, name="proj_mem_kv")
    o_mem = _mem_attention(h, c_mem, bw, kv, B, S)

    merged = _merge(o_gla, o_nsa, o_mem, w_br_gla.astype(cdt), w_br_nsa.astype(cdt), w_br_mem.astype(cdt), a, 0)
    alpha = (2 * depth) ** 0.25
    return _out_ln(merged, w_out.astype(cdt), x2, ln_g, ln_b, alpha).reshape(B, S, D)


def kernel(x, mem, w_in, b_merge, gla_w_a2, gla_b_a, gla_norm_g, nsa_pe_k, nsa_pe_v, nsa_wk1, nsa_wk2, nsa_wv1, nsa_wv2, w_mem_kv, w_br_gla, w_br_nsa, w_br_mem, w_out, ln_g, ln_b):
    depth = w_in.shape[0]
    for l in range(depth):
        x = _layer(x, mem, w_in[l], b_merge[l], gla_w_a2[l], gla_b_a[l], gla_norm_g[l], nsa_pe_k[l], nsa_pe_v[l],
                   nsa_wk1[l], nsa_wk2[l], nsa_wv1[l], nsa_wv2[l], w_mem_kv[l], w_br_gla[l], w_br_nsa[l],
                   w_br_mem[l], w_out[l], ln_g[l], ln_b[l], depth)
    return x
```

```python
import functools

import jax
import jax.numpy as jnp
from jax import lax
from jax.experimental import pallas as pl
from jax.experimental.pallas import tpu as pltpu

N_BRANCH = 3
GLA_HEADS = 4
GLA_LOWRANK = 16
GLA_TAU = 16.0
GLA_CHUNK = 64
NSA_HEADS = 16
NSA_GROUPS = 4
NSA_REP = NSA_HEADS // NSA_GROUPS
CMP_LEN = 32
CMP_STRIDE = 16
SEL_LEN = 64
SEL_TOPK = 16
WINDOW = 512
FORCE_SCORE = 1e4
MEM_HEADS = 4
LN_EPS = 1e-5
RMS_EPS = 1e-6
NEG_INF = -1e30
LOG2E = 1.4426950408889634

LANES = 128
VMEM_LIMIT_BYTES = 56 * 1024 * 1024
MXU_DTYPE = jnp.bfloat16

SEL_KV_TILE = 512
NSA_Q = 256
GLA_STEP_CHUNKS = 4
LN_ROWS = 64
OUT_ROWS = 256
SMALL_W = LANES
AUX_W = LANES
MASK_COL0 = 16
V_AUG = 16
MASK_BIG = -NEG_INF
ROW_ALIGN = 16
GATE_ROWS = 16

f32 = jnp.float32


def _dot(a, b):
    return jnp.dot(a, b, preferred_element_type=f32)


def _dot_nt(a, b):
    return lax.dot_general(a, b, (((1,), (1,)), ((), ())), preferred_element_type=f32)


def _dot_tn(a, b):
    return lax.dot_general(a, b, (((0,), (0,)), ((), ())), preferred_element_type=f32)


def _sigmoid(x):
    return 0.5 * (jnp.tanh(0.5 * x) + 1.0)


def _silu(x):
    return x * _sigmoid(x)


def _log_sigmoid(x):
    return -(jnp.maximum(-x, 0.0) + jnp.log(1.0 + jnp.exp(-jnp.abs(x))))


def _split2(x):
    hi = x.astype(MXU_DTYPE)
    lo = (x - hi.astype(f32)).astype(MXU_DTYPE)
    return hi, lo


def _split3(x):
    hi = x.astype(MXU_DTYPE)
    r1 = x - hi.astype(f32)
    mid = r1.astype(MXU_DTYPE)
    lo = (r1 - mid.astype(f32)).astype(MXU_DTYPE)
    return hi, mid, lo


def _params(*sem):
    return pltpu.CompilerParams(dimension_semantics=sem, vmem_limit_bytes=VMEM_LIMIT_BYTES)


def _proj_kernel(x_ref, w_ref, r_ref, o_ref, *, gate, nt, grouped):
    acc = _dot_nt(x_ref[...], w_ref[...]) if nt else _dot(x_ref[...], w_ref[...])
    res = (_sigmoid(acc + r_ref[...]) if gate else acc * r_ref[...]).astype(o_ref.dtype)
    if grouped:
        for c in range(o_ref.shape[0]):
            o_ref[c] = res[:, c * LANES:(c + 1) * LANES]
    else:
        o_ref[...] = res


def _project(x, w, row, out_dtype, c0=0, n=None, gate=False, nt=False, grouped=False, name="proj"):
    M, K = x.shape
    n = (w.shape[0] if nt else w.shape[1]) - c0 if n is None else n
    bm = min(1024, M)
    bn = min(1024, n)
    assert M % bm == 0 and n % bn == 0 and c0 % bn == 0
    wspec = (pl.BlockSpec((bn, K), lambda j, i: (c0 // bn + j, 0)) if nt
             else pl.BlockSpec((K, bn), lambda j, i: (0, c0 // bn + j)))
    if grouped:
        out_shape = jax.ShapeDtypeStruct((n // LANES, M, LANES), out_dtype)
        ospec = pl.BlockSpec((bn // LANES, bm, LANES), lambda j, i: (j, i, 0))
    else:
        out_shape = jax.ShapeDtypeStruct((M, n), out_dtype)
        ospec = pl.BlockSpec((bm, bn), lambda j, i: (i, j))
    return pl.pallas_call(
        functools.partial(_proj_kernel, gate=gate, nt=nt, grouped=grouped),
        out_shape=out_shape,
        grid=(n // bn, M // bm),
        in_specs=[pl.BlockSpec((bm, K), lambda j, i: (i, 0)), wspec, pl.BlockSpec((1, bn), lambda j, i: (0, j))],
        out_specs=ospec,
        compiler_params=_params("parallel", "parallel"),
        name=name,
    )(x, w, row.reshape(1, n).astype(f32))


def _row_window(rows, width, start_unit):
    return pl.BlockSpec((pl.Element(rows), pl.Element(width)), lambda *g: (start_unit(*g) * ROW_ALIGN, 0))


def _proj_small_kernel(x_ref, wa_ref, wb_ref, o_ref, xb_ref):
    xb = x_ref[...].astype(xb_ref.dtype)
    xb_ref[...] = xb
    pad = jnp.zeros((LANES - wa_ref.shape[0] - wb_ref.shape[0], wa_ref.shape[1]), f32)
    w = jnp.concatenate([wa_ref[...], wb_ref[...], pad], axis=0)
    o_ref[...] = _dot_nt(xb, w.astype(xb.dtype))


def _project_small(x, w_t, ra, na, rb, nb):
    M, K = x.shape
    bm = min(512, M)
    assert ra % ROW_ALIGN == 0 and rb % ROW_ALIGN == 0 and na % 8 == 0 and nb % 8 == 0 and na + nb <= LANES
    return pl.pallas_call(
        _proj_small_kernel,
        out_shape=(jax.ShapeDtypeStruct((M, LANES), f32), jax.ShapeDtypeStruct((M, K), MXU_DTYPE)),
        grid=(M // bm,),
        in_specs=[
            pl.BlockSpec((bm, K), lambda i: (i, 0)),
            _row_window(na, K, lambda i: ra // ROW_ALIGN),
            _row_window(nb, K, lambda i: rb // ROW_ALIGN),
        ],
        out_specs=(pl.BlockSpec((bm, LANES), lambda i: (i, 0)), pl.BlockSpec((bm, K), lambda i: (i, 0))),
        compiler_params=_params("parallel"),
        name="proj_small",
    )(x, w_t, w_t)


def _repack_kernel(w_ref, o_ref):
    o_ref[...] = w_ref[...].astype(o_ref.dtype)


def _repack(w_t, segments, bn=1024):
    N, K = w_t.shape
    segs, blk = [], 0
    for src, n in segments:
        assert n % bn == 0 and src % ROW_ALIGN == 0 and src + n <= N
        segs.append((blk, blk + n // bn, src // ROW_ALIGN - blk * (bn // ROW_ALIGN)))
        blk += n // bn

    def start_unit(j):
        s = jnp.int32(0)
        for lo, hi, off in segs:
            s = jnp.where((j >= lo) & (j < hi), j * (bn // ROW_ALIGN) + off, s)
        return s

    return pl.pallas_call(
        _repack_kernel,
        out_shape=jax.ShapeDtypeStruct((blk * bn, K), MXU_DTYPE),
        grid=(blk,),
        in_specs=[_row_window(bn, K, start_unit)],
        out_specs=pl.BlockSpec((bn, K), lambda j: (j, 0)),
        compiler_params=_params("parallel"),
        name="repack_w_in",
    )(w_t)


def _gla_kernel(q_ref, k_ref, v_ref, z_ref, ga_ref, wa_ref, ba_ref, ng_ref, o_ref, st_ref, *, dk, dv):
    C = GLA_CHUNK

    @pl.when(pl.program_id(1) == 0)
    def _():
        st_ref[...] = jnp.zeros_like(st_ref)

    row = lax.broadcasted_iota(jnp.int32, (C, C), 0)
    col = lax.broadcasted_iota(jnp.int32, (C, C), 1)
    tril = row >= col
    ltri = jnp.where(tril, 1.0, 0.0).astype(MXU_DTYPE)
    wa_hi, wa_lo = _split2(wa_ref[...])
    for c in range(GLA_STEP_CHUNKS):
        rows = slice(c * C, (c + 1) * C)
        ga_hi, ga_lo = _split2(ga_ref[rows, :])
        zz = _dot(ga_hi, wa_hi) + _dot(ga_lo, wa_hi) + _dot(ga_hi, wa_lo) + ba_ref[...]
        la = _log_sigmoid(zz) * (1.0 / GLA_TAU)
        la_hi, la_mid, la_lo = _split3(la)
        bcum = _dot(ltri, la_hi) + _dot(ltri, la_mid) + _dot(ltri, la_lo)
        for h in range(GLA_HEADS):
            kc = slice(h * dk, (h + 1) * dk)
            vc = slice(h * dv, (h + 1) * dv)
            b = bcum[:, kc]
            bl = b[C - 1:C, :]
            qh = q_ref[rows, kc].astype(f32)
            kh = k_ref[rows, kc].astype(f32)
            vh = v_ref[rows, vc]
            q_d = (qh * jnp.exp(b)).astype(MXU_DTYPE)
            k_d = (kh * jnp.exp(-b)).astype(MXU_DTYPE)
            k_e = (kh * jnp.exp(bl - b)).astype(MXU_DTYPE)
            att = jnp.where(tril, _dot_nt(q_d, k_d), 0.0)
            st = st_ref[h]
            o = _dot(att.astype(MXU_DTYPE), vh) + _dot_nt(q_d, st.astype(MXU_DTYPE))
            st_ref[h] = st * jnp.exp(bl) + _dot_tn(vh, k_e)
            ms = jnp.mean(o * o, axis=-1, keepdims=True)
            on = o * lax.rsqrt(ms + RMS_EPS) * ng_ref[...]
            zg = z_ref[rows, vc].astype(f32)
            o_ref[rows, vc] = (on * _silu(zg)).astype(o_ref.dtype)


def _gla(h, c0, kw, h_small, wa_pad, b_a, norm_g, B, S):
    T = B * S
    dk, dv = kw // GLA_HEADS, 2 * kw // GLA_HEADS
    cs = GLA_STEP_CHUNKS * GLA_CHUNK
    nb = S // cs
    assert S % cs == 0 and c0 % (2 * kw) == 0
    h_gla = h
    rowmap = lambda col: (lambda b, i: (b * nb + i, col))
    qc, vc = c0 // kw, c0 // (2 * kw)
    return pl.pallas_call(
        functools.partial(_gla_kernel, dk=dk, dv=dv),
        out_shape=jax.ShapeDtypeStruct((T, 2 * kw), MXU_DTYPE),
        grid=(B, nb),
        in_specs=[
            pl.BlockSpec((cs, kw), rowmap(qc)),
            pl.BlockSpec((cs, kw), rowmap(qc + 1)),
            pl.BlockSpec((cs, 2 * kw), rowmap(vc + 1)),
            pl.BlockSpec((cs, 2 * kw), rowmap(vc + 2)),
            pl.BlockSpec((cs, SMALL_W), rowmap(0)),
            pl.BlockSpec((SMALL_W, kw), lambda b, i: (0, 0)),
            pl.BlockSpec((1, kw), lambda b, i: (0, 0)),
            pl.BlockSpec((1, dv), lambda b, i: (0, 0)),
        ],
        out_specs=pl.BlockSpec((cs, 2 * kw), rowmap(0)),
        scratch_shapes=[pltpu.VMEM((GLA_HEADS, dv, dk), f32)],
        compiler_params=_params("parallel", "arbitrary"),
        name="gla",
    )(h_gla, h_gla, h_gla, h_gla, h_small, wa_pad, b_a.reshape(1, kw), norm_g.reshape(1, dv))


def _compress_kernel(x_ref, pe_ref, w1_ref, w2_ref, o_ref, *, transpose_out):
    x = x_ref[0, 0].astype(f32)
    half = x.shape[1]
    xa = (x + pe_ref[0:1, :]).astype(MXU_DTYPE)
    xb = (x + pe_ref[1:2, :]).astype(MXU_DTYPE)
    ya = _dot(xa, w1_ref[0:half, :])
    yb = _dot(xb, w1_ref[half:2 * half, :])
    nc = x.shape[0]
    pre = ya + pltpu.roll(yb, nc - 1, 0)
    out = _dot(_silu(pre).astype(MXU_DTYPE), w2_ref[...])
    o_ref[0, 0] = (out.T if transpose_out else out).astype(o_ref.dtype)


def _compress(xblk, pe, w1, w2, transpose_out):
    G, B, NC, W = xblk.shape
    hd = w2.shape[0]
    oshape = (hd, NC) if transpose_out else (NC, hd)
    return pl.pallas_call(
        functools.partial(_compress_kernel, transpose_out=transpose_out),
        out_shape=jax.ShapeDtypeStruct((B, G) + oshape, MXU_DTYPE),
        grid=(B, G),
        in_specs=[
            pl.BlockSpec((1, 1, NC, W), lambda b, g: (g, b, 0, 0)),
            pl.BlockSpec((2, W), lambda b, g: (0, 0)),
            pl.BlockSpec((2 * W, hd), lambda b, g: (0, 0)),
            pl.BlockSpec((hd, hd), lambda b, g: (0, 0)),
        ],
        out_specs=pl.BlockSpec((1, 1) + oshape, lambda b, g: (b, g, 0, 0)),
        compiler_params=_params("parallel", "parallel"),
        name="nsa_compress",
    )(xblk, pe.reshape(2, W).astype(f32), w1.astype(MXU_DTYPE), w2.astype(MXU_DTYPE))


def _mask_heads(ok, s, R, Q, fill=NEG_INF):
    return jnp.concatenate([jnp.where(ok, s[:, r * Q:(r + 1) * Q], fill) for r in range(R)], axis=1)


def _softmax2_cols(s, ok, R, Q):
    s = _mask_heads(ok, s, R, Q)
    m = jnp.max(s, axis=0, keepdims=True)
    e = _mask_heads(ok, jnp.exp2(s - m), R, Q, 0.0)
    den = jnp.sum(e, axis=0, keepdims=True)
    return e * jnp.where(den > 0.0, 1.0 / den, 0.0)


def _nsa_kernel(q_ref, kc_ref, auxc_ref, vct_ref, ov_ref, ks3_ref, aux_ref, vs_ref, kw3_ref, vw_ref, gt_ref,
                sl_ref, z_ref, o_ref, qa_ref, sel_ref, m_ref, acc_ref, s0_ref, s1_ref, p0_ref, p1_ref, al0_ref,
                al1_ref, vst_ref, vwt_ref, idx_ref, *, S):
    Q, R = NSA_Q, NSA_REP
    s_refs, p_refs, al_refs = (s0_ref, s1_ref), (p0_ref, p1_ref), (al0_ref, al1_ref)
    ks_ref, kw_ref = ks3_ref.at[0], kw3_ref.at[0]
    HD = q_ref.shape[1] // R
    RQ = R * Q
    NC = S // CMP_STRIDE
    NS = S // SEL_LEN
    KT = SEL_KV_TILE
    BPT = KT // SEL_LEN
    NT = S // KT
    MR = HD + MASK_COL0
    qb = pl.program_id(2)
    start = qb * Q

    @pl.when(qb == 0)
    def _():
        tail = jnp.where(lax.broadcasted_iota(jnp.int32, (V_AUG, S), 0) == 0, 1.0, 0.0).astype(vst_ref.dtype)
        vst_ref[HD:, :] = tail
        vwt_ref[HD:, :] = tail

        def fill(c, carry):
            r = pl.ds(pl.multiple_of(c * LANES, LANES), LANES)
            vst_ref[0:HD, r] = vs_ref[0, r, :].T
            vwt_ref[0:HD, r] = vw_ref[0, r, :].T
            return carry

        lax.fori_loop(0, S // LANES, fill, 0)

    q = q_ref[...]
    qa_ref[0:HD, :] = jnp.concatenate(
        [q[c * LANES:(c + 1) * LANES, r * HD:(r + 1) * HD].T for r in range(R) for c in range(Q // LANES)], axis=1)
    qa_ref[HD:MR, :] = sl_ref[0]
    qa_ref[MR:, :] = jnp.zeros((AUX_W - MASK_COL0, RQ), qa_ref.dtype)
    qa = qa_ref[...]

    ok_c = (lax.broadcasted_iota(jnp.int32, (NC, Q), 0) * CMP_STRIDE + (CMP_LEN - 1)
            <= start + lax.broadcasted_iota(jnp.int32, (NC, Q), 1))
    s_c = _dot(jnp.concatenate([kc_ref[0, 0], auxc_ref[...]], axis=1), qa)
    p_c = _softmax2_cols(s_c, ok_c, R, Q)
    o_cmp = _dot(vct_ref[0, 0], p_c.astype(MXU_DTYPE))

    p_sum = p_c[:, 0:Q]
    for r in range(1, R):
        p_sum = p_sum + p_c[:, r * Q:(r + 1) * Q]
    ov = ov_ref[...]
    ps_hi, ps_mid, ps_lo = _split3(p_sum)
    imp = _dot(ov, ps_hi) + _dot(ov, ps_mid) + _dot(ov, ps_lo)

    blk = lax.broadcasted_iota(jnp.int32, (NS, Q), 0)
    tq1 = start + lax.broadcasted_iota(jnp.int32, (NS, Q), 1)
    cur = jnp.right_shift(tq1, SEL_LEN.bit_length() - 1)
    forced = (blk == 0) | (blk == cur) | (blk == cur - 1)
    score = jnp.where(forced, FORCE_SCORE, jnp.where(blk * SEL_LEN <= tq1, imp, -1.0))
    blk_f = blk.astype(f32)
    for _ in range(min(SEL_TOPK, NS)):
        m = jnp.max(score, axis=0, keepdims=True)
        first = jnp.min(jnp.where(score == m, blk_f, float(NS)), axis=0, keepdims=True)
        score = jnp.where(blk_f == first, -jnp.inf, score)
    sel = jnp.where(score == -jnp.inf, 1.0, 0.0)
    sel_past = jnp.where(blk * SEL_LEN < start, sel, 0.0)
    sel_ref[0] = sel
    sel_ref[1] = sel_past
    n_act = jnp.int32(0)
    for i in range(NT):
        idx_ref[n_act] = jnp.int32(i)
        n_act = n_act + (jnp.max(sel_past[i * BPT:(i + 1) * BPT, :]) > 0.0).astype(jnp.int32)

    WK = WINDOW + Q
    ws = pl.multiple_of(jnp.maximum(start - WINDOW, 0), Q)
    dist_w = (start - ws) + (lax.broadcasted_iota(jnp.int32, (WK, Q), 1)
                             - lax.broadcasted_iota(jnp.int32, (WK, Q), 0))
    ok_w = (dist_w >= 0) & (dist_w < WINDOW)
    s_w = _dot(jnp.concatenate([kw_ref[pl.ds(ws, WK), :], aux_ref[pl.ds(ws, WK), :]], axis=1), qa)
    s_w = _mask_heads(ok_w, s_w, R, Q)
    e_w = jnp.exp2(s_w - jnp.max(s_w, axis=0, keepdims=True)).astype(MXU_DTYPE)
    acc_w = _dot(vwt_ref[:, pl.ds(ws, WK)], e_w)
    o_win = acc_w[0:HD, :] * (1.0 / acc_w[HD:HD + 1, :])

    def mask_rows(which, b0, valid):
        mrow = jnp.where(valid, (sel_ref[which, pl.ds(b0, BPT), :] - 1.0) * MASK_BIG, -MASK_BIG)
        mrow = jnp.concatenate([mrow] * R, axis=1)
        qa_ref[MR:MR + 2 * BPT, :] = jnp.concatenate([mrow, jnp.zeros_like(mrow)], axis=0).astype(qa_ref.dtype)

    q0 = pl.multiple_of(start, Q)
    mask_rows(0, pl.multiple_of((start // KT) * BPT, BPT), True)
    s_o = _dot(jnp.concatenate([ks_ref[pl.ds(q0, Q), :], aux_ref[pl.ds(q0, Q), :]], axis=1), qa_ref[...])
    ok_o = lax.broadcasted_iota(jnp.int32, (Q, Q), 0) <= lax.broadcasted_iota(jnp.int32, (Q, Q), 1)
    s_o = _mask_heads(ok_o, s_o, R, Q)
    m_o = jnp.max(s_o, axis=0, keepdims=True)
    m_ref[...] = m_o
    acc_ref[...] = _dot(vst_ref[:, pl.ds(q0, Q)], jnp.exp2(s_o - m_o).astype(MXU_DTYPE))

    def tile_of(j):
        return idx_ref[jnp.clip(j, 0, jnp.maximum(n_act - 1, 0))]

    def scores(j, slot):
        i = tile_of(j)
        k0 = pl.multiple_of(i * KT, KT)
        mask_rows(1, pl.multiple_of(i * BPT, BPT), j < n_act)
        s_refs[slot][...] = _dot(jnp.concatenate([ks_ref[pl.ds(k0, KT), :], aux_ref[pl.ds(k0, KT), :]], axis=1),
                                 qa_ref[...])

    def softmax(slot):
        s = s_refs[slot][...]
        m_old = m_ref[...]
        m_new = jnp.maximum(m_old, jnp.max(s, axis=0, keepdims=True))
        p_refs[slot][...] = jnp.exp2(s - m_new).astype(MXU_DTYPE)
        al_refs[slot][...] = jnp.exp2(m_old - m_new)
        m_ref[...] = m_new

    def accumulate(j, slot):
        k0 = pl.multiple_of(tile_of(j) * KT, KT)
        acc_ref[...] = al_refs[slot][...] * acc_ref[...] + _dot(vst_ref[:, pl.ds(k0, KT)], p_refs[slot][...])

    scores(0, 0)
    scores(1, 1)
    softmax(0)

    def pipe(k, c):
        j = 2 * k
        softmax(1)
        accumulate(j, 0)
        scores(j + 2, 0)
        accumulate(j + 1, 1)
        scores(j + 3, 1)
        softmax(0)
        return c

    n_pairs = n_act // 2
    lax.fori_loop(0, n_pairs, pipe, 0)
    accumulate(2 * n_pairs, 0)
    acc_s = acc_ref[...]
    o_sel = acc_s[0:HD, :] * (1.0 / acc_s[HD:HD + 1, :])

    gates = _sigmoid(gt_ref[0])
    for r in range(R):
        cs = slice(r * Q, (r + 1) * Q)
        o_r = (gates[3 * r:3 * r + 1, :] * o_cmp[:, cs] + gates[3 * r + 1:3 * r + 2, :] * o_sel[:, cs]
               + gates[3 * r + 2:3 * r + 3, :] * o_win[:, cs])
        hs = slice(r * HD, (r + 1) * HD)
        o_ref[:, hs] = (o_r.T * _silu(z_ref[:, hs].astype(f32))).astype(o_ref.dtype)


def _aux_table(pos, onehot):
    hi = (pos // SEL_LEN) * SEL_LEN
    lo = pos % SEL_LEN
    col = jnp.arange(AUX_W)[None, :]
    t = jnp.where(col < 3, hi[:, None], jnp.where(col < 6, lo[:, None], 0)).astype(f32)
    if onehot:
        blk = (pos // SEL_LEN) % (SEL_KV_TILE // SEL_LEN)
        t = t + jnp.where(col == MASK_COL0 + blk[:, None], 1.0, 0.0)
    return t.astype(MXU_DTYPE)


def _slope_rows(R, G):
    sl = jnp.exp2(-8.0 * (jnp.arange(NSA_HEADS, dtype=f32) + 1.0) / NSA_HEADS) * LOG2E
    parts = _split3(sl)
    rows = jnp.stack(parts + parts, axis=0).astype(f32)
    rows = jnp.repeat(rows.reshape(6, G, R).transpose(1, 0, 2), NSA_Q, axis=2)
    pad = jnp.zeros((G, MASK_COL0 - 6, R * NSA_Q), f32)
    return jnp.concatenate([rows, pad], axis=1).astype(MXU_DTYPE)


def _overlap_table(NS, NC):
    jj = jnp.arange(NS)[:, None] * SEL_LEN
    nn = jnp.arange(NC)[None, :] * CMP_STRIDE
    return jnp.where((nn < jj + SEL_LEN) & (nn + (CMP_LEN - 1) >= jj), 1.0, 0.0).astype(MXU_DTYPE)


def _nsa_attention(h, cq, cz, bw, kvh, k_cmp, v_cmp_t, gates_t, B, S):
    T = B * S
    G, R = NSA_GROUPS, NSA_REP
    HD = bw // NSA_HEADS
    assert cq % (R * HD) == 0 and cz % (R * HD) == 0
    qc0, zc0 = cq // (R * HD), cz // (R * HD)
    Q = NSA_Q
    NQ = S // Q
    NC = S // CMP_STRIDE
    NS = S // SEL_LEN
    RQ = R * Q
    KT = SEL_KV_TILE
    assert S % KT == 0 and S % Q == 0 and KT % Q == 0 and S >= WINDOW + Q
    assert S + CMP_LEN <= SEL_LEN * 256 and 2 * (KT // SEL_LEN) <= AUX_W - MASK_COL0
    aux_s = _aux_table(jnp.arange(S), True)
    aux_c = _aux_table(jnp.arange(NC) * CMP_STRIDE + (CMP_LEN - 1), False)
    HA = HD + V_AUG
    slab = lambda k: pl.BlockSpec((1, S, HD), lambda b, g, i: (k * G + g, b, 0))
    return pl.pallas_call(
        functools.partial(_nsa_kernel, S=S),
        out_shape=jax.ShapeDtypeStruct((T, bw), MXU_DTYPE),
        grid=(B, G, NQ),
        in_specs=[
            pl.BlockSpec((Q, R * HD), lambda b, g, i: (b * NQ + i, qc0 + g)),
            pl.BlockSpec((1, 1, NC, HD), lambda b, g, i: (b, g, 0, 0)),
            pl.BlockSpec((NC, AUX_W), lambda b, g, i: (0, 0)),
            pl.BlockSpec((1, 1, HD, NC), lambda b, g, i: (b, g, 0, 0)),
            pl.BlockSpec((NS, NC), lambda b, g, i: (0, 0)),
            slab(2),
            pl.BlockSpec((S, AUX_W), lambda b, g, i: (0, 0)),
            slab(3),
            slab(4),
            slab(5),
            pl.BlockSpec((1, GATE_ROWS, Q), lambda b, g, i: (g, 0, b * NQ + i)),
            pl.BlockSpec((1, MASK_COL0, RQ), lambda b, g, i: (g, 0, 0)),
            pl.BlockSpec((Q, R * HD), lambda b, g, i: (b * NQ + i, zc0 + g)),
        ],
        out_specs=pl.BlockSpec((Q, R * HD), lambda b, g, i: (b * NQ + i, g)),
        scratch_shapes=[
            pltpu.VMEM((HD + AUX_W, RQ), MXU_DTYPE),
            pltpu.VMEM((2, NS, Q), f32),
            pltpu.VMEM((1, RQ), f32),
            pltpu.VMEM((HA, RQ), f32),
            pltpu.VMEM((KT, RQ), f32), pltpu.VMEM((KT, RQ), f32),
            pltpu.VMEM((KT, RQ), MXU_DTYPE), pltpu.VMEM((KT, RQ), MXU_DTYPE),
            pltpu.VMEM((1, RQ), f32), pltpu.VMEM((1, RQ), f32),
            pltpu.VMEM((HA, S), MXU_DTYPE), pltpu.VMEM((HA, S), MXU_DTYPE),
            pltpu.SMEM((S // KT + 1,), jnp.int32),
        ],
        compiler_params=_params("parallel", "parallel", "arbitrary"),
        name="nsa_attention",
    )(h, k_cmp, aux_c, v_cmp_t, _overlap_table(NS, NC), kvh, aux_s, kvh, kvh, kvh, gates_t, _slope_rows(R, G), h)


def _mem_kernel(q_ref, z_ref, kv_ref, o_ref):
    hw = q_ref.shape[1] // MEM_HEADS
    bw = q_ref.shape[1]
    for h in range(MEM_HEADS):
        cs = slice(h * hw, (h + 1) * hw)
        s = _dot_nt(q_ref[:, cs], kv_ref[:, cs])
        m = jnp.max(s, axis=-1, keepdims=True)
        e = jnp.exp(s - m)
        p = e * (1.0 / jnp.sum(e, axis=-1, keepdims=True))
        o = _dot(p.astype(MXU_DTYPE), kv_ref[:, bw + h * hw:bw + (h + 1) * hw])
        o_ref[:, cs] = (o * _silu(z_ref[:, cs].astype(f32))).astype(o_ref.dtype)


def _mem_attention(h_mem, c0, bw, kv, B, S):
    T = B * S
    M = kv.shape[0] // B
    tq = min(512, S)
    nb = S // tq
    assert c0 % bw == 0
    qc = c0 // bw
    return pl.pallas_call(
        _mem_kernel,
        out_shape=jax.ShapeDtypeStruct((T, bw), MXU_DTYPE),
        grid=(B, nb),
        in_specs=[
            pl.BlockSpec((tq, bw), lambda b, i: (b * nb + i, qc)),
            pl.BlockSpec((tq, bw), lambda b, i: (b * nb + i, qc + 1)),
            pl.BlockSpec((M, 2 * bw), lambda b, i: (b, 0)),
        ],
        out_specs=pl.BlockSpec((tq, bw), lambda b, i: (b * nb + i, 0)),
        compiler_params=_params("parallel", "parallel"),
        name="mem_attention",
    )(h_mem, h_mem, kv)


def _merge_kernel(og_ref, on_ref, om_ref, wg_ref, wn_ref, wm_ref, ag_ref, an_ref, am_ref, o_ref):
    y = ag_ref[...].astype(f32) * _dot(og_ref[...], wg_ref[...])
    y = y + an_ref[...].astype(f32) * _dot(on_ref[...], wn_ref[...])
    y = y + am_ref[...].astype(f32) * _dot(om_ref[...], wm_ref[...])
    o_ref[...] = y.astype(o_ref.dtype)


def _merge(o_gla, o_nsa, o_mem, w_g, w_n, w_m, a, c0):
    T, bw = o_gla.shape
    D = w_g.shape[1]
    tm, tn = min(512, T), min(1024, D)
    nj = D // tn
    assert c0 % tn == 0
    osp = pl.BlockSpec((tm, bw), lambda j, i: (i, 0))
    wsp = pl.BlockSpec((bw, tn), lambda j, i: (0, j))
    asp = lambda c: pl.BlockSpec((tm, tn), lambda j, i: (i, c0 // tn + c * nj + j))
    return pl.pallas_call(
        _merge_kernel,
        out_shape=jax.ShapeDtypeStruct((T, D), MXU_DTYPE),
        grid=(nj, T // tm),
        in_specs=[osp, osp, osp, wsp, wsp, wsp, asp(0), asp(1), asp(2)],
        out_specs=pl.BlockSpec((tm, tn), lambda j, i: (i, j)),
        compiler_params=_params("parallel", "parallel"),
        name="branch_merge",
    )(o_gla, o_nsa, o_mem, w_g, w_n, w_m, a, a, a)


def _out_ln_kernel(m_ref, w_ref, x_ref, g_ref, b_ref, o_ref, *, alpha):
    o_ref[...] = alpha * x_ref[...] + _dot(m_ref[...], w_ref[...])

    def ln_rows(c, _):
        rows = pl.ds(pl.multiple_of(c * LN_ROWS, LN_ROWS), LN_ROWS)
        z = o_ref[rows, :]
        mu = jnp.mean(z, axis=-1, keepdims=True)
        zc = z - mu
        var = jnp.mean(zc * zc, axis=-1, keepdims=True)
        o_ref[rows, :] = zc * lax.rsqrt(var + LN_EPS) * g_ref[...] + b_ref[...]
        return 0

    lax.fori_loop(0, o_ref.shape[0] // LN_ROWS, ln_rows, 0)


def _out_ln(merged, w_out, x2, ln_g, ln_b, alpha):
    T, D = x2.shape
    tm = min(OUT_ROWS, T)
    return pl.pallas_call(
        functools.partial(_out_ln_kernel, alpha=alpha),
        out_shape=jax.ShapeDtypeStruct((T, D), x2.dtype),
        grid=(T // tm,),
        in_specs=[
            pl.BlockSpec((tm, D), lambda i: (i, 0)),
            pl.BlockSpec((D, D), lambda i: (0, 0), pipeline_mode=pl.Buffered(1)),
            pl.BlockSpec((tm, D), lambda i: (i, 0)),
            pl.BlockSpec((1, D), lambda i: (0, 0)),
            pl.BlockSpec((1, D), lambda i: (0, 0)),
        ],
        out_specs=pl.BlockSpec((tm, D), lambda i: (i, 0)),
        compiler_params=_params("parallel"),
        name="out_proj_layernorm",
    )(merged, w_out, x2, ln_g.reshape(1, D), ln_b.reshape(1, D))


def _layer(x, mem, w_in, b_merge, gla_w_a2, gla_b_a, gla_norm_g, nsa_pe_k, nsa_pe_v, nsa_wk1, nsa_wk2,
           nsa_wv1, nsa_wv2, w_mem_kv, w_br_gla, w_br_nsa, w_br_mem, w_out, ln_g, ln_b, depth):
    B, S, D = x.shape
    T = B * S
    bw = D // 2
    gk = bw // 2
    G, R = NSA_GROUPS, NSA_REP
    HD = bw // NSA_HEADS
    kvw = G * HD
    cdt = MXU_DTYPE

    o_ga = 2 * gk + 2 * bw
    o_nq = o_ga + GLA_LOWRANK
    o_nbg = o_nq + bw + 6 * kvw + bw
    o_mq = o_nbg + 3 * NSA_HEADS
    o_mrg = o_mq + 2 * bw
    assert w_in.shape[1] == o_mrg + N_BRANCH * D

    x2 = x.reshape(T, D)
    ones = lambda n: jnp.ones((n,), f32)

    w_t = jnp.swapaxes(w_in, 0, 1)
    h_small, xb = _project_small(x2, w_t, o_ga, GLA_LOWRANK, o_nbg, 3 * NSA_HEADS)
    o_nkv, o_nz = o_nq + bw, o_nq + bw + 6 * kvw
    w_all = _repack(w_t, [(0, o_ga), (o_mq, 2 * bw), (o_nq, bw), (o_nz, bw), (o_nkv, 6 * kvw), (o_mrg, N_BRANCH * D)])
    c_gla, c_mem = 0, o_ga
    c_nq = c_mem + 2 * bw
    c_nz = c_nq + bw
    c_nkv = c_nz + bw
    c_mrg = c_nkv + 6 * kvw
    dk, mhd = gk // GLA_HEADS, bw // MEM_HEADS
    scale = jnp.concatenate([jnp.full((gk,), dk ** -0.5, f32), ones(o_ga - gk),
                             jnp.full((bw,), mhd ** -0.5, f32), ones(bw),
                             jnp.full((bw,), HD ** -0.5 * LOG2E, f32), ones(bw)])
    h = _project(xb, w_all, scale, cdt, n=c_nkv, nt=True, name="proj_in")
    kvh = _project(xb, w_all, ones(6 * kvw), cdt, c0=c_nkv, n=6 * kvw, nt=True, grouped=True, name="proj_nsa_kv")
    a = _project(xb, w_all, b_merge, cdt, c0=c_mrg, gate=True, nt=True, name="proj_merge_gates")

    wa_pad = jnp.concatenate([gla_w_a2, jnp.zeros((SMALL_W - GLA_LOWRANK, gk), f32)], axis=0)
    o_gla = _gla(h, c_gla, gk, h_small, wa_pad, gla_b_a, gla_norm_g, B, S)

    NC = S // CMP_STRIDE
    blocks = lambda k: kvh[k * G:(k + 1) * G].reshape(G, B, NC, CMP_STRIDE * HD)
    k_cmp = _compress(blocks(0), nsa_pe_k, nsa_wk1, nsa_wk2, False)
    v_cmp_t = _compress(blocks(1), nsa_pe_v, nsa_wv1, nsa_wv2, True)
    gl = h_small[:, GLA_LOWRANK:GLA_LOWRANK + 3 * NSA_HEADS].reshape(T, G, 3 * R).transpose(1, 2, 0)
    gates_t = jnp.concatenate([gl, jnp.zeros((G, GATE_ROWS - 3 * R, T), f32)], axis=1)
    o_nsa = _nsa_attention(h, c_nq, c_nz, bw, kvh, k_cmp, v_cmp_t, gates_t, B, S)

    M = mem.shape[1]
    kv = _project(mem.reshape(B * M, D).astype(cdt), w_mem_kv.astype(cdt), ones(2 * bw), cdt, name="proj_mem_kv")
    o_mem = _mem_attention(h, c_mem, bw, kv, B, S)

    merged = _merge(o_gla, o_nsa, o_mem, w_br_gla.astype(cdt), w_br_nsa.astype(cdt), w_br_mem.astype(cdt), a, 0)
    alpha = (2 * depth) ** 0.25
    return _out_ln(merged, w_out.astype(cdt), x2, ln_g, ln_b, alpha).reshape(B, S, D)


def kernel(x, mem, w_in, b_merge, gla_w_a2, gla_b_a, gla_norm_g, nsa_pe_k, nsa_pe_v, nsa_wk1, nsa_wk2, nsa_wv1, nsa_wv2, w_mem_kv, w_br_gla, w_br_nsa, w_br_mem, w_out, ln_g, ln_b):
    depth = w_in.shape[0]
    for l in range(depth):
        x = _layer(x, mem, w_in[l], b_merge[l], gla_w_a2[l], gla_b_a[l], gla_norm_g[l], nsa_pe_k[l], nsa_pe_v[l],
                   nsa_wk1[l], nsa_wk2[l], nsa_wv1[l], nsa_wv2[l], w_mem_kv[l], w_br_gla[l], w_br_nsa[l],
                   w_br_mem[l], w_out[l], ln_g[l], ln_b[l], depth)
    return x
```

```python
import functools

import jax
import jax.numpy as jnp
from jax import lax
from jax.experimental import pallas as pl
from jax.experimental.pallas import tpu as pltpu

N_BRANCH = 3
GLA_HEADS = 4
GLA_LOWRANK = 16
GLA_TAU = 16.0
GLA_CHUNK = 64
NSA_HEADS = 16
NSA_GROUPS = 4
NSA_REP = NSA_HEADS // NSA_GROUPS
CMP_LEN = 32
CMP_STRIDE = 16
SEL_LEN = 64
SEL_TOPK = 16
WINDOW = 512
FORCE_SCORE = 1e4
MEM_HEADS = 4
LN_EPS = 1e-5
RMS_EPS = 1e-6
NEG_INF = -1e30
LOG2E = 1.4426950408889634

LANES = 128
VMEM_LIMIT_BYTES = 56 * 1024 * 1024
MXU_DTYPE = jnp.bfloat16

SEL_KV_TILE = 512
NSA_Q = 256
GLA_STEP_CHUNKS = 4
LN_ROWS = 64
OUT_ROWS = 256
SMALL_W = LANES
AUX_W = LANES
MASK_COL0 = 16
V_AUG = 16
MASK_BIG = -NEG_INF
ROW_ALIGN = 16
GATE_ROWS = 16

f32 = jnp.float32


def _dot(a, b):
    return jnp.dot(a, b, preferred_element_type=f32)


def _dot_nt(a, b):
    return lax.dot_general(a, b, (((1,), (1,)), ((), ())), preferred_element_type=f32)


def _dot_tn(a, b):
    return lax.dot_general(a, b, (((0,), (0,)), ((), ())), preferred_element_type=f32)


def _sigmoid(x):
    return 0.5 * (jnp.tanh(0.5 * x) + 1.0)


def _silu(x):
    return x * _sigmoid(x)


def _log_sigmoid(x):
    return -(jnp.maximum(-x, 0.0) + jnp.log(1.0 + jnp.exp(-jnp.abs(x))))


def _split2(x):
    hi = x.astype(MXU_DTYPE)
    lo = (x - hi.astype(f32)).astype(MXU_DTYPE)
    return hi, lo


def _split3(x):
    hi = x.astype(MXU_DTYPE)
    r1 = x - hi.astype(f32)
    mid = r1.astype(MXU_DTYPE)
    lo = (r1 - mid.astype(f32)).astype(MXU_DTYPE)
    return hi, mid, lo


def _params(*sem):
    return pltpu.CompilerParams(dimension_semantics=sem, vmem_limit_bytes=VMEM_LIMIT_BYTES)


def _proj_kernel(x_ref, w_ref, r_ref, o_ref, *, gate, nt, grouped):
    acc = _dot_nt(x_ref[...], w_ref[...]) if nt else _dot(x_ref[...], w_ref[...])
    res = (_sigmoid(acc + r_ref[...]) if gate else acc * r_ref[...]).astype(o_ref.dtype)
    if grouped:
        for c in range(o_ref.shape[0]):
            o_ref[c] = res[:, c * LANES:(c + 1) * LANES]
    else:
        o_ref[...] = res


def _project(x, w, row, out_dtype, c0=0, n=None, gate=False, nt=False, grouped=False, name="proj"):
    M, K = x.shape
    n = (w.shape[0] if nt else w.shape[1]) - c0 if n is None else n
    bm = min(1024, M)
    bn = min(1024, n)
    assert M % bm == 0 and n % bn == 0 and c0 % bn == 0
    wspec = (pl.BlockSpec((bn, K), lambda j, i: (c0 // bn + j, 0)) if nt
             else pl.BlockSpec((K, bn), lambda j, i: (0, c0 // bn + j)))
    if grouped:
        out_shape = jax.ShapeDtypeStruct((n // LANES, M, LANES), out_dtype)
        ospec = pl.BlockSpec((bn // LANES, bm, LANES), lambda j, i: (j, i, 0))
    else:
        out_shape = jax.ShapeDtypeStruct((M, n), out_dtype)
        ospec = pl.BlockSpec((bm, bn), lambda j, i: (i, j))
    return pl.pallas_call(
        functools.partial(_proj_kernel, gate=gate, nt=nt, grouped=grouped),
        out_shape=out_shape,
        grid=(n // bn, M // bm),
        in_specs=[pl.BlockSpec((bm, K), lambda j, i: (i, 0)), wspec, pl.BlockSpec((1, bn), lambda j, i: (0, j))],
        out_specs=ospec,
        compiler_params=_params("parallel", "parallel"),
        name=name,
    )(x, w, row.reshape(1, n).astype(f32))


def _row_window(rows, width, start_unit):
    return pl.BlockSpec((pl.Element(rows), pl.Element(width)), lambda *g: (start_unit(*g) * ROW_ALIGN, 0))


def _proj_small_kernel(x_ref, wa_ref, wb_ref, o_ref, xb_ref):
    xb = x_ref[...].astype(xb_ref.dtype)
    xb_ref[...] = xb
    pad = jnp.zeros((LANES - wa_ref.shape[0] - wb_ref.shape[0], wa_ref.shape[1]), f32)
    w = jnp.concatenate([wa_ref[...], wb_ref[...], pad], axis=0)
    o_ref[...] = _dot_nt(xb, w.astype(xb.dtype))


def _project_small(x, w_t, ra, na, rb, nb):
    M, K = x.shape
    bm = min(512, M)
    assert ra % ROW_ALIGN == 0 and rb % ROW_ALIGN == 0 and na % 8 == 0 and nb % 8 == 0 and na + nb <= LANES
    return pl.pallas_call(
        _proj_small_kernel,
        out_shape=(jax.ShapeDtypeStruct((M, LANES), f32), jax.ShapeDtypeStruct((M, K), MXU_DTYPE)),
        grid=(M // bm,),
        in_specs=[
            pl.BlockSpec((bm, K), lambda i: (i, 0)),
            _row_window(na, K, lambda i: ra // ROW_ALIGN),
            _row_window(nb, K, lambda i: rb // ROW_ALIGN),
        ],
        out_specs=(pl.BlockSpec((bm, LANES), lambda i: (i, 0)), pl.BlockSpec((bm, K), lambda i: (i, 0))),
        compiler_params=_params("parallel"),
        name="proj_small",
    )(x, w_t, w_t)


def _packed_blocks(segments, n_rows, bn):
    segs, blk = [], 0
    for src, n in segments:
        assert n % bn == 0 and src % ROW_ALIGN == 0 and src + n <= n_rows
        segs.append((blk, blk + n // bn, src // ROW_ALIGN - blk * (bn // ROW_ALIGN)))
        blk += n // bn

    def start_unit(j):
        s = jnp.int32(0)
        for lo, hi, off in segs:
            s = jnp.where((j >= lo) & (j < hi), j * (bn // ROW_ALIGN) + off, s)
        return s

    return blk, start_unit


def _proj_cast_kernel(x_ref, wrow_ref, r_ref, o_ref, w0_ref, w1_ref, *, gate, grouped, rows):
    jp, i = pl.program_id(0), pl.program_id(1)
    dst = pl.ds(pl.multiple_of(i * rows, rows), rows)

    def step(w_new, w_cur):
        w_new[dst, :] = wrow_ref[...].astype(w_new.dtype)
        acc = _dot_nt(x_ref[...], w_cur[...])
        res = (_sigmoid(acc + r_ref[...]) if gate else acc * r_ref[...]).astype(o_ref.dtype)
        if grouped:
            for c in range(o_ref.shape[0]):
                o_ref[c] = res[:, c * LANES:(c + 1) * LANES]
        else:
            o_ref[...] = res

    @pl.when(jp == 0)
    def _():
        w0_ref[dst, :] = wrow_ref[...].astype(w0_ref.dtype)

    pl.when((jp > 0) & (jp % 2 == 1))(functools.partial(step, w1_ref, w0_ref))
    pl.when((jp > 0) & (jp % 2 == 0))(functools.partial(step, w0_ref, w1_ref))


def _project_cast(x, w_t, start_unit, b0, nb, row, out_dtype, gate=False, grouped=False, name="proj"):
    M, K = x.shape
    bm = min(1024, M)
    bn = 1024
    mi = M // bm
    rows = bn // mi
    n = nb * bn
    assert M % bm == 0 and bn % mi == 0 and rows % ROW_ALIGN == 0
    itile = lambda jp, i: jnp.where(jp == 0, 0, i)
    oblk = lambda jp: jnp.maximum(jp - 1, 0)
    if grouped:
        out_shape = jax.ShapeDtypeStruct((n // LANES, M, LANES), out_dtype)
        ospec = pl.BlockSpec((bn // LANES, bm, LANES), lambda jp, i: (oblk(jp), itile(jp, i), 0))
    else:
        out_shape = jax.ShapeDtypeStruct((M, n), out_dtype)
        ospec = pl.BlockSpec((bm, bn), lambda jp, i: (itile(jp, i), oblk(jp)))
    return pl.pallas_call(
        functools.partial(_proj_cast_kernel, gate=gate, grouped=grouped, rows=rows),
        out_shape=out_shape,
        grid=(nb + 1, mi),
        in_specs=[
            pl.BlockSpec((bm, K), lambda jp, i: (itile(jp, i), 0)),
            _row_window(rows, K, lambda jp, i: start_unit(b0 + jnp.minimum(jp, nb - 1)) + i * (rows // ROW_ALIGN)),
            pl.BlockSpec((1, bn), lambda jp, i: (0, oblk(jp))),
        ],
        out_specs=ospec,
        scratch_shapes=[pltpu.VMEM((bn, K), MXU_DTYPE), pltpu.VMEM((bn, K), MXU_DTYPE)],
        compiler_params=_params("arbitrary", "arbitrary"),
        name=name,
    )(x, w_t, row.reshape(1, n).astype(f32))


def _gla_kernel(q_ref, k_ref, v_ref, z_ref, ga_ref, wa_ref, ba_ref, ng_ref, o_ref, st_ref, *, dk, dv):
    C = GLA_CHUNK

    @pl.when(pl.program_id(1) == 0)
    def _():
        st_ref[...] = jnp.zeros_like(st_ref)

    row = lax.broadcasted_iota(jnp.int32, (C, C), 0)
    col = lax.broadcasted_iota(jnp.int32, (C, C), 1)
    tril = row >= col
    ltri = jnp.where(tril, 1.0, 0.0).astype(MXU_DTYPE)
    wa_hi, wa_lo = _split2(wa_ref[...])
    for c in range(GLA_STEP_CHUNKS):
        rows = slice(c * C, (c + 1) * C)
        ga_hi, ga_lo = _split2(ga_ref[rows, :])
        zz = _dot(ga_hi, wa_hi) + _dot(ga_lo, wa_hi) + _dot(ga_hi, wa_lo) + ba_ref[...]
        la = _log_sigmoid(zz) * (1.0 / GLA_TAU)
        la_hi, la_mid, la_lo = _split3(la)
        bcum = _dot(ltri, la_hi) + _dot(ltri, la_mid) + _dot(ltri, la_lo)
        for h in range(GLA_HEADS):
            kc = slice(h * dk, (h + 1) * dk)
            vc = slice(h * dv, (h + 1) * dv)
            b = bcum[:, kc]
            bl = b[C - 1:C, :]
            qh = q_ref[rows, kc].astype(f32)
            kh = k_ref[rows, kc].astype(f32)
            vh = v_ref[rows, vc]
            q_d = (qh * jnp.exp(b)).astype(MXU_DTYPE)
            k_d = (kh * jnp.exp(-b)).astype(MXU_DTYPE)
            k_e = (kh * jnp.exp(bl - b)).astype(MXU_DTYPE)
            att = jnp.where(tril, _dot_nt(q_d, k_d), 0.0)
            st = st_ref[h]
            o = _dot(att.astype(MXU_DTYPE), vh) + _dot_nt(q_d, st.astype(MXU_DTYPE))
            st_ref[h] = st * jnp.exp(bl) + _dot_tn(vh, k_e)
            ms = jnp.mean(o * o, axis=-1, keepdims=True)
            on = o * lax.rsqrt(ms + RMS_EPS) * ng_ref[...]
            zg = z_ref[rows, vc].astype(f32)
            o_ref[rows, vc] = (on * _silu(zg)).astype(o_ref.dtype)


def _gla(h, c0, kw, h_small, wa_pad, b_a, norm_g, B, S):
    T = B * S
    dk, dv = kw // GLA_HEADS, 2 * kw // GLA_HEADS
    cs = GLA_STEP_CHUNKS * GLA_CHUNK
    nb = S // cs
    assert S % cs == 0 and c0 % (2 * kw) == 0
    h_gla = h
    rowmap = lambda col: (lambda b, i: (b * nb + i, col))
    qc, vc = c0 // kw, c0 // (2 * kw)
    return pl.pallas_call(
        functools.partial(_gla_kernel, dk=dk, dv=dv),
        out_shape=jax.ShapeDtypeStruct((T, 2 * kw), MXU_DTYPE),
        grid=(B, nb),
        in_specs=[
            pl.BlockSpec((cs, kw), rowmap(qc)),
            pl.BlockSpec((cs, kw), rowmap(qc + 1)),
            pl.BlockSpec((cs, 2 * kw), rowmap(vc + 1)),
            pl.BlockSpec((cs, 2 * kw), rowmap(vc + 2)),
            pl.BlockSpec((cs, SMALL_W), rowmap(0)),
            pl.BlockSpec((SMALL_W, kw), lambda b, i: (0, 0)),
            pl.BlockSpec((1, kw), lambda b, i: (0, 0)),
            pl.BlockSpec((1, dv), lambda b, i: (0, 0)),
        ],
        out_specs=pl.BlockSpec((cs, 2 * kw), rowmap(0)),
        scratch_shapes=[pltpu.VMEM((GLA_HEADS, dv, dk), f32)],
        compiler_params=_params("parallel", "arbitrary"),
        name="gla",
    )(h_gla, h_gla, h_gla, h_gla, h_small, wa_pad, b_a.reshape(1, kw), norm_g.reshape(1, dv))


def _compress_kernel(x_ref, pe_ref, w1_ref, w2_ref, o_ref, *, transpose_out):
    x = x_ref[0, 0].astype(f32)
    half = x.shape[1]
    xa = (x + pe_ref[0:1, :]).astype(MXU_DTYPE)
    xb = (x + pe_ref[1:2, :]).astype(MXU_DTYPE)
    ya = _dot(xa, w1_ref[0:half, :])
    yb = _dot(xb, w1_ref[half:2 * half, :])
    nc = x.shape[0]
    pre = ya + pltpu.roll(yb, nc - 1, 0)
    out = _dot(_silu(pre).astype(MXU_DTYPE), w2_ref[...])
    o_ref[0, 0] = (out.T if transpose_out else out).astype(o_ref.dtype)


def _compress(xblk, pe, w1, w2, transpose_out):
    G, B, NC, W = xblk.shape
    hd = w2.shape[0]
    oshape = (hd, NC) if transpose_out else (NC, hd)
    return pl.pallas_call(
        functools.partial(_compress_kernel, transpose_out=transpose_out),
        out_shape=jax.ShapeDtypeStruct((B, G) + oshape, MXU_DTYPE),
        grid=(B, G),
        in_specs=[
            pl.BlockSpec((1, 1, NC, W), lambda b, g: (g, b, 0, 0)),
            pl.BlockSpec((2, W), lambda b, g: (0, 0)),
            pl.BlockSpec((2 * W, hd), lambda b, g: (0, 0)),
            pl.BlockSpec((hd, hd), lambda b, g: (0, 0)),
        ],
        out_specs=pl.BlockSpec((1, 1) + oshape, lambda b, g: (b, g, 0, 0)),
        compiler_params=_params("parallel", "parallel"),
        name="nsa_compress",
    )(xblk, pe.reshape(2, W).astype(f32), w1.astype(MXU_DTYPE), w2.astype(MXU_DTYPE))


def _mask_heads(ok, s, R, Q, fill=NEG_INF):
    return jnp.concatenate([jnp.where(ok, s[:, r * Q:(r + 1) * Q], fill) for r in range(R)], axis=1)


def _softmax2_cols(s, ok, R, Q):
    s = _mask_heads(ok, s, R, Q)
    m = jnp.max(s, axis=0, keepdims=True)
    e = _mask_heads(ok, jnp.exp2(s - m), R, Q, 0.0)
    den = jnp.sum(e, axis=0, keepdims=True)
    return e * jnp.where(den > 0.0, 1.0 / den, 0.0)


def _nsa_kernel(q_ref, kc_ref, auxc_ref, vct_ref, ov_ref, ks3_ref, aux_ref, vs_ref, kw3_ref, vw_ref, gt_ref,
                sl_ref, z_ref, o_ref, qa_ref, sel_ref, m_ref, acc_ref, s0_ref, s1_ref, p0_ref, p1_ref, al0_ref,
                al1_ref, vst_ref, vwt_ref, idx_ref, *, S):
    Q, R = NSA_Q, NSA_REP
    s_refs, p_refs, al_refs = (s0_ref, s1_ref), (p0_ref, p1_ref), (al0_ref, al1_ref)
    ks_ref, kw_ref = ks3_ref.at[0], kw3_ref.at[0]
    HD = q_ref.shape[1] // R
    RQ = R * Q
    NC = S // CMP_STRIDE
    NS = S // SEL_LEN
    KT = SEL_KV_TILE
    BPT = KT // SEL_LEN
    NT = S // KT
    MR = HD + MASK_COL0
    qb = pl.program_id(2)
    start = qb * Q

    @pl.when(qb == 0)
    def _():
        tail = jnp.where(lax.broadcasted_iota(jnp.int32, (V_AUG, S), 0) == 0, 1.0, 0.0).astype(vst_ref.dtype)
        vst_ref[HD:, :] = tail
        vwt_ref[HD:, :] = tail

        def fill(c, carry):
            r = pl.ds(pl.multiple_of(c * LANES, LANES), LANES)
            vst_ref[0:HD, r] = vs_ref[0, r, :].T
            vwt_ref[0:HD, r] = vw_ref[0, r, :].T
            return carry

        lax.fori_loop(0, S // LANES, fill, 0)

    q = q_ref[...]
    qa_ref[0:HD, :] = jnp.concatenate(
        [q[c * LANES:(c + 1) * LANES, r * HD:(r + 1) * HD].T for r in range(R) for c in range(Q // LANES)], axis=1)
    qa_ref[HD:MR, :] = sl_ref[0]
    qa_ref[MR:, :] = jnp.zeros((AUX_W - MASK_COL0, RQ), qa_ref.dtype)
    qa = qa_ref[...]

    ok_c = (lax.broadcasted_iota(jnp.int32, (NC, Q), 0) * CMP_STRIDE + (CMP_LEN - 1)
            <= start + lax.broadcasted_iota(jnp.int32, (NC, Q), 1))
    s_c = _dot(jnp.concatenate([kc_ref[0, 0], auxc_ref[...]], axis=1), qa)
    p_c = _softmax2_cols(s_c, ok_c, R, Q)
    o_cmp = _dot(vct_ref[0, 0], p_c.astype(MXU_DTYPE))

    p_sum = p_c[:, 0:Q]
    for r in range(1, R):
        p_sum = p_sum + p_c[:, r * Q:(r + 1) * Q]
    ov = ov_ref[...]
    ps_hi, ps_mid, ps_lo = _split3(p_sum)
    imp = _dot(ov, ps_hi) + _dot(ov, ps_mid) + _dot(ov, ps_lo)

    blk = lax.broadcasted_iota(jnp.int32, (NS, Q), 0)
    tq1 = start + lax.broadcasted_iota(jnp.int32, (NS, Q), 1)
    cur = jnp.right_shift(tq1, SEL_LEN.bit_length() - 1)
    forced = (blk == 0) | (blk == cur) | (blk == cur - 1)
    score = jnp.where(forced, FORCE_SCORE, jnp.where(blk * SEL_LEN <= tq1, imp, -1.0))
    blk_f = blk.astype(f32)
    for _ in range(min(SEL_TOPK, NS)):
        m = jnp.max(score, axis=0, keepdims=True)
        first = jnp.min(jnp.where(score == m, blk_f, float(NS)), axis=0, keepdims=True)
        score = jnp.where(blk_f == first, -jnp.inf, score)
    sel = jnp.where(score == -jnp.inf, 1.0, 0.0)
    sel_past = jnp.where(blk * SEL_LEN < start, sel, 0.0)
    sel_ref[0] = sel
    sel_ref[1] = sel_past
    n_act = jnp.int32(0)
    for i in range(NT):
        idx_ref[n_act] = jnp.int32(i)
        n_act = n_act + (jnp.max(sel_past[i * BPT:(i + 1) * BPT, :]) > 0.0).astype(jnp.int32)

    WK = WINDOW + Q
    ws = pl.multiple_of(jnp.maximum(start - WINDOW, 0), Q)
    dist_w = (start - ws) + (lax.broadcasted_iota(jnp.int32, (WK, Q), 1)
                             - lax.broadcasted_iota(jnp.int32, (WK, Q), 0))
    ok_w = (dist_w >= 0) & (dist_w < WINDOW)
    s_w = _dot(jnp.concatenate([kw_ref[pl.ds(ws, WK), :], aux_ref[pl.ds(ws, WK), :]], axis=1), qa)
    s_w = _mask_heads(ok_w, s_w, R, Q)
    e_w = jnp.exp2(s_w - jnp.max(s_w, axis=0, keepdims=True)).astype(MXU_DTYPE)
    acc_w = _dot(vwt_ref[:, pl.ds(ws, WK)], e_w)
    o_win = acc_w[0:HD, :] * (1.0 / acc_w[HD:HD + 1, :])

    def mask_rows(which, b0, valid):
        mrow = jnp.where(valid, (sel_ref[which, pl.ds(b0, BPT), :] - 1.0) * MASK_BIG, -MASK_BIG)
        mrow = jnp.concatenate([mrow] * R, axis=1)
        qa_ref[MR:MR + 2 * BPT, :] = jnp.concatenate([mrow, jnp.zeros_like(mrow)], axis=0).astype(qa_ref.dtype)

    q0 = pl.multiple_of(start, Q)
    mask_rows(0, pl.multiple_of((start // KT) * BPT, BPT), True)
    s_o = _dot(jnp.concatenate([ks_ref[pl.ds(q0, Q), :], aux_ref[pl.ds(q0, Q), :]], axis=1), qa_ref[...])
    ok_o = lax.broadcasted_iota(jnp.int32, (Q, Q), 0) <= lax.broadcasted_iota(jnp.int32, (Q, Q), 1)
    s_o = _mask_heads(ok_o, s_o, R, Q)
    m_o = jnp.max(s_o, axis=0, keepdims=True)
    m_ref[...] = m_o
    acc_ref[...] = _dot(vst_ref[:, pl.ds(q0, Q)], jnp.exp2(s_o - m_o).astype(MXU_DTYPE))

    def tile_of(j):
        return idx_ref[jnp.clip(j, 0, jnp.maximum(n_act - 1, 0))]

    def scores(j, slot):
        i = tile_of(j)
        k0 = pl.multiple_of(i * KT, KT)
        mask_rows(1, pl.multiple_of(i * BPT, BPT), j < n_act)
        s_refs[slot][...] = _dot(jnp.concatenate([ks_ref[pl.ds(k0, KT), :], aux_ref[pl.ds(k0, KT), :]], axis=1),
                                 qa_ref[...])

    def softmax(slot):
        s = s_refs[slot][...]
        m_old = m_ref[...]
        m_new = jnp.maximum(m_old, jnp.max(s, axis=0, keepdims=True))
        p_refs[slot][...] = jnp.exp2(s - m_new).astype(MXU_DTYPE)
        al_refs[slot][...] = jnp.exp2(m_old - m_new)
        m_ref[...] = m_new

    def accumulate(j, slot):
        k0 = pl.multiple_of(tile_of(j) * KT, KT)
        acc_ref[...] = al_refs[slot][...] * acc_ref[...] + _dot(vst_ref[:, pl.ds(k0, KT)], p_refs[slot][...])

    scores(0, 0)
    scores(1, 1)
    softmax(0)

    def pipe(k, c):
        j = 2 * k
        softmax(1)
        accumulate(j, 0)
        scores(j + 2, 0)
        accumulate(j + 1, 1)
        scores(j + 3, 1)
        softmax(0)
        return c

    n_pairs = n_act // 2
    lax.fori_loop(0, n_pairs, pipe, 0)
    accumulate(2 * n_pairs, 0)
    acc_s = acc_ref[...]
    o_sel = acc_s[0:HD, :] * (1.0 / acc_s[HD:HD + 1, :])

    gates = _sigmoid(gt_ref[0])
    for r in range(R):
        cs = slice(r * Q, (r + 1) * Q)
        o_r = (gates[3 * r:3 * r + 1, :] * o_cmp[:, cs] + gates[3 * r + 1:3 * r + 2, :] * o_sel[:, cs]
               + gates[3 * r + 2:3 * r + 3, :] * o_win[:, cs])
        hs = slice(r * HD, (r + 1) * HD)
        o_ref[:, hs] = (o_r.T * _silu(z_ref[:, hs].astype(f32))).astype(o_ref.dtype)


def _aux_table(pos, onehot):
    hi = (pos // SEL_LEN) * SEL_LEN
    lo = pos % SEL_LEN
    col = jnp.arange(AUX_W)[None, :]
    t = jnp.where(col < 3, hi[:, None], jnp.where(col < 6, lo[:, None], 0)).astype(f32)
    if onehot:
        blk = (pos // SEL_LEN) % (SEL_KV_TILE // SEL_LEN)
        t = t + jnp.where(col == MASK_COL0 + blk[:, None], 1.0, 0.0)
    return t.astype(MXU_DTYPE)


def _slope_rows(R, G):
    sl = jnp.exp2(-8.0 * (jnp.arange(NSA_HEADS, dtype=f32) + 1.0) / NSA_HEADS) * LOG2E
    parts = _split3(sl)
    rows = jnp.stack(parts + parts, axis=0).astype(f32)
    rows = jnp.repeat(rows.reshape(6, G, R).transpose(1, 0, 2), NSA_Q, axis=2)
    pad = jnp.zeros((G, MASK_COL0 - 6, R * NSA_Q), f32)
    return jnp.concatenate([rows, pad], axis=1).astype(MXU_DTYPE)


def _overlap_table(NS, NC):
    jj = jnp.arange(NS)[:, None] * SEL_LEN
    nn = jnp.arange(NC)[None, :] * CMP_STRIDE
    return jnp.where((nn < jj + SEL_LEN) & (nn + (CMP_LEN - 1) >= jj), 1.0, 0.0).astype(MXU_DTYPE)


def _nsa_attention(h, cq, cz, bw, kvh, k_cmp, v_cmp_t, gates_t, B, S):
    T = B * S
    G, R = NSA_GROUPS, NSA_REP
    HD = bw // NSA_HEADS
    assert cq % (R * HD) == 0 and cz % (R * HD) == 0
    qc0, zc0 = cq // (R * HD), cz // (R * HD)
    Q = NSA_Q
    NQ = S // Q
    NC = S // CMP_STRIDE
    NS = S // SEL_LEN
    RQ = R * Q
    KT = SEL_KV_TILE
    assert S % KT == 0 and S % Q == 0 and KT % Q == 0 and S >= WINDOW + Q
    assert S + CMP_LEN <= SEL_LEN * 256 and 2 * (KT // SEL_LEN) <= AUX_W - MASK_COL0
    aux_s = _aux_table(jnp.arange(S), True)
    aux_c = _aux_table(jnp.arange(NC) * CMP_STRIDE + (CMP_LEN - 1), False)
    HA = HD + V_AUG
    slab = lambda k: pl.BlockSpec((1, S, HD), lambda b, g, i: (k * G + g, b, 0))
    return pl.pallas_call(
        functools.partial(_nsa_kernel, S=S),
        out_shape=jax.ShapeDtypeStruct((T, bw), MXU_DTYPE),
        grid=(B, G, NQ),
        in_specs=[
            pl.BlockSpec((Q, R * HD), lambda b, g, i: (b * NQ + i, qc0 + g)),
            pl.BlockSpec((1, 1, NC, HD), lambda b, g, i: (b, g, 0, 0)),
            pl.BlockSpec((NC, AUX_W), lambda b, g, i: (0, 0)),
            pl.BlockSpec((1, 1, HD, NC), lambda b, g, i: (b, g, 0, 0)),
            pl.BlockSpec((NS, NC), lambda b, g, i: (0, 0)),
            slab(2),
            pl.BlockSpec((S, AUX_W), lambda b, g, i: (0, 0)),
            slab(3),
            slab(4),
            slab(5),
            pl.BlockSpec((1, GATE_ROWS, Q), lambda b, g, i: (g, 0, b * NQ + i)),
            pl.BlockSpec((1, MASK_COL0, RQ), lambda b, g, i: (g, 0, 0)),
            pl.BlockSpec((Q, R * HD), lambda b, g, i: (b * NQ + i, zc0 + g)),
        ],
        out_specs=pl.BlockSpec((Q, R * HD), lambda b, g, i: (b * NQ + i, g)),
        scratch_shapes=[
            pltpu.VMEM((HD + AUX_W, RQ), MXU_DTYPE),
            pltpu.VMEM((2, NS, Q), f32),
            pltpu.VMEM((1, RQ), f32),
            pltpu.VMEM((HA, RQ), f32),
            pltpu.VMEM((KT, RQ), f32), pltpu.VMEM((KT, RQ), f32),
            pltpu.VMEM((KT, RQ), MXU_DTYPE), pltpu.VMEM((KT, RQ), MXU_DTYPE),
            pltpu.VMEM((1, RQ), f32), pltpu.VMEM((1, RQ), f32),
            pltpu.VMEM((HA, S), MXU_DTYPE), pltpu.VMEM((HA, S), MXU_DTYPE),
            pltpu.SMEM((S // KT + 1,), jnp.int32),
        ],
        compiler_params=_params("parallel", "parallel", "arbitrary"),
        name="nsa_attention",
    )(h, k_cmp, aux_c, v_cmp_t, _overlap_table(NS, NC), kvh, aux_s, kvh, kvh, kvh, gates_t, _slope_rows(R, G), h)


def _mem_kernel(q_ref, z_ref, kv_ref, o_ref):
    hw = q_ref.shape[1] // MEM_HEADS
    bw = q_ref.shape[1]
    for h in range(MEM_HEADS):
        cs = slice(h * hw, (h + 1) * hw)
        s = _dot_nt(q_ref[:, cs], kv_ref[:, cs])
        m = jnp.max(s, axis=-1, keepdims=True)
        e = jnp.exp(s - m)
        p = e * (1.0 / jnp.sum(e, axis=-1, keepdims=True))
        o = _dot(p.astype(MXU_DTYPE), kv_ref[:, bw + h * hw:bw + (h + 1) * hw])
        o_ref[:, cs] = (o * _silu(z_ref[:, cs].astype(f32))).astype(o_ref.dtype)


def _mem_attention(h_mem, c0, bw, kv, B, S):
    T = B * S
    M = kv.shape[0] // B
    tq = min(512, S)
    nb = S // tq
    assert c0 % bw == 0
    qc = c0 // bw
    return pl.pallas_call(
        _mem_kernel,
        out_shape=jax.ShapeDtypeStruct((T, bw), MXU_DTYPE),
        grid=(B, nb),
        in_specs=[
            pl.BlockSpec((tq, bw), lambda b, i: (b * nb + i, qc)),
            pl.BlockSpec((tq, bw), lambda b, i: (b * nb + i, qc + 1)),
            pl.BlockSpec((M, 2 * bw), lambda b, i: (b, 0)),
        ],
        out_specs=pl.BlockSpec((tq, bw), lambda b, i: (b * nb + i, 0)),
        compiler_params=_params("parallel", "parallel"),
        name="mem_attention",
    )(h_mem, h_mem, kv)


def _merge_kernel(og_ref, on_ref, om_ref, wg_ref, wn_ref, wm_ref, ag_ref, an_ref, am_ref, o_ref):
    y = ag_ref[...].astype(f32) * _dot(og_ref[...], wg_ref[...])
    y = y + an_ref[...].astype(f32) * _dot(on_ref[...], wn_ref[...])
    y = y + am_ref[...].astype(f32) * _dot(om_ref[...], wm_ref[...])
    o_ref[...] = y.astype(o_ref.dtype)


def _merge(o_gla, o_nsa, o_mem, w_g, w_n, w_m, a, c0):
    T, bw = o_gla.shape
    D = w_g.shape[1]
    tm, tn = min(512, T), min(1024, D)
    nj = D // tn
    assert c0 % tn == 0
    osp = pl.BlockSpec((tm, bw), lambda j, i: (i, 0))
    wsp = pl.BlockSpec((bw, tn), lambda j, i: (0, j))
    asp = lambda c: pl.BlockSpec((tm, tn), lambda j, i: (i, c0 // tn + c * nj + j))
    return pl.pallas_call(
        _merge_kernel,
        out_shape=jax.ShapeDtypeStruct((T, D), MXU_DTYPE),
        grid=(nj, T // tm),
        in_specs=[osp, osp, osp, wsp, wsp, wsp, asp(0), asp(1), asp(2)],
        out_specs=pl.BlockSpec((tm, tn), lambda j, i: (i, j)),
        compiler_params=_params("parallel", "parallel"),
        name="branch_merge",
    )(o_gla, o_nsa, o_mem, w_g, w_n, w_m, a, a, a)


def _out_ln_kernel(m_ref, w_ref, x_ref, g_ref, b_ref, o_ref, *, alpha):
    o_ref[...] = alpha * x_ref[...] + _dot(m_ref[...], w_ref[...])

    def ln_rows(c, _):
        rows = pl.ds(pl.multiple_of(c * LN_ROWS, LN_ROWS), LN_ROWS)
        z = o_ref[rows, :]
        mu = jnp.mean(z, axis=-1, keepdims=True)
        zc = z - mu
        var = jnp.mean(zc * zc, axis=-1, keepdims=True)
        o_ref[rows, :] = zc * lax.rsqrt(var + LN_EPS) * g_ref[...] + b_ref[...]
        return 0

    lax.fori_loop(0, o_ref.shape[0] // LN_ROWS, ln_rows, 0)


def _out_ln(merged, w_out, x2, ln_g, ln_b, alpha):
    T, D = x2.shape
    tm = min(OUT_ROWS, T)
    return pl.pallas_call(
        functools.partial(_out_ln_kernel, alpha=alpha),
        out_shape=jax.ShapeDtypeStruct((T, D), x2.dtype),
        grid=(T // tm,),
        in_specs=[
            pl.BlockSpec((tm, D), lambda i: (i, 0)),
            pl.BlockSpec((D, D), lambda i: (0, 0), pipeline_mode=pl.Buffered(1)),
            pl.BlockSpec((tm, D), lambda i: (i, 0)),
            pl.BlockSpec((1, D), lambda i: (0, 0)),
            pl.BlockSpec((1, D), lambda i: (0, 0)),
        ],
        out_specs=pl.BlockSpec((tm, D), lambda i: (i, 0)),
        compiler_params=_params("parallel"),
        name="out_proj_layernorm",
    )(merged, w_out, x2, ln_g.reshape(1, D), ln_b.reshape(1, D))


def _layer(x, mem, w_in, b_merge, gla_w_a2, gla_b_a, gla_norm_g, nsa_pe_k, nsa_pe_v, nsa_wk1, nsa_wk2,
           nsa_wv1, nsa_wv2, w_mem_kv, w_br_gla, w_br_nsa, w_br_mem, w_out, ln_g, ln_b, depth):
    B, S, D = x.shape
    T = B * S
    bw = D // 2
    gk = bw // 2
    G, R = NSA_GROUPS, NSA_REP
    HD = bw // NSA_HEADS
    kvw = G * HD
    cdt = MXU_DTYPE

    o_ga = 2 * gk + 2 * bw
    o_nq = o_ga + GLA_LOWRANK
    o_nbg = o_nq + bw + 6 * kvw + bw
    o_mq = o_nbg + 3 * NSA_HEADS
    o_mrg = o_mq + 2 * bw
    assert w_in.shape[1] == o_mrg + N_BRANCH * D

    x2 = x.reshape(T, D)
    ones = lambda n: jnp.ones((n,), f32)

    w_t = jnp.swapaxes(w_in, 0, 1)
    h_small, xb = _project_small(x2, w_t, o_ga, GLA_LOWRANK, o_nbg, 3 * NSA_HEADS)
    o_nkv, o_nz = o_nq + bw, o_nq + bw + 6 * kvw
    pbn = 1024
    _, w_blocks = _packed_blocks([(0, o_ga), (o_mq, 2 * bw), (o_nq, bw), (o_nz, bw), (o_nkv, 6 * kvw),
                                  (o_mrg, N_BRANCH * D)], w_t.shape[0], pbn)
    c_gla, c_mem = 0, o_ga
    c_nq = c_mem + 2 * bw
    c_nz = c_nq + bw
    c_nkv = c_nz + bw
    c_mrg = c_nkv + 6 * kvw
    dk, mhd = gk // GLA_HEADS, bw // MEM_HEADS
    scale = jnp.concatenate([jnp.full((gk,), dk ** -0.5, f32), ones(o_ga - gk),
                             jnp.full((bw,), mhd ** -0.5, f32), ones(bw),
                             jnp.full((bw,), HD ** -0.5 * LOG2E, f32), ones(bw)])
    h = _project_cast(xb, w_t, w_blocks, 0, c_nkv // pbn, scale, cdt, name="proj_in")
    kvh = _project_cast(xb, w_t, w_blocks, c_nkv // pbn, 6 * kvw // pbn, ones(6 * kvw), cdt, grouped=True,
                        name="proj_nsa_kv")
    a = _project_cast(xb, w_t, w_blocks, c_mrg // pbn, N_BRANCH * D // pbn, b_merge, cdt, gate=True,
                      name="proj_merge_gates")

    wa_pad = jnp.concatenate([gla_w_a2, jnp.zeros((SMALL_W - GLA_LOWRANK, gk), f32)], axis=0)
    o_gla = _gla(h, c_gla, gk, h_small, wa_pad, gla_b_a, gla_norm_g, B, S)

    NC = S // CMP_STRIDE
    blocks = lambda k: kvh[k * G:(k + 1) * G].reshape(G, B, NC, CMP_STRIDE * HD)
    k_cmp = _compress(blocks(0), nsa_pe_k, nsa_wk1, nsa_wk2, False)
    v_cmp_t = _compress(blocks(1), nsa_pe_v, nsa_wv1, nsa_wv2, True)
    gl = h_small[:, GLA_LOWRANK:GLA_LOWRANK + 3 * NSA_HEADS].reshape(T, G, 3 * R).transpose(1, 2, 0)
    gates_t = jnp.concatenate([gl, jnp.zeros((G, GATE_ROWS - 3 * R, T), f32)], axis=1)
    o_nsa = _nsa_attention(h, c_nq, c_nz, bw, kvh, k_cmp, v_cmp_t, gates_t, B, S)

    M = mem.shape[1]
    kv = _project(mem.reshape(B * M, D).astype(cdt), w_mem_kv.astype(cdt), ones(2 * bw), cdt, name="proj_mem_kv")
    o_mem = _mem_attention(h, c_mem, bw, kv, B, S)

    merged = _merge(o_gla, o_nsa, o_mem, w_br_gla.astype(cdt), w_br_nsa.astype(cdt), w_br_mem.astype(cdt), a, 0)
    alpha = (2 * depth) ** 0.25
    return _out_ln(merged, w_out.astype(cdt), x2, ln_g, ln_b, alpha).reshape(B, S, D)


def kernel(x, mem, w_in, b_merge, gla_w_a2, gla_b_a, gla_norm_g, nsa_pe_k, nsa_pe_v, nsa_wk1, nsa_wk2, nsa_wv1, nsa_wv2, w_mem_kv, w_br_gla, w_br_nsa, w_br_mem, w_out, ln_g, ln_b):
    depth = w_in.shape[0]
    for l in range(depth):
        x = _layer(x, mem, w_in[l], b_merge[l], gla_w_a2[l], gla_b_a[l], gla_norm_g[l], nsa_pe_k[l], nsa_pe_v[l],
                   nsa_wk1[l], nsa_wk2[l], nsa_wv1[l], nsa_wv2[l], w_mem_kv[l], w_br_gla[l], w_br_nsa[l],
                   w_br_mem[l], w_out[l], ln_g[l], ln_b[l], depth)
    return x
```
